```python
import jax, jax.numpy as jnp
from jax import lax
import numpy as np


D_MODEL = 1024
BATCH = 2
SEQ = 8192
DEPTH = 1

ATTN_HEADS = 16
HEAD_DIM = 64
ATTN_WIDTH = ATTN_HEADS * HEAD_DIM
DILATED_PATTERNS = ((128, 1), (512, 4), (2048, 16))
SSD_HEADS = 16
SSD_HEAD_DIM = 64
SSD_WIDTH = SSD_HEADS * SSD_HEAD_DIM
SSD_GROUPS = 2
SSD_HEADS_PER_GROUP = SSD_HEADS // SSD_GROUPS
SSD_STATE = 128
SSD_CHUNK = 128
CONV_WIDTH = 5
XBC_WIDTH = SSD_WIDTH + 2 * SSD_GROUPS * SSD_STATE
MIX_WIDTH = ATTN_WIDTH + SSD_WIDTH
IN_PROJ_WIDTH = 3 * ATTN_WIDTH + SSD_WIDTH + XBC_WIDTH + 2 * SSD_HEADS
N_EXPERTS = 32
TOP_K = 4
EXPERT_FF = 1024
SWIGLU_ALPHA = 1.702
SWIGLU_LIMIT = 7.0
RMS_EPS = 1e-6
N_MOD = 6

kernel_name = 'hybrid_dilated_attn_ssd_moe_block'


def _rmsnorm(t, g):
    var = jnp.mean(t * t, axis=-1, keepdims=True)
    return t * lax.rsqrt(var + RMS_EPS) * g


def _alibi_slopes(n):
    return jnp.exp2(-8.0 * jnp.arange(1, n + 1, dtype=jnp.float32) / n)


def _dilated_window_attention(q, k, v, slopes, window, dilation):
    bsz, nh, seq, hd = q.shape
    half = window // (2 * dilation)
    blk = half
    sub_len = seq // dilation
    nb = -(-sub_len // blk)
    pad_len = nb * blk

    def to_sub(t):
        t = t.reshape(bsz, nh, sub_len, dilation, hd).transpose(0, 1, 3, 2, 4)
        return jnp.pad(t, ((0, 0), (0, 0), (0, 0), (0, pad_len - sub_len), (0, 0)))

    qs, ks, vs = to_sub(q), to_sub(k), to_sub(v)
    qb = qs.reshape(bsz, nh, dilation, nb, blk, hd)

    def band(t):
        tp = jnp.pad(t, ((0, 0), (0, 0), (0, 0), (blk, blk), (0, 0)))
        tp = tp.reshape(bsz, nh, dilation, nb + 2, blk, hd)
        return jnp.concatenate([tp[:, :, :, 0:nb], tp[:, :, :, 1:nb + 1], tp[:, :, :, 2:nb + 2]], axis=4)

    kb, vb = band(ks), band(vs)
    scores = jnp.einsum('bhrnqe,bhrnke->bhrnqk', qb, kb)
    q_idx = jnp.arange(nb)[:, None] * blk + jnp.arange(blk)[None, :]
    k_idx = jnp.arange(nb)[:, None] * blk - blk + jnp.arange(3 * blk)[None, :]
    rel = k_idx[:, None, :] - q_idx[:, :, None]
    valid = (jnp.abs(rel) <= half) & (k_idx >= 0)[:, None, :] & (k_idx < sub_len)[:, None, :]
    dist = (jnp.abs(rel) * dilation).astype(jnp.float32)
    scores = scores - slopes[None, :, None, None, None, None] * dist
    scores = jnp.where(valid, scores, -jnp.inf)
    m = jnp.max(scores, axis=-1, keepdims=True)
    p = jnp.exp(scores - m)
    den = jnp.sum(p, axis=-1, keepdims=True)
    o = jnp.einsum('bhrnqk,bhrnke->bhrnqe', p, vb) / den
    lse = (m + jnp.log(den))[..., 0]

    def from_sub(t):
        t = t.reshape(bsz, nh, dilation, pad_len, *t.shape[5:])[:, :, :, :sub_len]
        t = jnp.swapaxes(t, 2, 3)
        return t.reshape(bsz, nh, seq, *t.shape[4:])

    return from_sub(o), from_sub(lse)


def _dilated_mixture_attention(q, k, v):
    slopes = _alibi_slopes(ATTN_HEADS)
    outs, lses = [], []
    for window, dilation in DILATED_PATTERNS:
        o, l = _dilated_window_attention(q, k, v, slopes, window, dilation)
        outs.append(o)
        lses.append(l)
    w = jax.nn.softmax(jnp.stack(lses), axis=0)
    return jnp.einsum('gbhs,gbhse->bhse', w, jnp.stack(outs))


def _ssd_chunked(xs, dt, a, bm, cm):
    bsz, seq = xs.shape[0], xs.shape[1]
    qn = SSD_CHUNK
    nc = seq // qn
    g, e, pdim, n = SSD_GROUPS, SSD_HEADS_PER_GROUP, SSD_HEAD_DIM, SSD_STATE
    xd = (xs * dt[..., None]).reshape(bsz, nc, qn, g, e, pdim)
    a_dt = jnp.moveaxis((dt * a).reshape(bsz, nc, qn, g, e), 2, -1)
    bc = bm.reshape(bsz, nc, qn, g, n)
    cc = cm.reshape(bsz, nc, qn, g, n)
    a_cum = jnp.cumsum(a_dt, axis=-1)
    tril = jnp.tril(jnp.ones((qn, qn), dtype=bool))
    seg = a_cum[..., :, None] - a_cum[..., None, :]
    decay_in = jnp.exp(jnp.where(tril, seg, -jnp.inf))
    cb = jnp.einsum('bclgn,bcsgn->bcgls', cc, bc)
    y_diag = jnp.einsum('bcgels,bcsgep->bclgep', cb[:, :, :, None] * decay_in, xd)
    decay_states = jnp.exp(a_cum[..., -1:] - a_cum)
    xw = xd * jnp.moveaxis(decay_states, -1, 2)[..., None]
    states = jnp.einsum('bcsgn,bcsgep->bcgepn', bc, xw)
    chunk_decay = jnp.exp(a_cum[..., -1])

    def step(h, inp):
        st, dec = inp
        return h * dec[..., None, None] + st, h

    h0 = jnp.zeros((bsz, g, e, pdim, n), dtype=xs.dtype)
    _, h_prev = lax.scan(step, h0, (jnp.moveaxis(states, 1, 0), jnp.moveaxis(chunk_decay, 1, 0)))
    h_prev = jnp.moveaxis(h_prev, 0, 1)
    y_off = jnp.einsum('bclgn,bcgepn->bclgep', cc, h_prev) * jnp.moveaxis(jnp.exp(a_cum), -1, 2)[..., None]
    return (y_diag + y_off).reshape(bsz, seq, g, e, pdim)


def _token_mixer(h, w_in, conv_w, conv_b, dt_bias, a_log, d_skip, g_ssd_norm, w_out):
    bsz, seq, _ = h.shape
    proj = h @ w_in
    cuts = [ATTN_WIDTH, 2 * ATTN_WIDTH, 3 * ATTN_WIDTH, 3 * ATTN_WIDTH + SSD_WIDTH,
            3 * ATTN_WIDTH + SSD_WIDTH + XBC_WIDTH]
    q, k, v, z, xbc, dt_raw = jnp.split(proj, cuts, axis=-1)

    def heads(t):
        return t.reshape(bsz, seq, ATTN_HEADS, HEAD_DIM).transpose(0, 2, 1, 3)

    attn = _dilated_mixture_attention(heads(q) * HEAD_DIM ** -0.5, heads(k), heads(v))
    attn = attn.transpose(0, 2, 1, 3).reshape(bsz, seq, ATTN_WIDTH)

    pad = CONV_WIDTH // 2
    xbc_p = jnp.pad(xbc, ((0, 0), (pad, pad), (0, 0)))
    conv = conv_b
    for i in range(CONV_WIDTH):
        conv = conv + xbc_p[:, i:i + seq] * conv_w[i]
    xbc = jax.nn.silu(conv)
    gh = (SSD_GROUPS, SSD_HEADS_PER_GROUP)
    xs = xbc[..., :SSD_WIDTH].reshape(bsz, seq, SSD_GROUPS, SSD_HEADS_PER_GROUP, SSD_HEAD_DIM)
    bm = xbc[..., SSD_WIDTH:SSD_WIDTH + SSD_GROUPS * SSD_STATE].reshape(bsz, seq, SSD_GROUPS, SSD_STATE)
    cm = xbc[..., SSD_WIDTH + SSD_GROUPS * SSD_STATE:].reshape(bsz, seq, SSD_GROUPS, SSD_STATE)
    dt = jax.nn.softplus(dt_raw.reshape(bsz, seq, 2, SSD_HEADS) + dt_bias)
    a = -jnp.exp(a_log)

    def flip(t):
        return jnp.flip(t, axis=1)

    y_fwd = _ssd_chunked(xs, dt[:, :, 0].reshape(bsz, seq, *gh), a[0].reshape(gh), bm, cm)
    y_bwd = flip(_ssd_chunked(flip(xs), flip(dt[:, :, 1].reshape(bsz, seq, *gh)),
                              a[1].reshape(gh), flip(bm), flip(cm)))
    y = y_fwd + y_bwd + d_skip.reshape(gh)[:, :, None] * xs
    y = y.reshape(bsz, seq, SSD_WIDTH)
    y = _rmsnorm(y * jax.nn.silu(z), g_ssd_norm)

    return jnp.concatenate([attn, y], axis=-1) @ w_out


def _moe(h, w_router, b_router, w_gate_up, b_gate_up, w_down, b_down):
    bsz, seq, dm = h.shape
    xt = h.reshape(bsz * seq, dm)
    logits = xt @ w_router + b_router
    top_val, top_idx = lax.top_k(logits, TOP_K)
    top_w = jax.nn.softmax(top_val, axis=-1)
    combine = jnp.sum(jax.nn.one_hot(top_idx, N_EXPERTS, dtype=xt.dtype) * top_w[..., None], axis=1)
    out = jnp.zeros_like(xt)
    for e in range(N_EXPERTS):
        gu = xt @ w_gate_up[e] + b_gate_up[e]
        gate = jnp.minimum(gu[:, :EXPERT_FF], SWIGLU_LIMIT)
        up = jnp.clip(gu[:, EXPERT_FF:], -SWIGLU_LIMIT, SWIGLU_LIMIT)
        act = (up + 1.0) * gate * jax.nn.sigmoid(SWIGLU_ALPHA * gate)
        out = out + combine[:, e:e + 1] * (act @ w_down[e] + b_down[e])
    return out.reshape(bsz, seq, dm)


def setup_inputs(seed: int = 0) -> dict:
    key = jax.random.key(seed)
    ks = jax.random.split(key, 24)
    f32 = jnp.float32

    def nrm(k, shape, scale):
        return jax.random.normal(k, shape, f32) * scale

    L = DEPTH
    x = nrm(ks[0], (BATCH, SEQ, D_MODEL), 1.0)
    c = nrm(ks[1], (BATCH, D_MODEL), 1.0)
    w_ada = nrm(ks[2], (L, D_MODEL, N_MOD * D_MODEL), 0.5 * D_MODEL ** -0.5)
    b_ada = nrm(ks[3], (L, N_MOD * D_MODEL), 0.02)
    g_pre_mix = 1.0 + nrm(ks[4], (L, D_MODEL), 0.05)
    g_post_mix = 1.0 + nrm(ks[5], (L, D_MODEL), 0.05)
    w_in_main = nrm(ks[6], (L, D_MODEL, IN_PROJ_WIDTH - 2 * SSD_HEADS), D_MODEL ** -0.5)
    w_in_dt = nrm(ks[7], (L, D_MODEL, 2 * SSD_HEADS), 0.1 * D_MODEL ** -0.5)
    w_in = jnp.concatenate([w_in_main, w_in_dt], axis=-1)
    conv_w = nrm(ks[8], (L, CONV_WIDTH, XBC_WIDTH), CONV_WIDTH ** -0.5)
    conv_b = nrm(ks[9], (L, XBC_WIDTH), 0.02)
    dt0 = jnp.exp(jax.random.uniform(ks[10], (L, 2, SSD_HEADS), f32,
                                     minval=float(np.log(1e-3)), maxval=float(np.log(1e-1))))
    dt_bias = dt0 + jnp.log(-jnp.expm1(-dt0))
    a_log = jnp.log(jax.random.uniform(ks[11], (L, 2, SSD_HEADS), f32, minval=1.0, maxval=16.0))
    d_skip = 1.0 + nrm(ks[12], (L, SSD_HEADS), 0.1)
    g_ssd_norm = 1.0 + nrm(ks[13], (L, SSD_WIDTH), 0.05)
    w_out = nrm(ks[14], (L, MIX_WIDTH, D_MODEL), MIX_WIDTH ** -0.5)
    g_pre_ffn = 1.0 + nrm(ks[15], (L, D_MODEL), 0.05)
    g_post_ffn = 1.0 + nrm(ks[16], (L, D_MODEL), 0.05)
    w_router = nrm(ks[17], (L, D_MODEL, N_EXPERTS), D_MODEL ** -0.5)
    b_router = nrm(ks[18], (L, N_EXPERTS), 0.01)
    w_gate_up = nrm(ks[19], (L, N_EXPERTS, D_MODEL, 2 * EXPERT_FF), D_MODEL ** -0.5)
    b_gate_up = nrm(ks[20], (L, N_EXPERTS, 2 * EXPERT_FF), 0.02)
    w_down = nrm(ks[21], (L, N_EXPERTS, EXPERT_FF, D_MODEL), EXPERT_FF ** -0.5)
    b_down = nrm(ks[22], (L, N_EXPERTS, D_MODEL), 0.02)
    return {'x': x, 'c': c, 'w_ada': w_ada, 'b_ada': b_ada,
            'g_pre_mix': g_pre_mix, 'g_post_mix': g_post_mix, 'w_in': w_in,
            'conv_w': conv_w, 'conv_b': conv_b, 'dt_bias': dt_bias, 'a_log': a_log,
            'd_skip': d_skip, 'g_ssd_norm': g_ssd_norm, 'w_out': w_out,
            'g_pre_ffn': g_pre_ffn, 'g_post_ffn': g_post_ffn,
            'w_router': w_router, 'b_router': b_router, 'w_gate_up': w_gate_up,
            'b_gate_up': b_gate_up, 'w_down': w_down, 'b_down': b_down}


def reference(x, c, w_ada, b_ada, g_pre_mix, g_post_mix, w_in, conv_w, conv_b, dt_bias,
              a_log, d_skip, g_ssd_norm, w_out, g_pre_ffn, g_post_ffn, w_router, b_router,
              w_gate_up, b_gate_up, w_down, b_down):
    f32 = jnp.float32
    out_dtype = x.dtype
    res = x.astype(f32)
    cs = jax.nn.silu(c.astype(f32))
    for layer in range(DEPTH):
        def p(t):
            return t[layer].astype(f32)

        mod = cs @ p(w_ada) + p(b_ada)
        shift_m, scale_m, gate_m, shift_f, scale_f, gate_f = [t[:, None, :] for t in jnp.split(mod, N_MOD, axis=-1)]
        h = _rmsnorm(res, p(g_pre_mix)) * (1.0 + scale_m) + shift_m
        mix = _token_mixer(h, p(w_in), p(conv_w), p(conv_b), p(dt_bias), p(a_log),
                           p(d_skip), p(g_ssd_norm), p(w_out))
        res = res + gate_m * _rmsnorm(mix, p(g_post_mix))
        h = _rmsnorm(res, p(g_pre_ffn)) * (1.0 + scale_f) + shift_f
        ffn = _moe(h, p(w_router), p(b_router), p(w_gate_up), p(b_gate_up), p(w_down), p(b_down))
        res = res + gate_f * _rmsnorm(ffn, p(g_post_ffn))
    return res.astype(out_dtype)
```

```python
import functools

import jax
import jax.numpy as jnp
from jax import lax
from jax.experimental import pallas as pl
from jax.experimental.pallas import tpu as pltpu

F32 = jnp.float32
BF16 = jnp.bfloat16

D_MODEL = 1024
ATTN_HEADS = 16
HEAD_DIM = 64
ATTN_WIDTH = ATTN_HEADS * HEAD_DIM
DILATIONS = (1, 4, 16)
ATTN_HALF = 64
SSD_HEADS = 16
SSD_HEAD_DIM = 64
SSD_WIDTH = SSD_HEADS * SSD_HEAD_DIM
SSD_GROUPS = 2
SSD_STATE = 128
SSD_CHUNK = 128
CONV_WIDTH = 5
XBC_WIDTH = SSD_WIDTH + 2 * SSD_GROUPS * SSD_STATE
N_EXPERTS = 32
TOP_K = 4
EXPERT_FF = 1024
SWIGLU_ALPHA = 1.702
SWIGLU_LIMIT = 7.0
RMS_EPS = 1e-6
N_MOD = 6
LANES = 128
SUBLANES = 8
VMEM_LIMIT = 56 * 1024 * 1024


def _cparams(*sem):
    return pltpu.CompilerParams(dimension_semantics=sem, vmem_limit_bytes=VMEM_LIMIT)


def _const_spec(shape):
    nd = len(shape)
    return pl.BlockSpec(shape, lambda *_: (0,) * nd)


def _split2(a):
    hi = a.astype(BF16)
    lo = (a - hi.astype(F32)).astype(BF16)
    return hi, lo


def _dot(a, b):
    return jnp.dot(a, b, preferred_element_type=F32)


def _dot_f32(a, b):
    ah, al = _split2(a)
    bh, bl = _split2(b)
    return _dot(ah, bh) + (_dot(ah, bl) + _dot(al, bh))


def _sigmoid(x):
    return 1.0 / (1.0 + jnp.exp(-x))


def _rms(x):
    return x * lax.rsqrt(jnp.mean(x * x, axis=-1, keepdims=True) + RMS_EPS)


def _ada_kernel(c_ref, w_ref, b_ref, o_ref):
    c = c_ref[...]
    o_ref[...] = _dot_f32(c * _sigmoid(c), w_ref[...]) + b_ref[...]


def _ada_mod(c, w_ada, b_ada):
    bsz = c.shape[0]
    c8 = jnp.zeros((SUBLANES, D_MODEL), F32).at[:bsz].set(c)
    out = pl.pallas_call(
        _ada_kernel,
        grid=(N_MOD,),
        in_specs=[_const_spec((SUBLANES, D_MODEL)),
                  pl.BlockSpec((D_MODEL, D_MODEL), lambda j: (0, j)),
                  pl.BlockSpec((1, D_MODEL), lambda j: (0, j))],
        out_specs=pl.BlockSpec((SUBLANES, D_MODEL), lambda j: (0, j)),
        out_shape=jax.ShapeDtypeStruct((SUBLANES, N_MOD * D_MODEL), F32),
        compiler_params=_cparams("parallel"),
        name="ada_mod",
    )(c8, w_ada, b_ada.reshape(1, -1))
    return out[:bsz].reshape(bsz, N_MOD, D_MODEL)


def _inproj_kernel(x_ref, xp_ref, xn_ref, mod_ref, g_ref, wqkv_ref, wz_ref, wxbc_ref, wdt_ref,
                   cw_ref, cb_ref, dtb_ref,
                   q_ref, k_ref, v_ref, z_ref, xc_ref, dt_ref, buf_ref, *, tm, tiles_per_seq):
    i = pl.program_id(0)
    shift = mod_ref[0, 0:1, :]
    scale = mod_ref[0, 1:2, :]
    g = g_ref[...]

    def norm_mod(x):
        return _rms(x) * g * (1.0 + scale) + shift

    h = norm_mod(x_ref[...])
    hb = h.astype(BF16)
    q_ref[...] = (_dot(hb, wqkv_ref[:, 0:ATTN_WIDTH]) * HEAD_DIM ** -0.5).astype(BF16)
    k_ref[...] = _dot(hb, wqkv_ref[:, ATTN_WIDTH:2 * ATTN_WIDTH]).astype(BF16)
    v_ref[...] = _dot(hb, wqkv_ref[:, 2 * ATTN_WIDTH:3 * ATTN_WIDTH]).astype(BF16)
    z_ref[...] = _dot(hb, wz_ref[...]).astype(BF16)
    dt_ref[...] = jax.nn.softplus(_dot_f32(h, wdt_ref[...]) + dtb_ref[...])

    hh = norm_mod(jnp.concatenate([xp_ref[...], xn_ref[...]], axis=0)).astype(BF16)
    halo = _dot(hh, wxbc_ref[...])
    t_in_seq = i % tiles_per_seq
    prev_ok = (t_in_seq != 0).astype(F32)
    next_ok = (t_in_seq != tiles_per_seq - 1).astype(F32)
    buf_ref[0:SUBLANES, :] = halo[0:SUBLANES] * prev_ok
    buf_ref[SUBLANES + tm:, :] = halo[SUBLANES:] * next_ok
    buf_ref[SUBLANES:SUBLANES + tm, :] = _dot(hb, wxbc_ref[...])
    acc = cb_ref[...]
    for j in range(CONV_WIDTH):
        off = SUBLANES - CONV_WIDTH // 2 + j
        acc = acc + buf_ref[off:off + tm, :] * cw_ref[j:j + 1, :]
    xc_ref[...] = (acc * _sigmoid(acc)).astype(BF16)


def _in_proj(x2, mod, g_pre, w_in, conv_w, conv_b, dt_bias, *, seq, tm=512):
    t_total = x2.shape[0]
    nt = t_total // tm
    tiles_per_seq = seq // tm
    hb = tm // SUBLANES
    n_hblk = t_total // SUBLANES
    qkv_w = 3 * ATTN_WIDTH
    w_qkv = w_in[:, :qkv_w].astype(BF16)
    w_z = w_in[:, qkv_w:qkv_w + SSD_WIDTH].astype(BF16)
    w_xbc = w_in[:, qkv_w + SSD_WIDTH:qkv_w + SSD_WIDTH + XBC_WIDTH].astype(BF16)
    w_dt_raw = w_in[:, qkv_w + SSD_WIDTH + XBC_WIDTH:]
    w_dt = jnp.zeros((D_MODEL, 2 * LANES), F32)
    dtb = jnp.zeros((1, 2 * LANES), F32)
    for dr in range(2):
        w_dt = w_dt.at[:, dr * LANES:dr * LANES + SSD_HEADS].set(w_dt_raw[:, dr * SSD_HEADS:(dr + 1) * SSD_HEADS])
        dtb = dtb.at[0, dr * LANES:dr * LANES + SSD_HEADS].set(dt_bias[dr])
    row = lambda i: (i, 0)
    out_shape = [jax.ShapeDtypeStruct((t_total, ATTN_WIDTH), BF16)] * 3 + [
        jax.ShapeDtypeStruct((t_total, SSD_WIDTH), BF16),
        jax.ShapeDtypeStruct((t_total, XBC_WIDTH), BF16),
        jax.ShapeDtypeStruct((t_total, 2 * LANES), F32)]
    return pl.pallas_call(
        functools.partial(_inproj_kernel, tm=tm, tiles_per_seq=tiles_per_seq),
        grid=(nt,),
        in_specs=[pl.BlockSpec((tm, D_MODEL), row),
                  pl.BlockSpec((SUBLANES, D_MODEL), lambda i: (jnp.maximum(i * hb - 1, 0), 0)),
                  pl.BlockSpec((SUBLANES, D_MODEL), lambda i: (jnp.minimum((i + 1) * hb, n_hblk - 1), 0)),
                  pl.BlockSpec((1, N_MOD, D_MODEL), lambda i: (i // tiles_per_seq, 0, 0)),
                  _const_spec((1, D_MODEL)),
                  _const_spec((D_MODEL, qkv_w)),
                  _const_spec((D_MODEL, SSD_WIDTH)),
                  _const_spec((D_MODEL, XBC_WIDTH)),
                  _const_spec((D_MODEL, 2 * LANES)),
                  _const_spec((CONV_WIDTH, XBC_WIDTH)),
                  _const_spec((1, XBC_WIDTH)),
                  _const_spec((1, 2 * LANES))],
        out_specs=[pl.BlockSpec((tm, ATTN_WIDTH), row)] * 3 + [
            pl.BlockSpec((tm, SSD_WIDTH), row),
            pl.BlockSpec((tm, XBC_WIDTH), row),
            pl.BlockSpec((tm, 2 * LANES), row)],
        out_shape=out_shape,
        scratch_shapes=[pltpu.VMEM((tm + 2 * SUBLANES, XBC_WIDTH), F32)],
        compiler_params=_cparams("parallel"),
        name="in_proj",
    )(x2, x2, x2, mod, g_pre.reshape(1, -1), w_qkv, w_z, w_xbc, w_dt, conv_w, conv_b.reshape(1, -1), dtb)


_NEG = -1e30


def _ssd_direction(xc_ref, dt_ref, a_ref, e_ref, h_ref, y_ref, reverse):
    qn = SSD_CHUNK
    gw = SSD_WIDTH // SSD_GROUPS
    li = lax.broadcasted_iota(jnp.int32, (qn, qn), 0)
    si = lax.broadcasted_iota(jnp.int32, (qn, qn), 1)
    mask = (si >= li) if reverse else (si <= li)
    tri = jnp.where(mask, 1.0, 0.0).astype(BF16)
    dt = dt_ref[...]
    adt = dt * a_ref[...]
    p0 = adt.astype(BF16)
    r0 = adt - p0.astype(F32)
    p1 = r0.astype(BF16)
    p2 = (r0 - p1.astype(F32)).astype(BF16)
    acum = _dot(tri, p0) + (_dot(tri, p1) + _dot(tri, p2))
    last = 0 if reverse else qn - 1
    eo = jnp.exp(acum)
    ds = jnp.exp(acum[last:last + 1, :] - acum)
    sh, sl = _split2(jnp.concatenate([dt, ds, eo], axis=0))
    ex = _dot(sh, e_ref[...]) + _dot(sl, e_ref[...])
    dt_x, ds_x, eo_x = ex[0:qn], ex[qn:2 * qn], ex[2 * qn:3 * qn]
    xd = xc_ref[:, 0:SSD_WIDTH].astype(F32) * dt_x
    xdb = xd.astype(BF16)
    xwb = (xd * ds_x).astype(BF16)
    acum_t = acum.T
    lane = lax.broadcasted_iota(jnp.int32, (qn, LANES), 1)
    for g in range(SSD_GROUPS):
        b0 = SSD_WIDTH + g * SSD_STATE
        c0 = SSD_WIDTH + SSD_GROUPS * SSD_STATE + g * SSD_STATE
        bm = xc_ref[:, b0:b0 + SSD_STATE]
        cm = xc_ref[:, c0:c0 + SSD_STATE]
        cb = lax.dot_general(cm, bm, (((1,), (1,)), ((), ())), preferred_element_type=F32)
        hg = h_ref[g]
        yoff = _dot(cm, hg.astype(BF16))
        st = lax.dot_general(bm, xwb[:, g * gw:(g + 1) * gw], (((0,), (0,)), ((), ())),
                             preferred_element_type=F32)
        h_ref[g] = hg * eo_x[last:last + 1, g * gw:(g + 1) * gw] + st
        for pr in range(gw // LANES):
            col = g * gw + pr * LANES
            xp = xdb[:, col:col + LANES]
            res = []
            for e in (col // SSD_HEAD_DIM, col // SSD_HEAD_DIM + 1):
                seg = acum[:, e:e + 1] - acum_t[e:e + 1, :]
                m = (cb * jnp.exp(jnp.where(mask, seg, _NEG))).astype(BF16)
                res.append(_dot(m, xp))
            yd = jnp.where(lane < SSD_HEAD_DIM, res[0], res[1])
            y = yd + yoff[:, pr * LANES:(pr + 1) * LANES] * eo_x[:, col:col + LANES]
            y_ref[:, col:col + LANES] = y.astype(BF16)


def _ssd_kernel(xcf_ref, xcb_ref, dtf_ref, dtb_ref, a_ref, e_ref, yf_ref, yb_ref, hf_ref, hb_ref):
    @pl.when(pl.program_id(1) == 0)
    def _():
        hf_ref[...] = jnp.zeros_like(hf_ref)
        hb_ref[...] = jnp.zeros_like(hb_ref)

    _ssd_direction(xcf_ref, dtf_ref, a_ref.at[0], e_ref, hf_ref, yf_ref, False)
    _ssd_direction(xcb_ref, dtb_ref, a_ref.at[1], e_ref, hb_ref, yb_ref, True)


def _ssd(xc, dt, a_log, *, seq):
    t_total = xc.shape[0]
    bsz = t_total // seq
    nc = seq // SSD_CHUNK
    a_rows = jnp.zeros((2, 1, LANES), F32).at[:, 0, :SSD_HEADS].set(-jnp.exp(a_log))
    expand = (jnp.arange(LANES)[:, None] == jnp.arange(SSD_WIDTH)[None, :] // SSD_HEAD_DIM).astype(BF16)
    fwd = lambda b, c: (b * nc + c, 0)
    bwd = lambda b, c: (b * nc + nc - 1 - c, 0)
    state = pltpu.VMEM((SSD_GROUPS, SSD_STATE, SSD_WIDTH // SSD_GROUPS), F32)
    return pl.pallas_call(
        _ssd_kernel,
        grid=(bsz, nc),
        in_specs=[pl.BlockSpec((SSD_CHUNK, XBC_WIDTH), fwd),
                  pl.BlockSpec((SSD_CHUNK, XBC_WIDTH), bwd),
                  pl.BlockSpec((SSD_CHUNK, LANES), fwd),
                  pl.BlockSpec((SSD_CHUNK, LANES), lambda b, c: (b * nc + nc - 1 - c, 1)),
                  _const_spec((2, 1, LANES)),
                  _const_spec((LANES, SSD_WIDTH))],
        out_specs=[pl.BlockSpec((SSD_CHUNK, SSD_WIDTH), fwd),
                   pl.BlockSpec((SSD_CHUNK, SSD_WIDTH), bwd)],
        out_shape=[jax.ShapeDtypeStruct((t_total, SSD_WIDTH), BF16)] * 2,
        scratch_shapes=[state, state],
        compiler_params=_cparams("parallel", "arbitrary"),
        name="ssd",
    )(xc, xc, dt, dt, a_rows, expand)


_ATTN_WIN = 3 * ATTN_HALF
_ATTN_LQ = 512


def _attn_kernel(q_ref, k_ref, kp_ref, kn_ref, v_ref, vp_ref, vn_ref, bias_ref, o_ref, lse_ref,
                 kw_ref, vw_ref, *, lq, sub_len):
    t = pl.program_id(2)
    hb = ATTN_HALF
    for src, halo_p, halo_n, win in ((k_ref, kp_ref, kn_ref, kw_ref), (v_ref, vp_ref, vn_ref, vw_ref)):
        win[0:hb, :] = halo_p[...]
        win[hb:hb + lq, :] = src[...]
        win[hb + lq:, :] = halo_n[...]
    lane = lax.broadcasted_iota(jnp.int32, (hb, LANES), 1)
    first_head = lane < HEAD_DIM
    kpos = lax.broadcasted_iota(jnp.int32, (1, _ATTN_WIN), 1)
    for hp in range(ATTN_HEADS // 2):
        cs = slice(hp * LANES, (hp + 1) * LANES)
        bias = bias_ref[hp]

        def body(qb, carry, cs=cs, bias=bias, hp=hp):
            r0 = pl.multiple_of(qb * hb, hb)
            q = q_ref[pl.ds(r0, hb), cs]
            zero = jnp.zeros_like(q)
            q2 = jnp.concatenate([jnp.where(first_head, q, zero), jnp.where(first_head, zero, q)], axis=0)
            s = lax.dot_general(q2, kw_ref[pl.ds(r0, _ATTN_WIN), cs], (((1,), (1,)), ((), ())),
                                preferred_element_type=F32) + bias
            kidx = t * lq + r0 - hb + kpos
            s = jnp.where((kidx >= 0) & (kidx < sub_len), s, _NEG)
            m = jnp.max(s, axis=-1, keepdims=True)
            p = jnp.exp(s - m)
            den = jnp.sum(p, axis=-1, keepdims=True)
            pv = _dot(p.astype(BF16), vw_ref[pl.ds(r0, _ATTN_WIN), cs]) * (1.0 / den)
            o_ref[pl.ds(r0, hb), cs] = jnp.where(first_head, pv[0:hb], pv[hb:]).astype(BF16)
            lse = m + jnp.log(den)
            prev = jnp.zeros((hb, LANES), F32) if hp == 0 else lse_ref[pl.ds(r0, hb), :]
            lse_ref[pl.ds(r0, hb), :] = jnp.where(lane == 2 * hp, lse[0:hb],
                                                  jnp.where(lane == 2 * hp + 1, lse[hb:], prev))
            return carry

        lax.fori_loop(0, lq // hb, body, 0, unroll=2)


def _attn_bias(dilation):
    slopes = jnp.exp2(-8.0 * jnp.arange(1, ATTN_HEADS + 1, dtype=F32) / ATTN_HEADS)
    rel = jnp.abs(jnp.arange(_ATTN_WIN)[None, :] - ATTN_HALF - jnp.arange(ATTN_HALF)[:, None])
    dist = (rel * dilation).astype(F32)
    b = jnp.where(rel <= ATTN_HALF, -slopes[:, None, None] * dist, _NEG)
    return b.reshape(ATTN_HEADS // 2, 2 * ATTN_HALF, _ATTN_WIN)


def _dilated_attention(q, k, v, dilation, *, seq):
    t_total = q.shape[0]
    bsz = t_total // seq
    sub_len = seq // dilation
    lq = min(_ATTN_LQ, sub_len)
    nt = sub_len // lq
    hpb = lq // ATTN_HALF
    n_hblk = t_total // dilation // ATTN_HALF
    view = lambda a: a.reshape(t_total // dilation, dilation * a.shape[1])
    main = lambda b, r, t: (b * nt + t, r)
    prev = lambda b, r, t: (jnp.maximum((b * nt + t) * hpb - 1, 0), r)
    nxt = lambda b, r, t: (jnp.minimum((b * nt + t + 1) * hpb, n_hblk - 1), r)
    blk = pl.BlockSpec((lq, ATTN_WIDTH), main)
    hblk_p = pl.BlockSpec((ATTN_HALF, ATTN_WIDTH), prev)
    hblk_n = pl.BlockSpec((ATTN_HALF, ATTN_WIDTH), nxt)
    window = pltpu.VMEM((lq + 2 * ATTN_HALF, ATTN_WIDTH), BF16)
    qv, kv, vv = view(q), view(k), view(v)
    o, lse = pl.pallas_call(
        functools.partial(_attn_kernel, lq=lq, sub_len=sub_len),
        grid=(bsz, dilation, nt),
        in_specs=[blk, blk, hblk_p, hblk_n, blk, hblk_p, hblk_n,
                  _const_spec((ATTN_HEADS // 2, 2 * ATTN_HALF, _ATTN_WIN))],
        out_specs=[blk, pl.BlockSpec((lq, LANES), main)],
        out_shape=[jax.ShapeDtypeStruct(qv.shape, BF16),
                   jax.ShapeDtypeStruct((t_total // dilation, dilation * LANES), F32)],
        scratch_shapes=[window, window],
        compiler_params=_cparams("parallel", "parallel", "parallel"),
        name=f"attn_d{dilation}",
    )(qv, kv, kv, kv, vv, vv, vv, _attn_bias(dilation))
    return o.reshape(t_total, ATTN_WIDTH), lse.reshape(t_total, LANES)


def _mix_kernel(o0_ref, o1_ref, o2_ref, l0_ref, l1_ref, l2_ref, yf_ref, yb_ref, xs_ref, z_ref, x_ref,
                mod_ref, e_ref, dskip_ref, gssd_ref, wout_ref, gpost_ref, gpre_ref, wr_ref, br_ref,
                res_ref, h2_ref, ri_ref, rw_ref, cnt_out_ref, cnt_ref, *, tm):
    i = pl.program_id(0)

    @pl.when(i == 0)
    def _():
        cnt_ref[...] = jnp.zeros_like(cnt_ref)

    l0, l1, l2 = l0_ref[...], l1_ref[...], l2_ref[...]
    m = jnp.maximum(jnp.maximum(l0, l1), l2)
    es = [jnp.exp(l - m) for l in (l0, l1, l2)]
    inv = 1.0 / (es[0] + es[1] + es[2])
    expand = e_ref[...]
    attn = jnp.zeros((tm, ATTN_WIDTH), F32)
    for e_g, o_ref in zip(es, (o0_ref, o1_ref, o2_ref)):
        wh, wl = _split2(e_g * inv)
        attn = attn + (_dot(wh, expand) + _dot(wl, expand)) * o_ref[...].astype(F32)

    xs = xs_ref[...].astype(F32)
    z = z_ref[...].astype(F32)
    y = yf_ref[...].astype(F32) + yb_ref[...].astype(F32) + dskip_ref[...] * xs
    y = _rms(y * (z * _sigmoid(z))) * gssd_ref[...]
    mix = _dot(attn.astype(BF16), wout_ref[0:ATTN_WIDTH, :]) + _dot(y.astype(BF16), wout_ref[ATTN_WIDTH:, :])
    gate_m = mod_ref[0, 2:3, :]
    shift_f = mod_ref[0, 3:4, :]
    scale_f = mod_ref[0, 4:5, :]
    res = x_ref[...] + gate_m * (_rms(mix) * gpost_ref[...])
    res_ref[...] = res
    h2 = _rms(res) * gpre_ref[...] * (1.0 + scale_f) + shift_f
    h2_ref[...] = h2

    vals = _dot_f32(h2, wr_ref[...]) + br_ref[...]
    lane = lax.broadcasted_iota(jnp.int32, (tm, LANES), 1)
    sels, tops = [], []
    for _ in range(TOP_K):
        mx = jnp.max(vals, axis=-1, keepdims=True)
        idx = jnp.min(jnp.where(vals == mx, lane, LANES), axis=-1, keepdims=True)
        sel = lane == idx
        sels.append(sel)
        tops.append((mx, idx))
        vals = jnp.where(sel, -jnp.inf, vals)
    ex = [jnp.exp(tv - tops[0][0]) for tv, _ in tops]
    den = ex[0] + ex[1] + ex[2] + ex[3]
    hit = jnp.zeros((tm, LANES), F32)
    for sel in sels:
        hit = jnp.where(sel, 1.0, hit)
    ri = lax.broadcasted_iota(jnp.int32, (tm, tm), 0)
    ci = lax.broadcasted_iota(jnp.int32, (tm, tm), 1)
    before = jnp.where(ci < ri, 1.0, 0.0).astype(BF16)
    cum = _dot(before, hit.astype(BF16)) + cnt_ref[0:1, :]
    cnt = cnt_ref[0:1, :] + jnp.sum(hit, axis=0, keepdims=True)
    cnt_ref[...] = jnp.broadcast_to(cnt, cnt_ref.shape)
    cnt_out_ref[...] = jnp.broadcast_to(cnt, cnt_out_ref.shape).astype(jnp.int32)
    route_i = jnp.zeros((tm, LANES), jnp.int32)
    route_w = jnp.zeros((tm, LANES), F32)
    for kk in range(TOP_K):
        pos = jnp.sum(jnp.where(sels[kk], cum, 0.0), axis=-1, keepdims=True).astype(jnp.int32)
        route_i = jnp.where(lane == kk, tops[kk][1], jnp.where(lane == TOP_K + kk, pos, route_i))
        route_w = jnp.where(lane == kk, ex[kk] / den, route_w)
    ri_ref[...] = route_i
    rw_ref[...] = route_w


def _mix_and_route(outs, lses, yf, yb, xc, z, x2, mod, d_skip, g_ssd, w_out, g_post, g_pre_ffn,
                   w_router, b_router, *, seq, tm=256):
    t_total = x2.shape[0]
    nt = t_total // tm
    tiles_per_seq = seq // tm
    expand = (jnp.arange(LANES)[:, None] == jnp.arange(ATTN_WIDTH)[None, :] // HEAD_DIM).astype(BF16)
    wr = jnp.zeros((D_MODEL, LANES), F32).at[:, :N_EXPERTS].set(w_router)
    br = jnp.full((1, LANES), _NEG, F32).at[0, :N_EXPERTS].set(b_router)
    row = lambda i: (i, 0)
    wide = pl.BlockSpec((tm, D_MODEL), row)
    narrow = pl.BlockSpec((tm, LANES), row)
    vec = _const_spec((1, D_MODEL))
    return pl.pallas_call(
        functools.partial(_mix_kernel, tm=tm),
        grid=(nt,),
        in_specs=[wide] * 3 + [narrow] * 3 + [wide] * 5 + [
            pl.BlockSpec((1, N_MOD, D_MODEL), lambda i: (i // tiles_per_seq, 0, 0)),
            _const_spec((LANES, ATTN_WIDTH)), vec, vec,
            _const_spec((ATTN_WIDTH + SSD_WIDTH, D_MODEL)), vec, vec,
            _const_spec((D_MODEL, LANES)), _const_spec((1, LANES))],
        out_specs=[wide, wide, narrow, narrow, _const_spec((SUBLANES, LANES))],
        out_shape=[jax.ShapeDtypeStruct((t_total, D_MODEL), F32),
                   jax.ShapeDtypeStruct((t_total, D_MODEL), F32),
                   jax.ShapeDtypeStruct((t_total, LANES), jnp.int32),
                   jax.ShapeDtypeStruct((t_total, LANES), F32),
                   jax.ShapeDtypeStruct((SUBLANES, LANES), jnp.int32)],
        scratch_shapes=[pltpu.VMEM((SUBLANES, LANES), F32)],
        compiler_params=_cparams("arbitrary"),
        name="mix_route",
    )(*outs, *lses, yf, yb, xc, z, x2, mod, expand, jnp.repeat(d_skip, SSD_HEAD_DIM).reshape(1, -1),
      g_ssd.reshape(1, -1), w_out.astype(BF16), g_post.reshape(1, -1), g_pre_ffn.reshape(1, -1), wr, br)


_MOE_TM = 256
_ROUTE_TM = 256


def _dispatch_kernel(dest_ref, h_ref, xs_in_ref, xs_ref, sem):
    del xs_in_ref
    n = _ROUTE_TM * TOP_K

    def copy(j):
        return pltpu.make_async_copy(h_ref.at[pl.ds(j // TOP_K, 1)], xs_ref.at[pl.ds(dest_ref[j], 1)], sem)

    def start(j, carry):
        copy(j).start()
        return carry

    def wait(j, carry):
        copy(j).wait()
        return carry

    lax.fori_loop(0, n, start, 0)
    lax.fori_loop(0, n, wait, 0)


def _dispatch(h2, dest):
    t_total = h2.shape[0]
    n_rows = t_total * TOP_K
    return pl.pallas_call(
        _dispatch_kernel,
        grid=(t_total // _ROUTE_TM,),
        in_specs=[pl.BlockSpec((_ROUTE_TM * TOP_K,), lambda i: (i,), memory_space=pltpu.SMEM),
                  pl.BlockSpec((_ROUTE_TM, D_MODEL), lambda i: (i, 0)),
                  pl.BlockSpec(memory_space=pl.ANY)],
        out_specs=pl.BlockSpec(memory_space=pl.ANY),
        out_shape=jax.ShapeDtypeStruct((n_rows, D_MODEL), F32),
        scratch_shapes=[pltpu.SemaphoreType.DMA(())],
        input_output_aliases={2: 0},
        compiler_params=_cparams("arbitrary"),
        name="moe_dispatch",
    )(dest, h2, jnp.zeros((n_rows, D_MODEL), F32))


def _gmm_kernel(tile_ref, exp_ref, valid_ref, offs_ref, x_ref, wgu_ref, bgu_ref, wdn_ref, bdn_ref, y_ref,
                wgu_bf, wdn_bf):
    j = pl.program_id(0)
    e = exp_ref[j]
    tile = tile_ref[j]
    prev_j = jnp.maximum(j - 1, 0)
    new_expert = (j == 0) | (exp_ref[prev_j] != e)
    new_tile = (j == 0) | (tile_ref[prev_j] != tile)

    @pl.when(new_expert)
    def _():
        wgu_bf[...] = wgu_ref[0].astype(BF16)
        wdn_bf[...] = wdn_ref[0].astype(BF16)

    @pl.when(valid_ref[j] != 0)
    def _():
        gu = _dot(x_ref[...].astype(BF16), wgu_bf[...]) + bgu_ref[0]
        gate = jnp.minimum(gu[:, :EXPERT_FF], SWIGLU_LIMIT)
        up = jnp.clip(gu[:, EXPERT_FF:], -SWIGLU_LIMIT, SWIGLU_LIMIT)
        act = (up + 1.0) * gate * _sigmoid(SWIGLU_ALPHA * gate)
        y = _dot(act.astype(BF16), wdn_bf[...]) + bdn_ref[0]
        rows = tile * _MOE_TM + lax.broadcasted_iota(jnp.int32, (_MOE_TM, 1), 0)
        mine = (rows >= offs_ref[e]) & (rows < offs_ref[e + 1])

        @pl.when(new_tile)
        def _():
            y_ref[...] = jnp.where(mine, y, 0.0)

        @pl.when(jnp.logical_not(new_tile))
        def _():
            y_ref[...] = jnp.where(mine, y, y_ref[...])


def _grouped_mlp(xs, counts, w_gate_up, b_gate_up, w_down, b_down):
    n_rows = xs.shape[0]
    n_tiles = n_rows // _MOE_TM
    n_visits = n_tiles + N_EXPERTS - 1
    offs = jnp.concatenate([jnp.zeros((1,), jnp.int32), jnp.cumsum(counts)]).astype(jnp.int32)
    first = offs[:-1] // _MOE_TM
    last = jnp.maximum(offs[1:] - 1, 0) // _MOE_TM
    per_expert = jnp.where(counts > 0, last - first + 1, 0)
    vstart = jnp.concatenate([jnp.zeros((1,), jnp.int32), jnp.cumsum(per_expert)]).astype(jnp.int32)
    total = vstart[-1]
    vis = jnp.minimum(jnp.arange(n_visits, dtype=jnp.int32), total - 1)
    exp_ids = (jnp.searchsorted(vstart, vis, side="right") - 1).astype(jnp.int32)
    tile_ids = (first[exp_ids] + vis - vstart[exp_ids]).astype(jnp.int32)
    valid = (jnp.arange(n_visits) < total).astype(jnp.int32)
    grid_spec = pltpu.PrefetchScalarGridSpec(
        num_scalar_prefetch=4,
        grid=(n_visits,),
        in_specs=[pl.BlockSpec((_MOE_TM, D_MODEL), lambda j, t, e, v, o: (t[j], 0)),
                  pl.BlockSpec((1, D_MODEL, 2 * EXPERT_FF), lambda j, t, e, v, o: (e[j], 0, 0)),
                  pl.BlockSpec((1, 1, 2 * EXPERT_FF), lambda j, t, e, v, o: (e[j], 0, 0)),
                  pl.BlockSpec((1, EXPERT_FF, D_MODEL), lambda j, t, e, v, o: (e[j], 0, 0)),
                  pl.BlockSpec((1, 1, D_MODEL), lambda j, t, e, v, o: (e[j], 0, 0))],
        out_specs=pl.BlockSpec((_MOE_TM, D_MODEL), lambda j, t, e, v, o: (t[j], 0)),
        scratch_shapes=[pltpu.VMEM((D_MODEL, 2 * EXPERT_FF), BF16), pltpu.VMEM((EXPERT_FF, D_MODEL), BF16)])
    return pl.pallas_call(
        _gmm_kernel,
        grid_spec=grid_spec,
        out_shape=jax.ShapeDtypeStruct((n_rows, D_MODEL), F32),
        compiler_params=_cparams("arbitrary"),
        name="moe_mlp",
    )(tile_ids, exp_ids, valid, offs, xs, w_gate_up, b_gate_up.reshape(N_EXPERTS, 1, -1),
      w_down, b_down.reshape(N_EXPERTS, 1, -1))


def _combine_kernel(dest_ref, ys_ref, rw_ref, res_ref, mod_ref, g_ref, o_ref, buf_ref, sem):
    n = _ROUTE_TM * TOP_K

    def copy(j):
        return pltpu.make_async_copy(ys_ref.at[pl.ds(dest_ref[j], 1)],
                                     buf_ref.at[j % TOP_K, pl.ds(j // TOP_K, 1)], sem)

    def start(j, carry):
        copy(j).start()
        return carry

    def wait(j, carry):
        copy(j).wait()
        return carry

    lax.fori_loop(0, n, start, 0)
    lax.fori_loop(0, n, wait, 0)
    rw = rw_ref[...]
    ffn = jnp.zeros((_ROUTE_TM, D_MODEL), F32)
    for kk in range(TOP_K):
        ffn = ffn + rw[:, kk:kk + 1] * buf_ref[kk]
    gate_f = mod_ref[0, 5:6, :]
    o_ref[...] = res_ref[...] + gate_f * (_rms(ffn) * g_ref[...])


def _combine(ys, dest, route_w, res, mod, g_post_ffn, *, seq):
    t_total = res.shape[0]
    tiles_per_seq = seq // _ROUTE_TM
    row = lambda i: (i, 0)
    return pl.pallas_call(
        _combine_kernel,
        grid=(t_total // _ROUTE_TM,),
        in_specs=[pl.BlockSpec((_ROUTE_TM * TOP_K,), lambda i: (i,), memory_space=pltpu.SMEM),
                  pl.BlockSpec(memory_space=pl.ANY),
                  pl.BlockSpec((_ROUTE_TM, LANES), row),
                  pl.BlockSpec((_ROUTE_TM, D_MODEL), row),
                  pl.BlockSpec((1, N_MOD, D_MODEL), lambda i: (i // tiles_per_seq, 0, 0)),
                  _const_spec((1, D_MODEL))],
        out_specs=pl.BlockSpec((_ROUTE_TM, D_MODEL), row),
        out_shape=jax.ShapeDtypeStruct((t_total, D_MODEL), F32),
        scratch_shapes=[pltpu.VMEM((TOP_K, _ROUTE_TM, D_MODEL), F32), pltpu.SemaphoreType.DMA(())],
        compiler_params=_cparams("arbitrary"),
        name="moe_combine",
    )(dest, ys, route_w, res, mod, g_post_ffn.reshape(1, -1))


def _forward(x, c, w_ada, b_ada, g_pre_mix, g_post_mix, w_in, conv_w, conv_b, dt_bias, a_log, d_skip,
             g_ssd_norm, w_out, g_pre_ffn, g_post_ffn, w_router, b_router, w_gate_up, b_gate_up, w_down, b_down):
    bsz, seq, _ = x.shape
    x2 = x.astype(F32).reshape(bsz * seq, D_MODEL)
    mod = _ada_mod(c.astype(F32), w_ada, b_ada)
    q, k, v, z, xc, dt = _in_proj(x2, mod, g_pre_mix, w_in, conv_w, conv_b, dt_bias, seq=seq)
    attn = [_dilated_attention(q, k, v, d, seq=seq) for d in DILATIONS]
    yf, yb = _ssd(xc, dt, a_log, seq=seq)
    res, h2, route_i, route_w, counts = _mix_and_route(
        [o for o, _ in attn], [l for _, l in attn], yf, yb, xc, z, x2, mod, d_skip, g_ssd_norm, w_out,
        g_post_mix, g_pre_ffn, w_router, b_router, seq=seq)
    counts = counts[0, :N_EXPERTS]
    offs = jnp.concatenate([jnp.zeros((1,), jnp.int32), jnp.cumsum(counts)[:-1]]).astype(jnp.int32)
    dest = (offs[route_i[:, :TOP_K]] + route_i[:, TOP_K:2 * TOP_K]).reshape(-1)
    xs = _dispatch(h2, dest)
    ys = _grouped_mlp(xs, counts, w_gate_up, b_gate_up, w_down, b_down)
    out = _combine(ys, dest, route_w, res, mod, g_post_ffn, seq=seq)
    return out.reshape(bsz, seq, D_MODEL)


def kernel(x, c, w_ada, b_ada, g_pre_mix, g_post_mix, w_in, conv_w, conv_b, dt_bias, a_log, d_skip, g_ssd_norm, w_out, g_pre_ffn, g_post_ffn, w_router, b_router, w_gate_up, b_gate_up, w_down, b_down):
    layer = lambda t: t[0].astype(F32)
    out = _forward(x, c, layer(w_ada), layer(b_ada), layer(g_pre_mix), layer(g_post_mix), layer(w_in),
                   layer(conv_w), layer(conv_b), layer(dt_bias), layer(a_log), layer(d_skip),
                   layer(g_ssd_norm), layer(w_out), layer(g_pre_ffn), layer(g_post_ffn), layer(w_router),
                   layer(b_router), layer(w_gate_up), layer(b_gate_up), layer(w_down), layer(b_down))
    return out.astype(x.dtype)
```

```python
import functools

import jax
import jax.numpy as jnp
from jax import lax
from jax.experimental import pallas as pl
from jax.experimental.pallas import tpu as pltpu

F32 = jnp.float32
BF16 = jnp.bfloat16

D_MODEL = 1024
ATTN_HEADS = 16
HEAD_DIM = 64
ATTN_WIDTH = ATTN_HEADS * HEAD_DIM
DILATIONS = (1, 4, 16)
ATTN_HALF = 64
SSD_HEADS = 16
SSD_HEAD_DIM = 64
SSD_WIDTH = SSD_HEADS * SSD_HEAD_DIM
SSD_GROUPS = 2
SSD_STATE = 128
SSD_CHUNK = 128
CONV_WIDTH = 5
XBC_WIDTH = SSD_WIDTH + 2 * SSD_GROUPS * SSD_STATE
N_EXPERTS = 32
TOP_K = 4
EXPERT_FF = 1024
SWIGLU_ALPHA = 1.702
SWIGLU_LIMIT = 7.0
RMS_EPS = 1e-6
N_MOD = 6
LANES = 128
SUBLANES = 8
VMEM_LIMIT = 56 * 1024 * 1024


def _cparams(*sem):
    return pltpu.CompilerParams(dimension_semantics=sem, vmem_limit_bytes=VMEM_LIMIT)


def _const_spec(shape):
    nd = len(shape)
    return pl.BlockSpec(shape, lambda *_: (0,) * nd)


def _split2(a):
    hi = a.astype(BF16)
    lo = (a - hi.astype(F32)).astype(BF16)
    return hi, lo


def _dot(a, b):
    return jnp.dot(a, b, preferred_element_type=F32)


def _dot_f32(a, b):
    ah, al = _split2(a)
    bh, bl = _split2(b)
    return _dot(ah, bh) + (_dot(ah, bl) + _dot(al, bh))


def _sigmoid(x):
    return 1.0 / (1.0 + jnp.exp(-x))


def _rms(x):
    return x * lax.rsqrt(jnp.mean(x * x, axis=-1, keepdims=True) + RMS_EPS)


def _ada_kernel(c_ref, w_ref, b_ref, o_ref):
    c = c_ref[...]
    o_ref[...] = _dot_f32(c * _sigmoid(c), w_ref[...]) + b_ref[...]


def _ada_mod(c, w_ada, b_ada):
    bsz = c.shape[0]
    c8 = jnp.zeros((SUBLANES, D_MODEL), F32).at[:bsz].set(c)
    out = pl.pallas_call(
        _ada_kernel,
        grid=(N_MOD,),
        in_specs=[_const_spec((SUBLANES, D_MODEL)),
                  pl.BlockSpec((D_MODEL, D_MODEL), lambda j: (0, j)),
                  pl.BlockSpec((1, D_MODEL), lambda j: (0, j))],
        out_specs=pl.BlockSpec((SUBLANES, D_MODEL), lambda j: (0, j)),
        out_shape=jax.ShapeDtypeStruct((SUBLANES, N_MOD * D_MODEL), F32),
        compiler_params=_cparams("parallel"),
        name="ada_mod",
    )(c8, w_ada, b_ada.reshape(1, -1))
    return out[:bsz].reshape(bsz, N_MOD, D_MODEL)


def _inproj_kernel(x_ref, xp_ref, xn_ref, mod_ref, g_ref, wqkv_ref, wz_ref, wxbc_ref, wdt_ref,
                   cw_ref, cb_ref, dtb_ref,
                   q1_ref, k1_ref, v1_ref, q4_ref, k4_ref, v4_ref, q16_ref, k16_ref, v16_ref,
                   z_ref, xc_ref, dt_ref, buf_ref, slab_ref, *, tm, tiles_per_seq):
    i = pl.program_id(0)
    shift = mod_ref[0, 0:1, :]
    scale = mod_ref[0, 1:2, :]
    g = g_ref[...]

    def norm_mod(x):
        return _rms(x) * g * (1.0 + scale) + shift

    h = norm_mod(x_ref[...])
    hb = h.astype(BF16)
    n_slab = ATTN_WIDTH // LANES
    outs = ((q1_ref, q4_ref, q16_ref), (k1_ref, k4_ref, k16_ref), (v1_ref, v4_ref, v16_ref))
    for c, (nat_ref, d4_ref, d16_ref) in enumerate(outs):
        r = _dot(hb, wqkv_ref[:, c * ATTN_WIDTH:(c + 1) * ATTN_WIDTH])
        if c == 0:
            r = r * HEAD_DIM ** -0.5
        nat_ref[...] = r.astype(BF16)
        for s in range(n_slab):
            slab_ref[s] = r[:, s * LANES:(s + 1) * LANES]
        for dil, d_ref in ((DILATIONS[1], d4_ref), (DILATIONS[2], d16_ref)):
            for rr in range(dil):
                for s in range(n_slab):
                    d_ref[0, rr, :, s * LANES:(s + 1) * LANES] = (
                        slab_ref[s, pl.ds(rr, tm // dil, stride=dil), :].astype(BF16))
    z_ref[...] = _dot(hb, wz_ref[...]).astype(BF16)
    dt_ref[...] = jax.nn.softplus(_dot_f32(h, wdt_ref[...]) + dtb_ref[...])

    hh = norm_mod(jnp.concatenate([xp_ref[...], xn_ref[...]], axis=0)).astype(BF16)
    halo = _dot(hh, wxbc_ref[...])
    t_in_seq = i % tiles_per_seq
    prev_ok = (t_in_seq != 0).astype(F32)
    next_ok = (t_in_seq != tiles_per_seq - 1).astype(F32)
    buf_ref[0:SUBLANES, :] = halo[0:SUBLANES] * prev_ok
    buf_ref[SUBLANES + tm:, :] = halo[SUBLANES:] * next_ok
    buf_ref[SUBLANES:SUBLANES + tm, :] = _dot(hb, wxbc_ref[...])
    acc = cb_ref[...]
    for j in range(CONV_WIDTH):
        off = SUBLANES - CONV_WIDTH // 2 + j
        acc = acc + buf_ref[off:off + tm, :] * cw_ref[j:j + 1, :]
    xc_ref[...] = (acc * _sigmoid(acc)).astype(BF16)


def _in_proj(x2, mod, g_pre, w_in, conv_w, conv_b, dt_bias, *, seq, tm=512):
    t_total = x2.shape[0]
    nt = t_total // tm
    tiles_per_seq = seq // tm
    hb = tm // SUBLANES
    n_hblk = t_total // SUBLANES
    qkv_w = 3 * ATTN_WIDTH
    w_qkv = w_in[:, :qkv_w].astype(BF16)
    w_z = w_in[:, qkv_w:qkv_w + SSD_WIDTH].astype(BF16)
    w_xbc = w_in[:, qkv_w + SSD_WIDTH:qkv_w + SSD_WIDTH + XBC_WIDTH].astype(BF16)
    w_dt_raw = w_in[:, qkv_w + SSD_WIDTH + XBC_WIDTH:]
    w_dt = jnp.zeros((D_MODEL, 2 * LANES), F32)
    dtb = jnp.zeros((1, 2 * LANES), F32)
    for dr in range(2):
        w_dt = w_dt.at[:, dr * LANES:dr * LANES + SSD_HEADS].set(w_dt_raw[:, dr * SSD_HEADS:(dr + 1) * SSD_HEADS])
        dtb = dtb.at[0, dr * LANES:dr * LANES + SSD_HEADS].set(dt_bias[dr])
    row = lambda i: (i, 0)
    bsz = t_total // seq
    dil_shapes, dil_specs = [], []
    for dil in DILATIONS[1:]:
        dil_shapes += [jax.ShapeDtypeStruct((bsz, dil, seq // dil, ATTN_WIDTH), BF16)] * 3
        dil_specs += [pl.BlockSpec((1, dil, tm // dil, ATTN_WIDTH),
                                   lambda i: (i // tiles_per_seq, 0, i % tiles_per_seq, 0))] * 3
    out_shape = [jax.ShapeDtypeStruct((t_total, ATTN_WIDTH), BF16)] * 3 + dil_shapes + [
        jax.ShapeDtypeStruct((t_total, SSD_WIDTH), BF16),
        jax.ShapeDtypeStruct((t_total, XBC_WIDTH), BF16),
        jax.ShapeDtypeStruct((t_total, 2 * LANES), F32)]
    return pl.pallas_call(
        functools.partial(_inproj_kernel, tm=tm, tiles_per_seq=tiles_per_seq),
        grid=(nt,),
        in_specs=[pl.BlockSpec((tm, D_MODEL), row),
                  pl.BlockSpec((SUBLANES, D_MODEL), lambda i: (jnp.maximum(i * hb - 1, 0), 0)),
                  pl.BlockSpec((SUBLANES, D_MODEL), lambda i: (jnp.minimum((i + 1) * hb, n_hblk - 1), 0)),
                  pl.BlockSpec((1, N_MOD, D_MODEL), lambda i: (i // tiles_per_seq, 0, 0)),
                  _const_spec((1, D_MODEL)),
                  _const_spec((D_MODEL, qkv_w)),
                  _const_spec((D_MODEL, SSD_WIDTH)),
                  _const_spec((D_MODEL, XBC_WIDTH)),
                  _const_spec((D_MODEL, 2 * LANES)),
                  _const_spec((CONV_WIDTH, XBC_WIDTH)),
                  _const_spec((1, XBC_WIDTH)),
                  _const_spec((1, 2 * LANES))],
        out_specs=[pl.BlockSpec((tm, ATTN_WIDTH), row)] * 3 + dil_specs + [
            pl.BlockSpec((tm, SSD_WIDTH), row),
            pl.BlockSpec((tm, XBC_WIDTH), row),
            pl.BlockSpec((tm, 2 * LANES), row)],
        out_shape=out_shape,
        scratch_shapes=[pltpu.VMEM((tm + 2 * SUBLANES, XBC_WIDTH), F32),
                        pltpu.VMEM((ATTN_WIDTH // LANES, tm, LANES), F32)],
        compiler_params=_cparams("parallel"),
        name="in_proj",
    )(x2, x2, x2, mod, g_pre.reshape(1, -1), w_qkv, w_z, w_xbc, w_dt, conv_w, conv_b.reshape(1, -1), dtb)


_NEG = -1e30


def _ssd_direction(xc_ref, dt_ref, a_ref, e_ref, h_ref, y_ref, reverse):
    qn = SSD_CHUNK
    gw = SSD_WIDTH // SSD_GROUPS
    li = lax.broadcasted_iota(jnp.int32, (qn, qn), 0)
    si = lax.broadcasted_iota(jnp.int32, (qn, qn), 1)
    mask = (si >= li) if reverse else (si <= li)
    tri = jnp.where(mask, 1.0, 0.0).astype(BF16)
    dt = dt_ref[...]
    adt = dt * a_ref[...]
    p0 = adt.astype(BF16)
    r0 = adt - p0.astype(F32)
    p1 = r0.astype(BF16)
    p2 = (r0 - p1.astype(F32)).astype(BF16)
    acum = _dot(tri, p0) + (_dot(tri, p1) + _dot(tri, p2))
    last = 0 if reverse else qn - 1
    eo = jnp.exp(acum)
    ds = jnp.exp(acum[last:last + 1, :] - acum)
    sh, sl = _split2(jnp.concatenate([dt, ds, eo], axis=0))
    ex = _dot(sh, e_ref[...]) + _dot(sl, e_ref[...])
    dt_x, ds_x, eo_x = ex[0:qn], ex[qn:2 * qn], ex[2 * qn:3 * qn]
    xd = xc_ref[:, 0:SSD_WIDTH].astype(F32) * dt_x
    xdb = xd.astype(BF16)
    xwb = (xd * ds_x).astype(BF16)
    acum_t = acum.T
    lane = lax.broadcasted_iota(jnp.int32, (qn, LANES), 1)
    for g in range(SSD_GROUPS):
        b0 = SSD_WIDTH + g * SSD_STATE
        c0 = SSD_WIDTH + SSD_GROUPS * SSD_STATE + g * SSD_STATE
        bm = xc_ref[:, b0:b0 + SSD_STATE]
        cm = xc_ref[:, c0:c0 + SSD_STATE]
        cb = lax.dot_general(cm, bm, (((1,), (1,)), ((), ())), preferred_element_type=F32)
        hg = h_ref[g]
        yoff = _dot(cm, hg.astype(BF16))
        st = lax.dot_general(bm, xwb[:, g * gw:(g + 1) * gw], (((0,), (0,)), ((), ())),
                             preferred_element_type=F32)
        h_ref[g] = hg * eo_x[last:last + 1, g * gw:(g + 1) * gw] + st
        for pr in range(gw // LANES):
            col = g * gw + pr * LANES
            xp = xdb[:, col:col + LANES]
            res = []
            for e in (col // SSD_HEAD_DIM, col // SSD_HEAD_DIM + 1):
                seg = acum[:, e:e + 1] - acum_t[e:e + 1, :]
                m = (cb * jnp.exp(jnp.where(mask, seg, _NEG))).astype(BF16)
                res.append(_dot(m, xp))
            yd = jnp.where(lane < SSD_HEAD_DIM, res[0], res[1])
            y = yd + yoff[:, pr * LANES:(pr + 1) * LANES] * eo_x[:, col:col + LANES]
            y_ref[:, col:col + LANES] = y.astype(BF16)


def _ssd_kernel(xcf_ref, xcb_ref, dtf_ref, dtb_ref, a_ref, e_ref, yf_ref, yb_ref, hf_ref, hb_ref):
    @pl.when(pl.program_id(1) == 0)
    def _():
        hf_ref[...] = jnp.zeros_like(hf_ref)
        hb_ref[...] = jnp.zeros_like(hb_ref)

    _ssd_direction(xcf_ref, dtf_ref, a_ref.at[0], e_ref, hf_ref, yf_ref, False)
    _ssd_direction(xcb_ref, dtb_ref, a_ref.at[1], e_ref, hb_ref, yb_ref, True)


def _ssd(xc, dt, a_log, *, seq):
    t_total = xc.shape[0]
    bsz = t_total // seq
    nc = seq // SSD_CHUNK
    a_rows = jnp.zeros((2, 1, LANES), F32).at[:, 0, :SSD_HEADS].set(-jnp.exp(a_log))
    expand = (jnp.arange(LANES)[:, None] == jnp.arange(SSD_WIDTH)[None, :] // SSD_HEAD_DIM).astype(BF16)
    fwd = lambda b, c: (b * nc + c, 0)
    bwd = lambda b, c: (b * nc + nc - 1 - c, 0)
    state = pltpu.VMEM((SSD_GROUPS, SSD_STATE, SSD_WIDTH // SSD_GROUPS), F32)
    return pl.pallas_call(
        _ssd_kernel,
        grid=(bsz, nc),
        in_specs=[pl.BlockSpec((SSD_CHUNK, XBC_WIDTH), fwd),
                  pl.BlockSpec((SSD_CHUNK, XBC_WIDTH), bwd),
                  pl.BlockSpec((SSD_CHUNK, LANES), fwd),
                  pl.BlockSpec((SSD_CHUNK, LANES), lambda b, c: (b * nc + nc - 1 - c, 1)),
                  _const_spec((2, 1, LANES)),
                  _const_spec((LANES, SSD_WIDTH))],
        out_specs=[pl.BlockSpec((SSD_CHUNK, SSD_WIDTH), fwd),
                   pl.BlockSpec((SSD_CHUNK, SSD_WIDTH), bwd)],
        out_shape=[jax.ShapeDtypeStruct((t_total, SSD_WIDTH), BF16)] * 2,
        scratch_shapes=[state, state],
        compiler_params=_cparams("parallel", "arbitrary"),
        name="ssd",
    )(xc, xc, dt, dt, a_rows, expand)


_ATTN_WIN = 3 * ATTN_HALF
_ATTN_LQ = 512


def _attn_kernel(q_ref, k_ref, kp_ref, kn_ref, v_ref, vp_ref, vn_ref, bias_ref, o_ref, lse_ref,
                 kw_ref, vw_ref, *, lq, sub_len):
    t = pl.program_id(2)
    hb = ATTN_HALF
    for src, halo_p, halo_n, win in ((k_ref, kp_ref, kn_ref, kw_ref), (v_ref, vp_ref, vn_ref, vw_ref)):
        win[0:hb, :] = halo_p[...]
        win[hb:hb + lq, :] = src[...]
        win[hb + lq:, :] = halo_n[...]
    lane = lax.broadcasted_iota(jnp.int32, (hb, LANES), 1)
    first_head = lane < HEAD_DIM
    kpos = lax.broadcasted_iota(jnp.int32, (1, _ATTN_WIN), 1)

    def body(qb, carry):
        r0 = pl.multiple_of(qb * hb, hb)
        kidx = t * lq + r0 - hb + kpos
        in_seq = (kidx >= 0) & (kidx < sub_len)
        lse_tile = jnp.zeros((hb, LANES), F32)
        for hp in range(ATTN_HEADS // 2):
            cs = slice(hp * LANES, (hp + 1) * LANES)
            q = q_ref[pl.ds(r0, hb), cs]
            zero = jnp.zeros_like(q)
            q2 = jnp.concatenate([jnp.where(first_head, q, zero), jnp.where(first_head, zero, q)], axis=0)
            s = lax.dot_general(q2, kw_ref[pl.ds(r0, _ATTN_WIN), cs], (((1,), (1,)), ((), ())),
                                preferred_element_type=F32) + bias_ref[hp]
            s = jnp.where(in_seq, s, _NEG)
            m = jnp.max(s, axis=-1, keepdims=True)
            p = jnp.exp(s - m)
            den = jnp.sum(p, axis=-1, keepdims=True)
            pv = _dot(p.astype(BF16), vw_ref[pl.ds(r0, _ATTN_WIN), cs]) * (1.0 / den)
            o_ref[pl.ds(r0, hb), cs] = jnp.where(first_head, pv[0:hb], pv[hb:]).astype(BF16)
            lse = m + jnp.log(den)
            lse_tile = jnp.where(lane == 2 * hp, lse[0:hb], jnp.where(lane == 2 * hp + 1, lse[hb:], lse_tile))
        lse_ref[pl.ds(r0, hb), :] = lse_tile
        return carry

    lax.fori_loop(0, lq // hb, body, 0)


def _attn_bias(dilation):
    slopes = jnp.exp2(-8.0 * jnp.arange(1, ATTN_HEADS + 1, dtype=F32) / ATTN_HEADS)
    rel = jnp.abs(jnp.arange(_ATTN_WIN)[None, :] - ATTN_HALF - jnp.arange(ATTN_HALF)[:, None])
    dist = (rel * dilation).astype(F32)
    b = jnp.where(rel <= ATTN_HALF, -slopes[:, None, None] * dist, _NEG)
    return b.reshape(ATTN_HEADS // 2, 2 * ATTN_HALF, _ATTN_WIN)


def _dilated_attention(q, k, v, dilation, *, seq):
    t_total = q.shape[0]
    bsz = t_total // seq
    sub_len = seq // dilation
    lq = min(_ATTN_LQ, sub_len)
    nt = sub_len // lq
    hpb = lq // ATTN_HALF
    n_hblk = t_total // ATTN_HALF
    tile = lambda b, r, t: (b * dilation + r) * nt + t
    main = lambda b, r, t: (tile(b, r, t), 0)
    prev = lambda b, r, t: (jnp.maximum(tile(b, r, t) * hpb - 1, 0), 0)
    nxt = lambda b, r, t: (jnp.minimum((tile(b, r, t) + 1) * hpb, n_hblk - 1), 0)
    blk = pl.BlockSpec((lq, ATTN_WIDTH), main)
    hblk_p = pl.BlockSpec((ATTN_HALF, ATTN_WIDTH), prev)
    hblk_n = pl.BlockSpec((ATTN_HALF, ATTN_WIDTH), nxt)
    window = pltpu.VMEM((lq + 2 * ATTN_HALF, ATTN_WIDTH), BF16)
    return pl.pallas_call(
        functools.partial(_attn_kernel, lq=lq, sub_len=sub_len),
        grid=(bsz, dilation, nt),
        in_specs=[blk, blk, hblk_p, hblk_n, blk, hblk_p, hblk_n,
                  _const_spec((ATTN_HEADS // 2, 2 * ATTN_HALF, _ATTN_WIN))],
        out_specs=[blk, pl.BlockSpec((lq, LANES), main)],
        out_shape=[jax.ShapeDtypeStruct((t_total, ATTN_WIDTH), BF16),
                   jax.ShapeDtypeStruct((t_total, LANES), F32)],
        scratch_shapes=[window, window],
        compiler_params=_cparams("parallel", "parallel", "parallel"),
        name=f"attn_d{dilation}",
    )(q, k, k, k, v, v, v, _attn_bias(dilation))


def _mix_kernel(o0_ref, o1_ref, o2_ref, l0_ref, l1_ref, l2_ref, yf_ref, yb_ref, xs_ref, z_ref, x_ref,
                mod_ref, e_ref, dskip_ref, gssd_ref, wout_ref, gpost_ref, gpre_ref, wr_ref, br_ref,
                res_ref, h2_ref, ri_ref, rw_ref, cnt_out_ref, cnt_ref, lnat_ref, onat_ref, *, tm):
    i = pl.program_id(0)
    n_slab = ATTN_WIDTH // LANES

    @pl.when(i == 0)
    def _():
        cnt_ref[...] = jnp.zeros_like(cnt_ref)

    dilated = ((DILATIONS[1], l1_ref, o1_ref), (DILATIONS[2], l2_ref, o2_ref))
    for gi, (dil, l_ref, _) in enumerate(dilated):
        for rr in range(dil):
            lnat_ref[gi, pl.ds(rr, tm // dil, stride=dil), :] = l_ref[0, rr]

    l0, l1, l2 = l0_ref[...], lnat_ref[0], lnat_ref[1]
    m = jnp.maximum(jnp.maximum(l0, l1), l2)
    es = [jnp.exp(l - m) for l in (l0, l1, l2)]
    inv = 1.0 / (es[0] + es[1] + es[2])
    expand = e_ref[...]

    def widen(w):
        wh, wl = _split2(w)
        return _dot(wh, expand) + _dot(wl, expand)

    attn = widen(es[0] * inv) * o0_ref[...].astype(F32)
    for gi, (dil, _, o_ref) in enumerate(dilated):
        for rr in range(dil):
            for s in range(n_slab):
                onat_ref[s, pl.ds(rr, tm // dil, stride=dil), :] = (
                    o_ref[0, rr, :, s * LANES:(s + 1) * LANES].astype(F32))
        o_nat = jnp.concatenate([onat_ref[s] for s in range(n_slab)], axis=1)
        attn = attn + widen(es[gi + 1] * inv) * o_nat

    xs = xs_ref[...].astype(F32)
    z = z_ref[...].astype(F32)
    y = yf_ref[...].astype(F32) + yb_ref[...].astype(F32) + dskip_ref[...] * xs
    y = _rms(y * (z * _sigmoid(z))) * gssd_ref[...]
    mix = _dot(attn.astype(BF16), wout_ref[0:ATTN_WIDTH, :]) + _dot(y.astype(BF16), wout_ref[ATTN_WIDTH:, :])
    gate_m = mod_ref[0, 2:3, :]
    shift_f = mod_ref[0, 3:4, :]
    scale_f = mod_ref[0, 4:5, :]
    res = x_ref[...] + gate_m * (_rms(mix) * gpost_ref[...])
    res_ref[...] = res
    h2 = _rms(res) * gpre_ref[...] * (1.0 + scale_f) + shift_f
    h2_ref[...] = h2

    vals = _dot_f32(h2, wr_ref[...]) + br_ref[...]
    lane = lax.broadcasted_iota(jnp.int32, (tm, LANES), 1)
    sels, tops = [], []
    for _ in range(TOP_K):
        mx = jnp.max(vals, axis=-1, keepdims=True)
        idx = jnp.min(jnp.where(vals == mx, lane, LANES), axis=-1, keepdims=True)
        sel = lane == idx
        sels.append(sel)
        tops.append((mx, idx))
        vals = jnp.where(sel, -jnp.inf, vals)
    ex = [jnp.exp(tv - tops[0][0]) for tv, _ in tops]
    den = ex[0] + ex[1] + ex[2] + ex[3]
    hit = jnp.zeros((tm, LANES), F32)
    for sel in sels:
        hit = jnp.where(sel, 1.0, hit)
    ri = lax.broadcasted_iota(jnp.int32, (tm, tm), 0)
    ci = lax.broadcasted_iota(jnp.int32, (tm, tm), 1)
    before = jnp.where(ci < ri, 1.0, 0.0).astype(BF16)
    cum = _dot(before, hit.astype(BF16)) + cnt_ref[0:1, :]
    cnt = cnt_ref[0:1, :] + jnp.sum(hit, axis=0, keepdims=True)
    cnt_ref[...] = jnp.broadcast_to(cnt, cnt_ref.shape)
    cnt_out_ref[...] = jnp.broadcast_to(cnt, cnt_out_ref.shape).astype(jnp.int32)
    route_i = jnp.zeros((tm, LANES), jnp.int32)
    route_w = jnp.zeros((tm, LANES), F32)
    for kk in range(TOP_K):
        pos = jnp.sum(jnp.where(sels[kk], cum, 0.0), axis=-1, keepdims=True).astype(jnp.int32)
        route_i = jnp.where(lane == kk, tops[kk][1], jnp.where(lane == TOP_K + kk, pos, route_i))
        route_w = jnp.where(lane == kk, ex[kk] / den, route_w)
    ri_ref[...] = route_i
    rw_ref[...] = route_w


def _mix_and_route(outs, lses, yf, yb, xc, z, x2, mod, d_skip, g_ssd, w_out, g_post, g_pre_ffn,
                   w_router, b_router, *, seq, tm=256):
    t_total = x2.shape[0]
    nt = t_total // tm
    tiles_per_seq = seq // tm
    expand = (jnp.arange(LANES)[:, None] == jnp.arange(ATTN_WIDTH)[None, :] // HEAD_DIM).astype(BF16)
    wr = jnp.zeros((D_MODEL, LANES), F32).at[:, :N_EXPERTS].set(w_router)
    br = jnp.full((1, LANES), _NEG, F32).at[0, :N_EXPERTS].set(b_router)
    row = lambda i: (i, 0)
    wide = pl.BlockSpec((tm, D_MODEL), row)
    narrow = pl.BlockSpec((tm, LANES), row)
    vec = _const_spec((1, D_MODEL))
    bsz = t_total // seq

    def dilated(a, dil):
        a = a.reshape(bsz, dil, seq // dil, a.shape[-1])
        spec = pl.BlockSpec((1, dil, tm // dil, a.shape[-1]),
                            lambda i: (i // tiles_per_seq, 0, i % tiles_per_seq, 0))
        return a, spec

    o_args, o_specs = [outs[0]], [wide]
    l_args, l_specs = [lses[0]], [narrow]
    for dil, o, l in zip(DILATIONS[1:], outs[1:], lses[1:]):
        a, spec = dilated(o, dil)
        o_args.append(a)
        o_specs.append(spec)
        a, spec = dilated(l, dil)
        l_args.append(a)
        l_specs.append(spec)
    outs, lses = o_args, l_args
    return pl.pallas_call(
        functools.partial(_mix_kernel, tm=tm),
        grid=(nt,),
        in_specs=o_specs + l_specs + [wide] * 5 + [
            pl.BlockSpec((1, N_MOD, D_MODEL), lambda i: (i // tiles_per_seq, 0, 0)),
            _const_spec((LANES, ATTN_WIDTH)), vec, vec,
            _const_spec((ATTN_WIDTH + SSD_WIDTH, D_MODEL)), vec, vec,
            _const_spec((D_MODEL, LANES)), _const_spec((1, LANES))],
        out_specs=[wide, wide, narrow, narrow, _const_spec((SUBLANES, LANES))],
        out_shape=[jax.ShapeDtypeStruct((t_total, D_MODEL), F32),
                   jax.ShapeDtypeStruct((t_total, D_MODEL), F32),
                   jax.ShapeDtypeStruct((t_total, LANES), jnp.int32),
                   jax.ShapeDtypeStruct((t_total, LANES), F32),
                   jax.ShapeDtypeStruct((SUBLANES, LANES), jnp.int32)],
        scratch_shapes=[pltpu.VMEM((SUBLANES, LANES), F32),
                        pltpu.VMEM((len(DILATIONS) - 1, tm, LANES), F32),
                        pltpu.VMEM((ATTN_WIDTH // LANES, tm, LANES), F32)],
        compiler_params=_cparams("arbitrary"),
        name="mix_route",
    )(*outs, *lses, yf, yb, xc, z, x2, mod, expand, jnp.repeat(d_skip, SSD_HEAD_DIM).reshape(1, -1),
      g_ssd.reshape(1, -1), w_out.astype(BF16), g_post.reshape(1, -1), g_pre_ffn.reshape(1, -1), wr, br)


_MOE_TM = 256
_ROUTE_TM = 256


def _dispatch_kernel(dest_ref, h_ref, xs_ref, sem):
    n = _ROUTE_TM * TOP_K

    def copy(j):
        return pltpu.make_async_copy(h_ref.at[pl.ds(j // TOP_K, 1)], xs_ref.at[pl.ds(dest_ref[j], 1)], sem)

    def start(j, carry):
        copy(j).start()
        return carry

    def wait(j, carry):
        copy(j).wait()
        return carry

    lax.fori_loop(0, n, start, 0)
    lax.fori_loop(0, n, wait, 0)


def _dispatch(h2, dest):
    t_total = h2.shape[0]
    n_rows = t_total * TOP_K
    return pl.pallas_call(
        _dispatch_kernel,
        grid=(t_total // _ROUTE_TM,),
        in_specs=[pl.BlockSpec((_ROUTE_TM * TOP_K,), lambda i: (i,), memory_space=pltpu.SMEM),
                  pl.BlockSpec((_ROUTE_TM, D_MODEL), lambda i: (i, 0))],
        out_specs=pl.BlockSpec(memory_space=pl.ANY),
        out_shape=jax.ShapeDtypeStruct((n_rows, D_MODEL), F32),
        scratch_shapes=[pltpu.SemaphoreType.DMA(())],
        compiler_params=_cparams("arbitrary"),
        name="moe_dispatch",
    )(dest, h2)


def _gmm_kernel(tile_ref, exp_ref, valid_ref, offs_ref, x_ref, wgu_ref, bgu_ref, wdn_ref, bdn_ref, y_ref,
                wgu_bf, wdn_bf):
    j = pl.program_id(0)
    e = exp_ref[j]
    tile = tile_ref[j]
    prev_j = jnp.maximum(j - 1, 0)
    new_expert = (j == 0) | (exp_ref[prev_j] != e)
    new_tile = (j == 0) | (tile_ref[prev_j] != tile)

    @pl.when(new_expert)
    def _():
        wgu_bf[...] = wgu_ref[0].astype(BF16)
        wdn_bf[...] = wdn_ref[0].astype(BF16)

    @pl.when(valid_ref[j] != 0)
    def _():
        gu = _dot(x_ref[...].astype(BF16), wgu_bf[...]) + bgu_ref[0]
        gate = jnp.minimum(gu[:, :EXPERT_FF], SWIGLU_LIMIT)
        up = jnp.clip(gu[:, EXPERT_FF:], -SWIGLU_LIMIT, SWIGLU_LIMIT)
        act = (up + 1.0) * gate * _sigmoid(SWIGLU_ALPHA * gate)
        y = _dot(act.astype(BF16), wdn_bf[...]) + bdn_ref[0]
        rows = tile * _MOE_TM + lax.broadcasted_iota(jnp.int32, (_MOE_TM, 1), 0)
        mine = (rows >= offs_ref[e]) & (rows < offs_ref[e + 1])

        @pl.when(new_tile)
        def _():
            y_ref[...] = jnp.where(mine, y, 0.0)

        @pl.when(jnp.logical_not(new_tile))
        def _():
            y_ref[...] = jnp.where(mine, y, y_ref[...])


def _grouped_mlp(xs, counts, w_gate_up, b_gate_up, w_down, b_down):
    n_rows = xs.shape[0]
    n_tiles = n_rows // _MOE_TM
    n_visits = n_tiles + N_EXPERTS - 1
    offs = jnp.concatenate([jnp.zeros((1,), jnp.int32), jnp.cumsum(counts)]).astype(jnp.int32)
    first = offs[:-1] // _MOE_TM
    last = jnp.maximum(offs[1:] - 1, 0) // _MOE_TM
    per_expert = jnp.where(counts > 0, last - first + 1, 0)
    vstart = jnp.concatenate([jnp.zeros((1,), jnp.int32), jnp.cumsum(per_expert)]).astype(jnp.int32)
    total = vstart[-1]
    vis = jnp.minimum(jnp.arange(n_visits, dtype=jnp.int32), total - 1)
    exp_ids = (jnp.sum(vstart[None, :] <= vis[:, None], axis=1) - 1).astype(jnp.int32)
    tile_ids = (first[exp_ids] + vis - vstart[exp_ids]).astype(jnp.int32)
    valid = (jnp.arange(n_visits) < total).astype(jnp.int32)
    grid_spec = pltpu.PrefetchScalarGridSpec(
        num_scalar_prefetch=4,
        grid=(n_visits,),
        in_specs=[pl.BlockSpec((_MOE_TM, D_MODEL), lambda j, t, e, v, o: (t[j], 0)),
                  pl.BlockSpec((1, D_MODEL, 2 * EXPERT_FF), lambda j, t, e, v, o: (e[j], 0, 0)),
                  pl.BlockSpec((1, 1, 2 * EXPERT_FF), lambda j, t, e, v, o: (e[j], 0, 0)),
                  pl.BlockSpec((1, EXPERT_FF, D_MODEL), lambda j, t, e, v, o: (e[j], 0, 0)),
                  pl.BlockSpec((1, 1, D_MODEL), lambda j, t, e, v, o: (e[j], 0, 0))],
        out_specs=pl.BlockSpec((_MOE_TM, D_MODEL), lambda j, t, e, v, o: (t[j], 0)),
        scratch_shapes=[pltpu.VMEM((D_MODEL, 2 * EXPERT_FF), BF16), pltpu.VMEM((EXPERT_FF, D_MODEL), BF16)])
    return pl.pallas_call(
        _gmm_kernel,
        grid_spec=grid_spec,
        out_shape=jax.ShapeDtypeStruct((n_rows, D_MODEL), F32),
        compiler_params=_cparams("arbitrary"),
        name="moe_mlp",
    )(tile_ids, exp_ids, valid, offs, xs, w_gate_up, b_gate_up.reshape(N_EXPERTS, 1, -1),
      w_down, b_down.reshape(N_EXPERTS, 1, -1))


def _combine_kernel(dest_ref, ys_ref, rw_ref, res_ref, mod_ref, g_ref, o_ref, buf_ref, sem):
    n = _ROUTE_TM * TOP_K

    def copy(j):
        return pltpu.make_async_copy(ys_ref.at[pl.ds(dest_ref[j], 1)],
                                     buf_ref.at[j % TOP_K, pl.ds(j // TOP_K, 1)], sem)

    def start(j, carry):
        copy(j).start()
        return carry

    def wait(j, carry):
        copy(j).wait()
        return carry

    lax.fori_loop(0, n, start, 0)
    lax.fori_loop(0, n, wait, 0)
    rw = rw_ref[...]
    ffn = jnp.zeros((_ROUTE_TM, D_MODEL), F32)
    for kk in range(TOP_K):
        ffn = ffn + rw[:, kk:kk + 1] * buf_ref[kk]
    gate_f = mod_ref[0, 5:6, :]
    o_ref[...] = res_ref[...] + gate_f * (_rms(ffn) * g_ref[...])


def _combine(ys, dest, route_w, res, mod, g_post_ffn, *, seq):
    t_total = res.shape[0]
    tiles_per_seq = seq // _ROUTE_TM
    row = lambda i: (i, 0)
    return pl.pallas_call(
        _combine_kernel,
        grid=(t_total // _ROUTE_TM,),
        in_specs=[pl.BlockSpec((_ROUTE_TM * TOP_K,), lambda i: (i,), memory_space=pltpu.SMEM),
                  pl.BlockSpec(memory_space=pl.ANY),
                  pl.BlockSpec((_ROUTE_TM, LANES), row),
                  pl.BlockSpec((_ROUTE_TM, D_MODEL), row),
                  pl.BlockSpec((1, N_MOD, D_MODEL), lambda i: (i // tiles_per_seq, 0, 0)),
                  _const_spec((1, D_MODEL))],
        out_specs=pl.BlockSpec((_ROUTE_TM, D_MODEL), row),
        out_shape=jax.ShapeDtypeStruct((t_total, D_MODEL), F32),
        scratch_shapes=[pltpu.VMEM((TOP_K, _ROUTE_TM, D_MODEL), F32), pltpu.SemaphoreType.DMA(())],
        compiler_params=_cparams("arbitrary"),
        name="moe_combine",
    )(dest, ys, route_w, res, mod, g_post_ffn.reshape(1, -1))


def _forward(x, c, w_ada, b_ada, g_pre_mix, g_post_mix, w_in, conv_w, conv_b, dt_bias, a_log, d_skip,
             g_ssd_norm, w_out, g_pre_ffn, g_post_ffn, w_router, b_router, w_gate_up, b_gate_up, w_down, b_down):
    bsz, seq, _ = x.shape
    x2 = x.astype(F32).reshape(bsz * seq, D_MODEL)
    mod = _ada_mod(c.astype(F32), w_ada, b_ada)
    *qkv, z, xc, dt = _in_proj(x2, mod, g_pre_mix, w_in, conv_w, conv_b, dt_bias, seq=seq)
    attn = []
    for gi, dil in enumerate(DILATIONS):
        q, k, v = (a.reshape(bsz * seq, ATTN_WIDTH) for a in qkv[3 * gi:3 * gi + 3])
        attn.append(_dilated_attention(q, k, v, dil, seq=seq))
    yf, yb = _ssd(xc, dt, a_log, seq=seq)
    res, h2, route_i, route_w, counts = _mix_and_route(
        [o for o, _ in attn], [l for _, l in attn], yf, yb, xc, z, x2, mod, d_skip, g_ssd_norm, w_out,
        g_post_mix, g_pre_ffn, w_router, b_router, seq=seq)
    counts = counts[0, :N_EXPERTS]
    offs = jnp.concatenate([jnp.zeros((1,), jnp.int32), jnp.cumsum(counts)[:-1]]).astype(jnp.int32)
    dest = (offs[route_i[:, :TOP_K]] + route_i[:, TOP_K:2 * TOP_K]).reshape(-1)
    xs = _dispatch(h2, dest)
    ys = _grouped_mlp(xs, counts, w_gate_up, b_gate_up, w_down, b_down)
    out = _combine(ys, dest, route_w, res, mod, g_post_ffn, seq=seq)
    return out.reshape(bsz, seq, D_MODEL)


def kernel(x, c, w_ada, b_ada, g_pre_mix, g_post_mix, w_in, conv_w, conv_b, dt_bias, a_log, d_skip, g_ssd_norm, w_out, g_pre_ffn, g_post_ffn, w_router, b_router, w_gate_up, b_gate_up, w_down, b_down):
    layer = lambda t: t[0].astype(F32)
    out = _forward(x, c, layer(w_ada), layer(b_ada), layer(g_pre_mix), layer(g_post_mix), layer(w_in),
                   layer(conv_w), layer(conv_b), layer(dt_bias), layer(a_log), layer(d_skip),
                   layer(g_ssd_norm), layer(w_out), layer(g_pre_ffn), layer(g_post_ffn), layer(w_router),
                   layer(b_router), layer(w_gate_up), layer(b_gate_up), layer(w_down), layer(b_down))
    return out.astype(x.dtype)
```

```python
import functools

import jax
import jax.numpy as jnp
from jax import lax
from jax.experimental import pallas as pl
from jax.experimental.pallas import tpu as pltpu

F32 = jnp.float32
BF16 = jnp.bfloat16

D_MODEL = 1024
ATTN_HEADS = 16
HEAD_DIM = 64
ATTN_WIDTH = ATTN_HEADS * HEAD_DIM
DILATIONS = (1, 4, 16)
ATTN_HALF = 64
SSD_HEADS = 16
SSD_HEAD_DIM = 64
SSD_WIDTH = SSD_HEADS * SSD_HEAD_DIM
SSD_GROUPS = 2
SSD_STATE = 128
SSD_CHUNK = 128
CONV_WIDTH = 5
XBC_WIDTH = SSD_WIDTH + 2 * SSD_GROUPS * SSD_STATE
N_EXPERTS = 32
TOP_K = 4
EXPERT_FF = 1024
SWIGLU_ALPHA = 1.702
SWIGLU_LIMIT = 7.0
RMS_EPS = 1e-6
N_MOD = 6
LANES = 128
SUBLANES = 8
VMEM_LIMIT = 56 * 1024 * 1024


def _cparams(*sem):
    return pltpu.CompilerParams(dimension_semantics=sem, vmem_limit_bytes=VMEM_LIMIT)


def _const_spec(shape):
    nd = len(shape)
    return pl.BlockSpec(shape, lambda *_: (0,) * nd)


def _split2(a):
    hi = a.astype(BF16)
    lo = (a - hi.astype(F32)).astype(BF16)
    return hi, lo


def _dot(a, b):
    return jnp.dot(a, b, preferred_element_type=F32)


def _dot_f32(a, b):
    ah, al = _split2(a)
    bh, bl = _split2(b)
    return _dot(ah, bh) + (_dot(ah, bl) + _dot(al, bh))


def _sigmoid(x):
    return 1.0 / (1.0 + jnp.exp(-x))


def _rms(x):
    return x * lax.rsqrt(jnp.mean(x * x, axis=-1, keepdims=True) + RMS_EPS)


def _ada_kernel(c_ref, w_ref, b_ref, o_ref):
    c = c_ref[...]
    o_ref[...] = _dot_f32(c * _sigmoid(c), w_ref[...]) + b_ref[...]


def _ada_mod(c, w_ada, b_ada):
    bsz = c.shape[0]
    c8 = jnp.zeros((SUBLANES, D_MODEL), F32).at[:bsz].set(c)
    out = pl.pallas_call(
        _ada_kernel,
        grid=(N_MOD,),
        in_specs=[_const_spec((SUBLANES, D_MODEL)),
                  pl.BlockSpec((D_MODEL, D_MODEL), lambda j: (0, j)),
                  pl.BlockSpec((1, D_MODEL), lambda j: (0, j))],
        out_specs=pl.BlockSpec((SUBLANES, D_MODEL), lambda j: (0, j)),
        out_shape=jax.ShapeDtypeStruct((SUBLANES, N_MOD * D_MODEL), F32),
        compiler_params=_cparams("parallel"),
        name="ada_mod",
    )(c8, w_ada, b_ada.reshape(1, -1))
    return out[:bsz].reshape(bsz, N_MOD, D_MODEL)


def _inproj_kernel(x_ref, xp_ref, xn_ref, mod_ref, g_ref, wqkv_ref, wz_ref, wxbc_ref, wdt_ref,
                   cw_ref, cb_ref, dtb_ref,
                   q1_ref, k1_ref, v1_ref, q4_ref, k4_ref, v4_ref, q16_ref, k16_ref, v16_ref,
                   z_ref, xc_ref, dt_ref, buf_ref, slab_ref, *, tm, tiles_per_seq):
    i = pl.program_id(0)
    shift = mod_ref[0, 0:1, :]
    scale = mod_ref[0, 1:2, :]
    g = g_ref[...]

    def norm_mod(x):
        return _rms(x) * g * (1.0 + scale) + shift

    h = norm_mod(x_ref[...])
    hb = h.astype(BF16)
    n_slab = ATTN_WIDTH // LANES
    outs = ((q1_ref, q4_ref, q16_ref), (k1_ref, k4_ref, k16_ref), (v1_ref, v4_ref, v16_ref))
    for c, (nat_ref, d4_ref, d16_ref) in enumerate(outs):
        r = _dot(hb, wqkv_ref[:, c * ATTN_WIDTH:(c + 1) * ATTN_WIDTH])
        if c == 0:
            r = r * HEAD_DIM ** -0.5
        nat_ref[...] = r.astype(BF16)
        for s in range(n_slab):
            slab_ref[s] = r[:, s * LANES:(s + 1) * LANES]
        for dil, d_ref in ((DILATIONS[1], d4_ref), (DILATIONS[2], d16_ref)):
            for rr in range(dil):
                for s in range(n_slab):
                    d_ref[0, rr, :, s * LANES:(s + 1) * LANES] = (
                        slab_ref[s, pl.ds(rr, tm // dil, stride=dil), :].astype(BF16))
    z_ref[...] = _dot(hb, wz_ref[...]).astype(BF16)
    dt_ref[...] = jax.nn.softplus(_dot_f32(h, wdt_ref[...]) + dtb_ref[...])

    hh = norm_mod(jnp.concatenate([xp_ref[...], xn_ref[...]], axis=0)).astype(BF16)
    halo = _dot(hh, wxbc_ref[...])
    t_in_seq = i % tiles_per_seq
    prev_ok = (t_in_seq != 0).astype(F32)
    next_ok = (t_in_seq != tiles_per_seq - 1).astype(F32)
    buf_ref[0:SUBLANES, :] = halo[0:SUBLANES] * prev_ok
    buf_ref[SUBLANES + tm:, :] = halo[SUBLANES:] * next_ok
    buf_ref[SUBLANES:SUBLANES + tm, :] = _dot(hb, wxbc_ref[...])
    acc = cb_ref[...]
    for j in range(CONV_WIDTH):
        off = SUBLANES - CONV_WIDTH // 2 + j
        acc = acc + buf_ref[off:off + tm, :] * cw_ref[j:j + 1, :]
    xc_ref[...] = (acc * _sigmoid(acc)).astype(BF16)


def _in_proj(x2, mod, g_pre, w_in, conv_w, conv_b, dt_bias, *, seq, tm=512):
    t_total = x2.shape[0]
    nt = t_total // tm
    tiles_per_seq = seq // tm
    hb = tm // SUBLANES
    n_hblk = t_total // SUBLANES
    qkv_w = 3 * ATTN_WIDTH
    w_qkv = w_in[:, :qkv_w].astype(BF16)
    w_z = w_in[:, qkv_w:qkv_w + SSD_WIDTH].astype(BF16)
    w_xbc = w_in[:, qkv_w + SSD_WIDTH:qkv_w + SSD_WIDTH + XBC_WIDTH].astype(BF16)
    w_dt_raw = w_in[:, qkv_w + SSD_WIDTH + XBC_WIDTH:]
    w_dt = jnp.zeros((D_MODEL, 2 * LANES), F32)
    dtb = jnp.zeros((1, 2 * LANES), F32)
    for dr in range(2):
        w_dt = w_dt.at[:, dr * LANES:dr * LANES + SSD_HEADS].set(w_dt_raw[:, dr * SSD_HEADS:(dr + 1) * SSD_HEADS])
        dtb = dtb.at[0, dr * LANES:dr * LANES + SSD_HEADS].set(dt_bias[dr])
    row = lambda i: (i, 0)
    bsz = t_total // seq
    dil_shapes, dil_specs = [], []
    for dil in DILATIONS[1:]:
        dil_shapes += [jax.ShapeDtypeStruct((bsz, dil, seq // dil, ATTN_WIDTH), BF16)] * 3
        dil_specs += [pl.BlockSpec((1, dil, tm // dil, ATTN_WIDTH),
                                   lambda i: (i // tiles_per_seq, 0, i % tiles_per_seq, 0))] * 3
    out_shape = [jax.ShapeDtypeStruct((t_total, ATTN_WIDTH), BF16)] * 3 + dil_shapes + [
        jax.ShapeDtypeStruct((t_total, SSD_WIDTH), BF16),
        jax.ShapeDtypeStruct((t_total, XBC_WIDTH), BF16),
        jax.ShapeDtypeStruct((t_total, 2 * LANES), F32)]
    return pl.pallas_call(
        functools.partial(_inproj_kernel, tm=tm, tiles_per_seq=tiles_per_seq),
        grid=(nt,),
        in_specs=[pl.BlockSpec((tm, D_MODEL), row),
                  pl.BlockSpec((SUBLANES, D_MODEL), lambda i: (jnp.maximum(i * hb - 1, 0), 0)),
                  pl.BlockSpec((SUBLANES, D_MODEL), lambda i: (jnp.minimum((i + 1) * hb, n_hblk - 1), 0)),
                  pl.BlockSpec((1, N_MOD, D_MODEL), lambda i: (i // tiles_per_seq, 0, 0)),
                  _const_spec((1, D_MODEL)),
                  _const_spec((D_MODEL, qkv_w)),
                  _const_spec((D_MODEL, SSD_WIDTH)),
                  _const_spec((D_MODEL, XBC_WIDTH)),
                  _const_spec((D_MODEL, 2 * LANES)),
                  _const_spec((CONV_WIDTH, XBC_WIDTH)),
                  _const_spec((1, XBC_WIDTH)),
                  _const_spec((1, 2 * LANES))],
        out_specs=[pl.BlockSpec((tm, ATTN_WIDTH), row)] * 3 + dil_specs + [
            pl.BlockSpec((tm, SSD_WIDTH), row),
            pl.BlockSpec((tm, XBC_WIDTH), row),
            pl.BlockSpec((tm, 2 * LANES), row)],
        out_shape=out_shape,
        scratch_shapes=[pltpu.VMEM((tm + 2 * SUBLANES, XBC_WIDTH), F32),
                        pltpu.VMEM((ATTN_WIDTH // LANES, tm, LANES), F32)],
        compiler_params=_cparams("parallel"),
        name="in_proj",
    )(x2, x2, x2, mod, g_pre.reshape(1, -1), w_qkv, w_z, w_xbc, w_dt, conv_w, conv_b.reshape(1, -1), dtb)


_NEG = -1e30


def _ssd_direction(xc_ref, dt_ref, a_ref, e_ref, h_ref, y_ref, reverse):
    qn = SSD_CHUNK
    gw = SSD_WIDTH // SSD_GROUPS
    li = lax.broadcasted_iota(jnp.int32, (qn, qn), 0)
    si = lax.broadcasted_iota(jnp.int32, (qn, qn), 1)
    mask = (si >= li) if reverse else (si <= li)
    tri = jnp.where(mask, 1.0, 0.0).astype(BF16)
    dt = dt_ref[...]
    adt = dt * a_ref[...]
    p0 = adt.astype(BF16)
    r0 = adt - p0.astype(F32)
    p1 = r0.astype(BF16)
    p2 = (r0 - p1.astype(F32)).astype(BF16)
    acum = _dot(tri, p0) + (_dot(tri, p1) + _dot(tri, p2))
    last = 0 if reverse else qn - 1
    eo = jnp.exp(acum)
    ds = jnp.exp(acum[last:last + 1, :] - acum)
    sh, sl = _split2(jnp.concatenate([dt, ds, eo], axis=0))
    ex = _dot(sh, e_ref[...]) + _dot(sl, e_ref[...])
    dt_x, ds_x, eo_x = ex[0:qn], ex[qn:2 * qn], ex[2 * qn:3 * qn]
    xd = xc_ref[:, 0:SSD_WIDTH].astype(F32) * dt_x
    xdb = xd.astype(BF16)
    xwb = (xd * ds_x).astype(BF16)
    acum_t = acum.T
    lane = lax.broadcasted_iota(jnp.int32, (qn, LANES), 1)
    for g in range(SSD_GROUPS):
        b0 = SSD_WIDTH + g * SSD_STATE
        c0 = SSD_WIDTH + SSD_GROUPS * SSD_STATE + g * SSD_STATE
        bm = xc_ref[:, b0:b0 + SSD_STATE]
        cm = xc_ref[:, c0:c0 + SSD_STATE]
        cb = lax.dot_general(cm, bm, (((1,), (1,)), ((), ())), preferred_element_type=F32)
        hg = h_ref[g]
        yoff = _dot(cm, hg.astype(BF16))
        st = lax.dot_general(bm, xwb[:, g * gw:(g + 1) * gw], (((0,), (0,)), ((), ())),
                             preferred_element_type=F32)
        h_ref[g] = hg * eo_x[last:last + 1, g * gw:(g + 1) * gw] + st
        for pr in range(gw // LANES):
            col = g * gw + pr * LANES
            xp = xdb[:, col:col + LANES]
            res = []
            for e in (col // SSD_HEAD_DIM, col // SSD_HEAD_DIM + 1):
                seg = acum[:, e:e + 1] - acum_t[e:e + 1, :]
                m = (cb * jnp.exp(jnp.where(mask, seg, _NEG))).astype(BF16)
                res.append(_dot(m, xp))
            yd = jnp.where(lane < SSD_HEAD_DIM, res[0], res[1])
            y = yd + yoff[:, pr * LANES:(pr + 1) * LANES] * eo_x[:, col:col + LANES]
            y_ref[:, col:col + LANES] = y.astype(BF16)


def _ssd_kernel(xcf_ref, xcb_ref, dtf_ref, dtb_ref, a_ref, e_ref, yf_ref, yb_ref, hf_ref, hb_ref):
    @pl.when(pl.program_id(1) == 0)
    def _():
        hf_ref[...] = jnp.zeros_like(hf_ref)
        hb_ref[...] = jnp.zeros_like(hb_ref)

    _ssd_direction(xcf_ref, dtf_ref, a_ref.at[0], e_ref, hf_ref, yf_ref, False)
    _ssd_direction(xcb_ref, dtb_ref, a_ref.at[1], e_ref, hb_ref, yb_ref, True)


def _ssd(xc, dt, a_log, *, seq):
    t_total = xc.shape[0]
    bsz = t_total // seq
    nc = seq // SSD_CHUNK
    a_rows = jnp.zeros((2, 1, LANES), F32).at[:, 0, :SSD_HEADS].set(-jnp.exp(a_log))
    expand = (jnp.arange(LANES)[:, None] == jnp.arange(SSD_WIDTH)[None, :] // SSD_HEAD_DIM).astype(BF16)
    fwd = lambda b, c: (b * nc + c, 0)
    bwd = lambda b, c: (b * nc + nc - 1 - c, 0)
    state = pltpu.VMEM((SSD_GROUPS, SSD_STATE, SSD_WIDTH // SSD_GROUPS), F32)
    return pl.pallas_call(
        _ssd_kernel,
        grid=(bsz, nc),
        in_specs=[pl.BlockSpec((SSD_CHUNK, XBC_WIDTH), fwd),
                  pl.BlockSpec((SSD_CHUNK, XBC_WIDTH), bwd),
                  pl.BlockSpec((SSD_CHUNK, LANES), fwd),
                  pl.BlockSpec((SSD_CHUNK, LANES), lambda b, c: (b * nc + nc - 1 - c, 1)),
                  _const_spec((2, 1, LANES)),
                  _const_spec((LANES, SSD_WIDTH))],
        out_specs=[pl.BlockSpec((SSD_CHUNK, SSD_WIDTH), fwd),
                   pl.BlockSpec((SSD_CHUNK, SSD_WIDTH), bwd)],
        out_shape=[jax.ShapeDtypeStruct((t_total, SSD_WIDTH), BF16)] * 2,
        scratch_shapes=[state, state],
        compiler_params=_cparams("parallel", "arbitrary"),
        name="ssd",
    )(xc, xc, dt, dt, a_rows, expand)


_ATTN_QB = 2 * ATTN_HALF
_ATTN_WIN = _ATTN_QB + 2 * ATTN_HALF
_ATTN_LQ = 512


def _attn_kernel(q_ref, k_ref, kp_ref, kn_ref, v_ref, vp_ref, vn_ref, bias_ref, o_ref, lse_ref,
                 kw_ref, vw_ref, *, lq, sub_len):
    t = pl.program_id(2)
    hb = ATTN_HALF
    for src, halo_p, halo_n, win in ((k_ref, kp_ref, kn_ref, kw_ref), (v_ref, vp_ref, vn_ref, vw_ref)):
        win[0:hb, :] = halo_p[...]
        win[hb:hb + lq, :] = src[...]
        win[hb + lq:, :] = halo_n[...]
    qn = _ATTN_QB
    lane = lax.broadcasted_iota(jnp.int32, (qn, LANES), 1)
    first_head = lane < HEAD_DIM
    kpos = lax.broadcasted_iota(jnp.int32, (1, _ATTN_WIN), 1)

    def body(qb, carry):
        r0 = pl.multiple_of(qb * qn, qn)
        kidx = t * lq + r0 - hb + kpos
        in_seq = (kidx >= 0) & (kidx < sub_len)
        lse_tile = jnp.zeros((qn, LANES), F32)
        for hp in range(ATTN_HEADS // 2):
            cs = slice(hp * LANES, (hp + 1) * LANES)
            q = q_ref[pl.ds(r0, qn), cs]
            zero = jnp.zeros_like(q)
            q2 = jnp.concatenate([jnp.where(first_head, q, zero), jnp.where(first_head, zero, q)], axis=0)
            s = lax.dot_general(q2, kw_ref[pl.ds(r0, _ATTN_WIN), cs], (((1,), (1,)), ((), ())),
                                preferred_element_type=F32) + bias_ref[hp]
            s = jnp.where(in_seq, s, _NEG)
            m = jnp.max(s, axis=-1, keepdims=True)
            p = jnp.exp(s - m)
            den = jnp.sum(p, axis=-1, keepdims=True)
            pv = _dot(p.astype(BF16), vw_ref[pl.ds(r0, _ATTN_WIN), cs]) * (1.0 / den)
            o_ref[pl.ds(r0, qn), cs] = jnp.where(first_head, pv[0:qn], pv[qn:]).astype(BF16)
            lse = m + jnp.log(den)
            lse_tile = jnp.where(lane == 2 * hp, lse[0:qn], jnp.where(lane == 2 * hp + 1, lse[qn:], lse_tile))
        lse_ref[pl.ds(r0, qn), :] = lse_tile
        return carry

    lax.fori_loop(0, lq // qn, body, 0)


def _attn_bias(dilation):
    slopes = jnp.exp2(-8.0 * jnp.arange(1, ATTN_HEADS + 1, dtype=F32) / ATTN_HEADS)
    rel = jnp.abs(jnp.arange(_ATTN_WIN)[None, :] - ATTN_HALF - jnp.arange(_ATTN_QB)[:, None])
    dist = (rel * dilation).astype(F32)
    b = jnp.where(rel <= ATTN_HALF, -slopes[:, None, None] * dist, _NEG)
    return b.reshape(ATTN_HEADS // 2, 2 * _ATTN_QB, _ATTN_WIN)


def _dilated_attention(q, k, v, dilation, *, seq):
    t_total = q.shape[0]
    bsz = t_total // seq
    sub_len = seq // dilation
    lq = min(_ATTN_LQ, sub_len)
    nt = sub_len // lq
    hpb = lq // ATTN_HALF
    n_hblk = t_total // ATTN_HALF
    tile = lambda b, r, t: (b * dilation + r) * nt + t
    main = lambda b, r, t: (tile(b, r, t), 0)
    prev = lambda b, r, t: (jnp.maximum(tile(b, r, t) * hpb - 1, 0), 0)
    nxt = lambda b, r, t: (jnp.minimum((tile(b, r, t) + 1) * hpb, n_hblk - 1), 0)
    blk = pl.BlockSpec((lq, ATTN_WIDTH), main)
    hblk_p = pl.BlockSpec((ATTN_HALF, ATTN_WIDTH), prev)
    hblk_n = pl.BlockSpec((ATTN_HALF, ATTN_WIDTH), nxt)
    window = pltpu.VMEM((lq + 2 * ATTN_HALF, ATTN_WIDTH), BF16)
    return pl.pallas_call(
        functools.partial(_attn_kernel, lq=lq, sub_len=sub_len),
        grid=(bsz, dilation, nt),
        in_specs=[blk, blk, hblk_p, hblk_n, blk, hblk_p, hblk_n,
                  _const_spec((ATTN_HEADS // 2, 2 * _ATTN_QB, _ATTN_WIN))],
        out_specs=[blk, pl.BlockSpec((lq, LANES), main)],
        out_shape=[jax.ShapeDtypeStruct((t_total, ATTN_WIDTH), BF16),
                   jax.ShapeDtypeStruct((t_total, LANES), F32)],
        scratch_shapes=[window, window],
        compiler_params=_cparams("parallel", "parallel", "parallel"),
        name=f"attn_d{dilation}",
    )(q, k, k, k, v, v, v, _attn_bias(dilation))


def _mix_kernel(o0_ref, o1_ref, o2_ref, l0_ref, l1_ref, l2_ref, yf_ref, yb_ref, xs_ref, z_ref, x_ref,
                mod_ref, e_ref, dskip_ref, gssd_ref, wout_ref, gpost_ref, gpre_ref, wr_ref, br_ref,
                res_ref, h2_ref, rank_ref, wts_ref, cnt_ref, lnat_ref, onat_ref, *, tm):
    n_slab = ATTN_WIDTH // LANES

    dilated = ((DILATIONS[1], l1_ref, o1_ref), (DILATIONS[2], l2_ref, o2_ref))
    for gi, (dil, l_ref, _) in enumerate(dilated):
        for rr in range(dil):
            lnat_ref[gi, pl.ds(rr, tm // dil, stride=dil), :] = l_ref[0, rr]

    l0, l1, l2 = l0_ref[...], lnat_ref[0], lnat_ref[1]
    m = jnp.maximum(jnp.maximum(l0, l1), l2)
    es = [jnp.exp(l - m) for l in (l0, l1, l2)]
    inv = 1.0 / (es[0] + es[1] + es[2])
    expand = e_ref[...]

    def widen(w):
        wh, wl = _split2(w)
        return _dot(wh, expand) + _dot(wl, expand)

    attn = widen(es[0] * inv) * o0_ref[...].astype(F32)
    for gi, (dil, _, o_ref) in enumerate(dilated):
        for rr in range(dil):
            for s in range(n_slab):
                onat_ref[s, pl.ds(rr, tm // dil, stride=dil), :] = (
                    o_ref[0, rr, :, s * LANES:(s + 1) * LANES].astype(F32))
        o_nat = jnp.concatenate([onat_ref[s] for s in range(n_slab)], axis=1)
        attn = attn + widen(es[gi + 1] * inv) * o_nat

    xs = xs_ref[...].astype(F32)
    z = z_ref[...].astype(F32)
    y = yf_ref[...].astype(F32) + yb_ref[...].astype(F32) + dskip_ref[...] * xs
    y = _rms(y * (z * _sigmoid(z))) * gssd_ref[...]
    mix = _dot(attn.astype(BF16), wout_ref[0:ATTN_WIDTH, :]) + _dot(y.astype(BF16), wout_ref[ATTN_WIDTH:, :])
    gate_m = mod_ref[0, 2:3, :]
    shift_f = mod_ref[0, 3:4, :]
    scale_f = mod_ref[0, 4:5, :]
    res = x_ref[...] + gate_m * (_rms(mix) * gpost_ref[...])
    res_ref[...] = res
    h2 = _rms(res) * gpre_ref[...] * (1.0 + scale_f) + shift_f
    h2_ref[...] = h2.astype(BF16)

    vals = _dot_f32(h2, wr_ref[...]) + br_ref[...]
    lane = lax.broadcasted_iota(jnp.int32, (tm, LANES), 1)
    sels, tops = [], []
    for _ in range(TOP_K):
        mx = jnp.max(vals, axis=-1, keepdims=True)
        idx = jnp.min(jnp.where(vals == mx, lane, LANES), axis=-1, keepdims=True)
        sel = lane == idx
        sels.append(sel)
        tops.append((mx, idx))
        vals = jnp.where(sel, -jnp.inf, vals)
    ex = [jnp.exp(tv - tops[0][0]) for tv, _ in tops]
    den = ex[0] + ex[1] + ex[2] + ex[3]
    hit = jnp.zeros((tm, LANES), F32)
    wts = jnp.zeros((tm, LANES), F32)
    for kk, sel in enumerate(sels):
        hit = jnp.where(sel, 1.0, hit)
        wts = jnp.where(sel, ex[kk] / den, wts)
    ri = lax.broadcasted_iota(jnp.int32, (tm, tm), 0)
    ci = lax.broadcasted_iota(jnp.int32, (tm, tm), 1)
    before = jnp.where(ci < ri, 1.0, 0.0).astype(BF16)
    rank = _dot(before, hit.astype(BF16))
    rank_ref[...] = jnp.where(hit > 0.0, rank, -1.0)
    wts_ref[...] = wts
    cnt = jnp.sum(hit, axis=0, keepdims=True)
    cnt_ref[0] = jnp.broadcast_to(cnt, (SUBLANES, LANES)).astype(jnp.int32)


def _mix_and_route(outs, lses, yf, yb, xc, z, x2, mod, d_skip, g_ssd, w_out, g_post, g_pre_ffn,
                   w_router, b_router, *, seq, tm=256):
    t_total = x2.shape[0]
    nt = t_total // tm
    tiles_per_seq = seq // tm
    expand = (jnp.arange(LANES)[:, None] == jnp.arange(ATTN_WIDTH)[None, :] // HEAD_DIM).astype(BF16)
    wr = jnp.zeros((D_MODEL, LANES), F32).at[:, :N_EXPERTS].set(w_router)
    br = jnp.full((1, LANES), _NEG, F32).at[0, :N_EXPERTS].set(b_router)
    row = lambda i: (i, 0)
    wide = pl.BlockSpec((tm, D_MODEL), row)
    narrow = pl.BlockSpec((tm, LANES), row)
    vec = _const_spec((1, D_MODEL))
    bsz = t_total // seq

    def dilated(a, dil):
        a = a.reshape(bsz, dil, seq // dil, a.shape[-1])
        spec = pl.BlockSpec((1, dil, tm // dil, a.shape[-1]),
                            lambda i: (i // tiles_per_seq, 0, i % tiles_per_seq, 0))
        return a, spec

    o_args, o_specs = [outs[0]], [wide]
    l_args, l_specs = [lses[0]], [narrow]
    for dil, o, l in zip(DILATIONS[1:], outs[1:], lses[1:]):
        a, spec = dilated(o, dil)
        o_args.append(a)
        o_specs.append(spec)
        a, spec = dilated(l, dil)
        l_args.append(a)
        l_specs.append(spec)
    outs, lses = o_args, l_args
    return pl.pallas_call(
        functools.partial(_mix_kernel, tm=tm),
        grid=(nt,),
        in_specs=o_specs + l_specs + [wide] * 5 + [
            pl.BlockSpec((1, N_MOD, D_MODEL), lambda i: (i // tiles_per_seq, 0, 0)),
            _const_spec((LANES, ATTN_WIDTH)), vec, vec,
            _const_spec((ATTN_WIDTH + SSD_WIDTH, D_MODEL)), vec, vec,
            _const_spec((D_MODEL, LANES)), _const_spec((1, LANES))],
        out_specs=[wide, wide, narrow, narrow, pl.BlockSpec((1, SUBLANES, LANES), lambda i: (i, 0, 0))],
        out_shape=[jax.ShapeDtypeStruct((t_total, D_MODEL), F32),
                   jax.ShapeDtypeStruct((t_total, D_MODEL), BF16),
                   jax.ShapeDtypeStruct((t_total, LANES), F32),
                   jax.ShapeDtypeStruct((t_total, LANES), F32),
                   jax.ShapeDtypeStruct((nt, SUBLANES, LANES), jnp.int32)],
        scratch_shapes=[pltpu.VMEM((len(DILATIONS) - 1, tm, LANES), F32),
                        pltpu.VMEM((ATTN_WIDTH // LANES, tm, LANES), F32)],
        compiler_params=_cparams("parallel"),
        name="mix_route",
    )(*outs, *lses, yf, yb, xc, z, x2, mod, expand, jnp.repeat(d_skip, SSD_HEAD_DIM).reshape(1, -1),
      g_ssd.reshape(1, -1), w_out.astype(BF16), g_post.reshape(1, -1), g_pre_ffn.reshape(1, -1), wr, br)


_MOE_TM = 256
_ROUTE_TM = 256
_MOE_CAP = 64
_MOE_GROUP = 8
_ROW_SPLIT = D_MODEL // LANES


def _store_rows(ref, lead, row0, vals):
    n = vals.shape[0]
    for q in range(_ROW_SPLIT):
        ref[lead, pl.ds(row0 * _ROW_SPLIT + q, n, stride=_ROW_SPLIT), :] = vals[:, q * LANES:(q + 1) * LANES]


def _load_rows(ref, lead, row0, n):
    parts = [ref[lead, pl.ds(row0 * _ROW_SPLIT + q, n, stride=_ROW_SPLIT), :] for q in range(_ROW_SPLIT)]
    return jnp.concatenate(parts, axis=1)


def _sorted_rows(t_total):
    return t_total * TOP_K + N_EXPERTS * _MOE_CAP


def _dispatch_kernel(seg_ref, n_ref, nch_ref, rank_ref, h_ref, xs_ref, stage_ref, sem, *, nt):
    i = pl.program_id(0)
    slot = i % 2
    cap = _MOE_CAP
    rank = jnp.where(i < nt, rank_ref[...], -1.0)
    rank_t = rank.T
    jrow = lax.broadcasted_iota(jnp.int32, (cap, _ROUTE_TM), 0).astype(F32)
    hb = h_ref[...]

    def compact(c, dst_slot):
        target = jrow + jnp.asarray(c * cap, F32)
        for g in range(N_EXPERTS // _MOE_GROUP):
            sel = [jnp.where(rank_t[e:e + 1, :] == target, 1.0, 0.0).astype(BF16)
                   for e in range(g * _MOE_GROUP, (g + 1) * _MOE_GROUP)]
            rows = _dot(jnp.concatenate(sel, axis=0), hb)
            _store_rows(stage_ref, dst_slot, g * _MOE_GROUP * cap, rows)

    def seg_copy(step, e, c, s):
        dst = pl.multiple_of((seg_ref[step * N_EXPERTS + e] + c * cap) * _ROW_SPLIT, _ROW_SPLIT)
        return pltpu.make_async_copy(stage_ref.at[s, pl.ds(e * cap * _ROW_SPLIT, cap * _ROW_SPLIT)],
                                     xs_ref.at[pl.ds(dst, cap * _ROW_SPLIT)], sem.at[s])

    compact(0, slot)

    @pl.when(i > 0)
    def _():
        for e in range(N_EXPERTS):
            seg_copy(i - 1, e, 0, 1 - slot).wait()

    for e in range(N_EXPERTS):
        seg_copy(i, e, 0, slot).start()

    def overflow_chunk(c, carry):
        compact(c, 1 - slot)
        for e in range(N_EXPERTS):
            @pl.when(n_ref[i * N_EXPERTS + e] > c * cap)
            def _():
                seg_copy(i, e, c, 1 - slot).start()
        for e in range(N_EXPERTS):
            @pl.when(n_ref[i * N_EXPERTS + e] > c * cap)
            def _():
                seg_copy(i, e, c, 1 - slot).wait()
        return carry

    lax.fori_loop(1, nch_ref[i], overflow_chunk, 0)

    @pl.when(i == nt)
    def _():
        for e in range(N_EXPERTS):
            seg_copy(i, e, 0, slot).wait()


def _dispatch(h2, rank, seg, n_seg, n_chunks):
    t_total = h2.shape[0]
    nt = t_total // _ROUTE_TM
    n_rows = _sorted_rows(t_total)
    tile = lambda i, *_: (jnp.minimum(i, nt - 1), 0)
    grid_spec = pltpu.PrefetchScalarGridSpec(
        num_scalar_prefetch=3,
        grid=(nt + 1,),
        in_specs=[pl.BlockSpec((_ROUTE_TM, LANES), tile),
                  pl.BlockSpec((_ROUTE_TM, D_MODEL), tile)],
        out_specs=pl.BlockSpec(memory_space=pl.ANY),
        scratch_shapes=[pltpu.VMEM((2, N_EXPERTS * _MOE_CAP * _ROW_SPLIT, LANES), F32),
                        pltpu.SemaphoreType.DMA((2,))])
    return pl.pallas_call(
        functools.partial(_dispatch_kernel, nt=nt),
        grid_spec=grid_spec,
        out_shape=jax.ShapeDtypeStruct((n_rows * _ROW_SPLIT, LANES), F32),
        compiler_params=_cparams("arbitrary"),
        name="moe_dispatch",
    )(seg, n_seg, n_chunks, rank, h2)


def _gmm_kernel(tile_ref, exp_ref, valid_ref, offs_ref, x_ref, wgu_ref, bgu_ref, wdn_ref, bdn_ref, y_ref,
                wgu_bf, wdn_bf):
    j = pl.program_id(0)
    e = exp_ref[j]
    tile = tile_ref[j]
    prev_j = jnp.maximum(j - 1, 0)
    new_expert = (j == 0) | (exp_ref[prev_j] != e)
    new_tile = (j == 0) | (tile_ref[prev_j] != tile)

    @pl.when(new_expert)
    def _():
        wgu_bf[...] = wgu_ref[0].astype(BF16)
        wdn_bf[...] = wdn_ref[0].astype(BF16)

    @pl.when(valid_ref[j] != 0)
    def _():
        gu = _dot(_load_rows(x_ref, 0, 0, _MOE_TM).astype(BF16), wgu_bf[...]) + bgu_ref[0]
        gate = jnp.minimum(gu[:, :EXPERT_FF], SWIGLU_LIMIT)
        up = jnp.clip(gu[:, EXPERT_FF:], -SWIGLU_LIMIT, SWIGLU_LIMIT)
        act = (up + 1.0) * gate * _sigmoid(SWIGLU_ALPHA * gate)
        y = _dot(act.astype(BF16), wdn_bf[...]) + bdn_ref[0]
        rows = tile * _MOE_TM + lax.broadcasted_iota(jnp.int32, (_MOE_TM, 1), 0)
        mine = (rows >= offs_ref[e]) & (rows < offs_ref[e + 1])

        @pl.when(new_tile)
        def _():
            _store_rows(y_ref, 0, 0, jnp.where(mine, y, jnp.zeros_like(y)))

        @pl.when(jnp.logical_not(new_tile))
        def _():
            _store_rows(y_ref, 0, 0, jnp.where(mine, y, _load_rows(y_ref, 0, 0, _MOE_TM)))


def _grouped_mlp(xs, offs, w_gate_up, b_gate_up, w_down, b_down):
    n_tiles = xs.shape[0] // (_MOE_TM * _ROW_SPLIT)
    n_visits = n_tiles + N_EXPERTS - 1
    first = offs[:-1] // _MOE_TM
    last = (offs[1:] - 1) // _MOE_TM
    per_expert = last - first + 1
    vstart = jnp.concatenate([jnp.zeros((1,), jnp.int32), jnp.cumsum(per_expert)]).astype(jnp.int32)
    total = vstart[-1]
    vis = jnp.minimum(jnp.arange(n_visits, dtype=jnp.int32), total - 1)
    exp_ids = (jnp.sum(vstart[None, :] <= vis[:, None], axis=1) - 1).astype(jnp.int32)
    tile_ids = (first[exp_ids] + vis - vstart[exp_ids]).astype(jnp.int32)
    valid = (jnp.arange(n_visits) < total).astype(jnp.int32)
    grid_spec = pltpu.PrefetchScalarGridSpec(
        num_scalar_prefetch=4,
        grid=(n_visits,),
        in_specs=[pl.BlockSpec((1, _MOE_TM * _ROW_SPLIT, LANES), lambda j, t, e, v, o: (t[j], 0, 0)),
                  pl.BlockSpec((1, D_MODEL, 2 * EXPERT_FF), lambda j, t, e, v, o: (e[j], 0, 0)),
                  pl.BlockSpec((1, 1, 2 * EXPERT_FF), lambda j, t, e, v, o: (e[j], 0, 0)),
                  pl.BlockSpec((1, EXPERT_FF, D_MODEL), lambda j, t, e, v, o: (e[j], 0, 0)),
                  pl.BlockSpec((1, 1, D_MODEL), lambda j, t, e, v, o: (e[j], 0, 0))],
        out_specs=pl.BlockSpec((1, _MOE_TM * _ROW_SPLIT, LANES), lambda j, t, e, v, o: (t[j], 0, 0)),
        scratch_shapes=[pltpu.VMEM((D_MODEL, 2 * EXPERT_FF), BF16), pltpu.VMEM((EXPERT_FF, D_MODEL), BF16)])
    tiled = (n_tiles, _MOE_TM * _ROW_SPLIT, LANES)
    ys = pl.pallas_call(
        _gmm_kernel,
        grid_spec=grid_spec,
        out_shape=jax.ShapeDtypeStruct(tiled, F32),
        compiler_params=_cparams("arbitrary"),
        name="moe_mlp",
    )(tile_ids, exp_ids, valid, offs, xs.reshape(tiled), w_gate_up, b_gate_up.reshape(N_EXPERTS, 1, -1),
      w_down, b_down.reshape(N_EXPERTS, 1, -1))
    return ys.reshape(xs.shape)


def _combine_kernel(seg_ref, n_ref, nch_ref, ys_ref, rank_ref, wts_ref, e_ref, res_ref, mod_ref, g_ref,
                    o_ref, ybuf_ref, sem, *, nt):
    i = pl.program_id(0)
    slot = i % 2
    cap = _MOE_CAP
    gcols = _MOE_GROUP * cap

    def seg_copy(step, e, c, s):
        src = pl.multiple_of((seg_ref[step * N_EXPERTS + e] + c * cap) * _ROW_SPLIT, _ROW_SPLIT)
        return pltpu.make_async_copy(ys_ref.at[pl.ds(src, cap * _ROW_SPLIT)],
                                     ybuf_ref.at[s, pl.ds(e * cap * _ROW_SPLIT, cap * _ROW_SPLIT)], sem.at[s])

    @pl.when(i == 0)
    def _():
        for e in range(N_EXPERTS):
            seg_copy(0, e, 0, 0).start()

    @pl.when(i + 1 < nt)
    def _():
        for e in range(N_EXPERTS):
            seg_copy(i + 1, e, 0, 1 - slot).start()

    for e in range(N_EXPERTS):
        seg_copy(i, e, 0, slot).wait()

    tm = _ROUTE_TM
    wh, wl = _split2(wts_ref[...])
    stacked = jnp.concatenate([rank_ref[...].astype(BF16), wh, wl], axis=0)
    jcol = (lax.broadcasted_iota(jnp.int32, (tm, gcols), 1) % cap).astype(F32)

    def accumulate(c, ffn):
        target = jcol + jnp.asarray(c * cap, F32)
        for g in range(N_EXPERTS // _MOE_GROUP):
            wide = _dot(stacked, e_ref[:, g * gcols:(g + 1) * gcols])
            hit = wide[0:tm] == target
            p2 = jnp.concatenate([jnp.where(hit, wide[tm:2 * tm], 0.0).astype(BF16),
                                  jnp.where(hit, wide[2 * tm:], 0.0).astype(BF16)], axis=0)
            y = _load_rows(ybuf_ref, slot, g * gcols, gcols).astype(BF16)
            part = _dot(p2, y)
            ffn = ffn + (part[0:tm] + part[tm:])
        return ffn

    ffn = accumulate(0, jnp.zeros((_ROUTE_TM, D_MODEL), F32))

    def overflow_chunk(c, ffn):
        for e in range(N_EXPERTS):
            @pl.when(n_ref[i * N_EXPERTS + e] > c * cap)
            def _():
                seg_copy(i, e, c, slot).start()
        for e in range(N_EXPERTS):
            @pl.when(n_ref[i * N_EXPERTS + e] > c * cap)
            def _():
                seg_copy(i, e, c, slot).wait()
        return accumulate(c, ffn)

    ffn = lax.fori_loop(1, nch_ref[i], overflow_chunk, ffn)
    gate_f = mod_ref[0, 5:6, :]
    o_ref[...] = res_ref[...] + gate_f * (_rms(ffn) * g_ref[...])


def _combine(ys, rank, wts, seg, n_seg, n_chunks, res, mod, g_post_ffn, *, seq):
    t_total = res.shape[0]
    nt = t_total // _ROUTE_TM
    tiles_per_seq = seq // _ROUTE_TM
    row = lambda i, *_: (i, 0)
    widen = (jnp.arange(LANES)[:, None] == jnp.arange(N_EXPERTS * _MOE_CAP)[None, :] // _MOE_CAP).astype(BF16)
    grid_spec = pltpu.PrefetchScalarGridSpec(
        num_scalar_prefetch=3,
        grid=(nt,),
        in_specs=[pl.BlockSpec(memory_space=pl.ANY),
                  pl.BlockSpec((_ROUTE_TM, LANES), row),
                  pl.BlockSpec((_ROUTE_TM, LANES), row),
                  pl.BlockSpec((LANES, N_EXPERTS * _MOE_CAP), lambda i, *_: (0, 0)),
                  pl.BlockSpec((_ROUTE_TM, D_MODEL), row),
                  pl.BlockSpec((1, N_MOD, D_MODEL), lambda i, *_: (i // tiles_per_seq, 0, 0)),
                  pl.BlockSpec((1, D_MODEL), lambda i, *_: (0, 0))],
        out_specs=pl.BlockSpec((_ROUTE_TM, D_MODEL), row),
        scratch_shapes=[pltpu.VMEM((2, N_EXPERTS * _MOE_CAP * _ROW_SPLIT, LANES), F32),
                        pltpu.SemaphoreType.DMA((2,))])
    return pl.pallas_call(
        functools.partial(_combine_kernel, nt=nt),
        grid_spec=grid_spec,
        out_shape=jax.ShapeDtypeStruct((t_total, D_MODEL), F32),
        compiler_params=_cparams("arbitrary"),
        name="moe_combine",
    )(seg, n_seg, n_chunks, ys, rank, wts, widen, res, mod, g_post_ffn.reshape(1, -1))


def _forward(x, c, w_ada, b_ada, g_pre_mix, g_post_mix, w_in, conv_w, conv_b, dt_bias, a_log, d_skip,
             g_ssd_norm, w_out, g_pre_ffn, g_post_ffn, w_router, b_router, w_gate_up, b_gate_up, w_down, b_down):
    bsz, seq, _ = x.shape
    x2 = x.astype(F32).reshape(bsz * seq, D_MODEL)
    mod = _ada_mod(c.astype(F32), w_ada, b_ada)
    *qkv, z, xc, dt = _in_proj(x2, mod, g_pre_mix, w_in, conv_w, conv_b, dt_bias, seq=seq)
    attn = []
    for gi, dil in enumerate(DILATIONS):
        q, k, v = (a.reshape(bsz * seq, ATTN_WIDTH) for a in qkv[3 * gi:3 * gi + 3])
        attn.append(_dilated_attention(q, k, v, dil, seq=seq))
    yf, yb = _ssd(xc, dt, a_log, seq=seq)
    res, h2, rank, wts, cnt = _mix_and_route(
        [o for o, _ in attn], [l for _, l in attn], yf, yb, xc, z, x2, mod, d_skip, g_ssd_norm, w_out,
        g_post_mix, g_pre_ffn, w_router, b_router, seq=seq, tm=_ROUTE_TM)
    n_te = cnt[:, 0, :N_EXPERTS]
    zero_row = jnp.zeros((1, N_EXPERTS), jnp.int32)
    region = jnp.sum(n_te, axis=0) + _MOE_CAP
    offs = jnp.concatenate([jnp.zeros((1,), jnp.int32), jnp.cumsum(region)]).astype(jnp.int32)
    before = jnp.concatenate([zero_row, jnp.cumsum(n_te, axis=0)], axis=0)
    seg = (offs[None, :N_EXPERTS] + before).astype(jnp.int32).reshape(-1)
    n_seg = jnp.concatenate([n_te, zero_row], axis=0)
    n_chunks = jnp.maximum((jnp.max(n_seg, axis=1) + _MOE_CAP - 1) // _MOE_CAP, 1).astype(jnp.int32)
    n_seg = n_seg.reshape(-1)
    xs = _dispatch(h2, rank, seg, n_seg, n_chunks)
    ys = _grouped_mlp(xs, offs, w_gate_up, b_gate_up, w_down, b_down)
    out = _combine(ys, rank, wts, seg, n_seg, n_chunks, res, mod, g_post_ffn, seq=seq)
    return out.reshape(bsz, seq, D_MODEL)


def kernel(x, c, w_ada, b_ada, g_pre_mix, g_post_mix, w_in, conv_w, conv_b, dt_bias, a_log, d_skip, g_ssd_norm, w_out, g_pre_ffn, g_post_ffn, w_router, b_router, w_gate_up, b_gate_up, w_down, b_down):
    layer = lambda t: t[0].astype(F32)
    out = _forward(x, c, layer(w_ada), layer(b_ada), layer(g_pre_mix), layer(g_post_mix), layer(w_in),
                   layer(conv_w), layer(conv_b), layer(dt_bias), layer(a_log), layer(d_skip),
                   layer(g_ssd_norm), layer(w_out), layer(g_pre_ffn), layer(g_post_ffn), layer(w_router),
                   layer(b_router), layer(w_gate_up), layer(b_gate_up), layer(w_down), layer(b_down))
    return out.astype(x.dtype)
```

```python
import functools

import jax
import jax.numpy as jnp
from jax import lax
from jax.experimental import pallas as pl
from jax.experimental.pallas import tpu as pltpu

F32 = jnp.float32
BF16 = jnp.bfloat16

D_MODEL = 1024
ATTN_HEADS = 16
HEAD_DIM = 64
ATTN_WIDTH = ATTN_HEADS * HEAD_DIM
DILATIONS = (1, 4, 16)
ATTN_HALF = 64
SSD_HEADS = 16
SSD_HEAD_DIM = 64
SSD_WIDTH = SSD_HEADS * SSD_HEAD_DIM
SSD_GROUPS = 2
SSD_STATE = 128
SSD_CHUNK = 128
CONV_WIDTH = 5
XBC_WIDTH = SSD_WIDTH + 2 * SSD_GROUPS * SSD_STATE
N_EXPERTS = 32
TOP_K = 4
EXPERT_FF = 1024
SWIGLU_ALPHA = 1.702
SWIGLU_LIMIT = 7.0
RMS_EPS = 1e-6
N_MOD = 6
LANES = 128
SUBLANES = 8
VMEM_LIMIT = 56 * 1024 * 1024


def _cparams(*sem):
    return pltpu.CompilerParams(dimension_semantics=sem, vmem_limit_bytes=VMEM_LIMIT)


def _const_spec(shape):
    nd = len(shape)
    return pl.BlockSpec(shape, lambda *_: (0,) * nd)


def _split2(a):
    hi = a.astype(BF16)
    lo = (a - hi.astype(F32)).astype(BF16)
    return hi, lo


def _dot(a, b):
    return jnp.dot(a, b, preferred_element_type=F32)


def _dot_f32(a, b):
    ah, al = _split2(a)
    bh, bl = _split2(b)
    return _dot(ah, bh) + (_dot(ah, bl) + _dot(al, bh))


def _sigmoid(x):
    return 1.0 / (1.0 + jnp.exp(-x))


def _rms(x):
    return x * lax.rsqrt(jnp.mean(x * x, axis=-1, keepdims=True) + RMS_EPS)


def _ada_kernel(c_ref, w_ref, b_ref, o_ref):
    c = c_ref[...]
    o_ref[...] = _dot_f32(c * _sigmoid(c), w_ref[...]) + b_ref[...]


def _ada_mod(c, w_ada, b_ada):
    bsz = c.shape[0]
    c8 = jnp.zeros((SUBLANES, D_MODEL), F32).at[:bsz].set(c)
    out = pl.pallas_call(
        _ada_kernel,
        grid=(N_MOD,),
        in_specs=[_const_spec((SUBLANES, D_MODEL)),
                  pl.BlockSpec((D_MODEL, D_MODEL), lambda j: (0, j)),
                  pl.BlockSpec((1, D_MODEL), lambda j: (0, j))],
        out_specs=pl.BlockSpec((SUBLANES, D_MODEL), lambda j: (0, j)),
        out_shape=jax.ShapeDtypeStruct((SUBLANES, N_MOD * D_MODEL), F32),
        compiler_params=_cparams("parallel"),
        name="ada_mod",
    )(c8, w_ada, b_ada.reshape(1, -1))
    return out[:bsz].reshape(bsz, N_MOD, D_MODEL)


def _inproj_kernel(x_ref, xp_ref, xn_ref, mod_ref, g_ref, wqkv_ref, wz_ref, wxbc_ref, wdt_ref,
                   cw_ref, cb_ref, dtb_ref,
                   q1_ref, k1_ref, v1_ref, q4_ref, k4_ref, v4_ref, q16_ref, k16_ref, v16_ref,
                   z_ref, xc_ref, dt_ref, buf_ref, slab_ref, *, tm, tiles_per_seq):
    i = pl.program_id(0)
    shift = mod_ref[0, 0:1, :]
    scale = mod_ref[0, 1:2, :]
    g = g_ref[...]

    def norm_mod(x):
        return _rms(x) * g * (1.0 + scale) + shift

    h = norm_mod(x_ref[...])
    hb = h.astype(BF16)
    n_slab = ATTN_WIDTH // LANES
    outs = ((q1_ref, q4_ref, q16_ref), (k1_ref, k4_ref, k16_ref), (v1_ref, v4_ref, v16_ref))
    for c, (nat_ref, d4_ref, d16_ref) in enumerate(outs):
        r = _dot(hb, wqkv_ref[:, c * ATTN_WIDTH:(c + 1) * ATTN_WIDTH])
        if c == 0:
            r = r * HEAD_DIM ** -0.5
        nat_ref[...] = r.astype(BF16)
        for s in range(n_slab):
            slab_ref[s] = r[:, s * LANES:(s + 1) * LANES]
        for dil, d_ref in ((DILATIONS[1], d4_ref), (DILATIONS[2], d16_ref)):
            for rr in range(dil):
                for s in range(n_slab):
                    d_ref[0, rr, :, s * LANES:(s + 1) * LANES] = (
                        slab_ref[s, pl.ds(rr, tm // dil, stride=dil), :].astype(BF16))
    z_ref[...] = _dot(hb, wz_ref[...]).astype(BF16)
    dt_ref[...] = jax.nn.softplus(_dot_f32(h, wdt_ref[...]) + dtb_ref[...])

    hh = norm_mod(jnp.concatenate([xp_ref[...], xn_ref[...]], axis=0)).astype(BF16)
    halo = _dot(hh, wxbc_ref[...])
    t_in_seq = i % tiles_per_seq
    prev_ok = (t_in_seq != 0).astype(F32)
    next_ok = (t_in_seq != tiles_per_seq - 1).astype(F32)
    buf_ref[0:SUBLANES, :] = halo[0:SUBLANES] * prev_ok
    buf_ref[SUBLANES + tm:, :] = halo[SUBLANES:] * next_ok
    buf_ref[SUBLANES:SUBLANES + tm, :] = _dot(hb, wxbc_ref[...])
    acc = cb_ref[...]
    for j in range(CONV_WIDTH):
        off = SUBLANES - CONV_WIDTH // 2 + j
        acc = acc + buf_ref[off:off + tm, :] * cw_ref[j:j + 1, :]
    xc_ref[...] = (acc * _sigmoid(acc)).astype(BF16)


def _in_proj(x2, mod, g_pre, w_in, conv_w, conv_b, dt_bias, *, seq, tm=512):
    t_total = x2.shape[0]
    nt = t_total // tm
    tiles_per_seq = seq // tm
    hb = tm // SUBLANES
    n_hblk = t_total // SUBLANES
    qkv_w = 3 * ATTN_WIDTH
    w_qkv = w_in[:, :qkv_w].astype(BF16)
    w_z = w_in[:, qkv_w:qkv_w + SSD_WIDTH].astype(BF16)
    w_xbc = w_in[:, qkv_w + SSD_WIDTH:qkv_w + SSD_WIDTH + XBC_WIDTH].astype(BF16)
    w_dt_raw = w_in[:, qkv_w + SSD_WIDTH + XBC_WIDTH:]
    w_dt = jnp.zeros((D_MODEL, 2 * LANES), F32)
    dtb = jnp.zeros((1, 2 * LANES), F32)
    for dr in range(2):
        w_dt = w_dt.at[:, dr * LANES:dr * LANES + SSD_HEADS].set(w_dt_raw[:, dr * SSD_HEADS:(dr + 1) * SSD_HEADS])
        dtb = dtb.at[0, dr * LANES:dr * LANES + SSD_HEADS].set(dt_bias[dr])
    row = lambda i: (i, 0)
    bsz = t_total // seq
    dil_shapes, dil_specs = [], []
    for dil in DILATIONS[1:]:
        dil_shapes += [jax.ShapeDtypeStruct((bsz, dil, seq // dil, ATTN_WIDTH), BF16)] * 3
        dil_specs += [pl.BlockSpec((1, dil, tm // dil, ATTN_WIDTH),
                                   lambda i: (i // tiles_per_seq, 0, i % tiles_per_seq, 0))] * 3
    out_shape = [jax.ShapeDtypeStruct((t_total, ATTN_WIDTH), BF16)] * 3 + dil_shapes + [
        jax.ShapeDtypeStruct((t_total, SSD_WIDTH), BF16),
        jax.ShapeDtypeStruct((t_total, XBC_WIDTH), BF16),
        jax.ShapeDtypeStruct((t_total, 2 * LANES), F32)]
    return pl.pallas_call(
        functools.partial(_inproj_kernel, tm=tm, tiles_per_seq=tiles_per_seq),
        grid=(nt,),
        in_specs=[pl.BlockSpec((tm, D_MODEL), row),
                  pl.BlockSpec((SUBLANES, D_MODEL), lambda i: (jnp.maximum(i * hb - 1, 0), 0)),
                  pl.BlockSpec((SUBLANES, D_MODEL), lambda i: (jnp.minimum((i + 1) * hb, n_hblk - 1), 0)),
                  pl.BlockSpec((1, N_MOD, D_MODEL), lambda i: (i // tiles_per_seq, 0, 0)),
                  _const_spec((1, D_MODEL)),
                  _const_spec((D_MODEL, qkv_w)),
                  _const_spec((D_MODEL, SSD_WIDTH)),
                  _const_spec((D_MODEL, XBC_WIDTH)),
                  _const_spec((D_MODEL, 2 * LANES)),
                  _const_spec((CONV_WIDTH, XBC_WIDTH)),
                  _const_spec((1, XBC_WIDTH)),
                  _const_spec((1, 2 * LANES))],
        out_specs=[pl.BlockSpec((tm, ATTN_WIDTH), row)] * 3 + dil_specs + [
            pl.BlockSpec((tm, SSD_WIDTH), row),
            pl.BlockSpec((tm, XBC_WIDTH), row),
            pl.BlockSpec((tm, 2 * LANES), row)],
        out_shape=out_shape,
        scratch_shapes=[pltpu.VMEM((tm + 2 * SUBLANES, XBC_WIDTH), F32),
                        pltpu.VMEM((ATTN_WIDTH // LANES, tm, LANES), F32)],
        compiler_params=_cparams("parallel"),
        name="in_proj",
    )(x2, x2, x2, mod, g_pre.reshape(1, -1), w_qkv, w_z, w_xbc, w_dt, conv_w, conv_b.reshape(1, -1), dtb)


_NEG = -1e30


def _ssd_direction(xc_ref, dt_ref, a_ref, e_ref, h_ref, y_ref, reverse):
    qn = SSD_CHUNK
    gw = SSD_WIDTH // SSD_GROUPS
    li = lax.broadcasted_iota(jnp.int32, (qn, qn), 0)
    si = lax.broadcasted_iota(jnp.int32, (qn, qn), 1)
    mask = (si >= li) if reverse else (si <= li)
    tri = jnp.where(mask, 1.0, 0.0).astype(BF16)
    dt = dt_ref[...]
    adt = dt * a_ref[...]
    p0 = adt.astype(BF16)
    r0 = adt - p0.astype(F32)
    p1 = r0.astype(BF16)
    p2 = (r0 - p1.astype(F32)).astype(BF16)
    acum = _dot(tri, p0) + (_dot(tri, p1) + _dot(tri, p2))
    last = 0 if reverse else qn - 1
    eo = jnp.exp(acum)
    ds = jnp.exp(acum[last:last + 1, :] - acum)
    sh, sl = _split2(jnp.concatenate([dt, ds, eo], axis=0))
    ex = _dot(sh, e_ref[...]) + _dot(sl, e_ref[...])
    dt_x, ds_x, eo_x = ex[0:qn], ex[qn:2 * qn], ex[2 * qn:3 * qn]
    xd = xc_ref[:, 0:SSD_WIDTH].astype(F32) * dt_x
    xdb = xd.astype(BF16)
    xwb = (xd * ds_x).astype(BF16)
    acum_t = acum.T
    lane = lax.broadcasted_iota(jnp.int32, (qn, LANES), 1)
    for g in range(SSD_GROUPS):
        b0 = SSD_WIDTH + g * SSD_STATE
        c0 = SSD_WIDTH + SSD_GROUPS * SSD_STATE + g * SSD_STATE
        bm = xc_ref[:, b0:b0 + SSD_STATE]
        cm = xc_ref[:, c0:c0 + SSD_STATE]
        cb = lax.dot_general(cm, bm, (((1,), (1,)), ((), ())), preferred_element_type=F32)
        hg = h_ref[g]
        yoff = _dot(cm, hg.astype(BF16))
        st = lax.dot_general(bm, xwb[:, g * gw:(g + 1) * gw], (((0,), (0,)), ((), ())),
                             preferred_element_type=F32)
        h_ref[g] = hg * eo_x[last:last + 1, g * gw:(g + 1) * gw] + st
        for pr in range(gw // LANES):
            col = g * gw + pr * LANES
            xp = xdb[:, col:col + LANES]
            res = []
            for e in (col // SSD_HEAD_DIM, col // SSD_HEAD_DIM + 1):
                seg = acum[:, e:e + 1] - acum_t[e:e + 1, :]
                m = (cb * jnp.exp(jnp.where(mask, seg, _NEG))).astype(BF16)
                res.append(_dot(m, xp))
            yd = jnp.where(lane < SSD_HEAD_DIM, res[0], res[1])
            y = yd + yoff[:, pr * LANES:(pr + 1) * LANES] * eo_x[:, col:col + LANES]
            y_ref[:, col:col + LANES] = y.astype(BF16)


def _ssd_kernel(xcf_ref, xcb_ref, dtf_ref, dtb_ref, a_ref, e_ref, yf_ref, yb_ref, hf_ref, hb_ref):
    @pl.when(pl.program_id(1) == 0)
    def _():
        hf_ref[...] = jnp.zeros_like(hf_ref)
        hb_ref[...] = jnp.zeros_like(hb_ref)

    _ssd_direction(xcf_ref, dtf_ref, a_ref.at[0], e_ref, hf_ref, yf_ref, False)
    _ssd_direction(xcb_ref, dtb_ref, a_ref.at[1], e_ref, hb_ref, yb_ref, True)


def _ssd(xc, dt, a_log, *, seq):
    t_total = xc.shape[0]
    bsz = t_total // seq
    nc = seq // SSD_CHUNK
    a_rows = jnp.zeros((2, 1, LANES), F32).at[:, 0, :SSD_HEADS].set(-jnp.exp(a_log))
    expand = (jnp.arange(LANES)[:, None] == jnp.arange(SSD_WIDTH)[None, :] // SSD_HEAD_DIM).astype(BF16)
    fwd = lambda b, c: (b * nc + c, 0)
    bwd = lambda b, c: (b * nc + nc - 1 - c, 0)
    state = pltpu.VMEM((SSD_GROUPS, SSD_STATE, SSD_WIDTH // SSD_GROUPS), F32)
    return pl.pallas_call(
        _ssd_kernel,
        grid=(bsz, nc),
        in_specs=[pl.BlockSpec((SSD_CHUNK, XBC_WIDTH), fwd),
                  pl.BlockSpec((SSD_CHUNK, XBC_WIDTH), bwd),
                  pl.BlockSpec((SSD_CHUNK, LANES), fwd),
                  pl.BlockSpec((SSD_CHUNK, LANES), lambda b, c: (b * nc + nc - 1 - c, 1)),
                  _const_spec((2, 1, LANES)),
                  _const_spec((LANES, SSD_WIDTH))],
        out_specs=[pl.BlockSpec((SSD_CHUNK, SSD_WIDTH), fwd),
                   pl.BlockSpec((SSD_CHUNK, SSD_WIDTH), bwd)],
        out_shape=[jax.ShapeDtypeStruct((t_total, SSD_WIDTH), BF16)] * 2,
        scratch_shapes=[state, state],
        compiler_params=_cparams("parallel", "arbitrary"),
        name="ssd",
    )(xc, xc, dt, dt, a_rows, expand)


_ATTN_QB = 2 * ATTN_HALF
_ATTN_WIN = _ATTN_QB + 2 * ATTN_HALF
_ATTN_LQ = 512


def _attn_kernel(q_ref, k_ref, kp_ref, kn_ref, v_ref, vp_ref, vn_ref, bias_ref, o_ref, lse_ref,
                 kw_ref, vw_ref, *, lq, sub_len):
    t = pl.program_id(2)
    hb = ATTN_HALF
    for src, halo_p, halo_n, win in ((k_ref, kp_ref, kn_ref, kw_ref), (v_ref, vp_ref, vn_ref, vw_ref)):
        win[0:hb, :] = halo_p[...]
        win[hb:hb + lq, :] = src[...]
        win[hb + lq:, :] = halo_n[...]
    qn = _ATTN_QB
    lane = lax.broadcasted_iota(jnp.int32, (qn, LANES), 1)
    first_head = lane < HEAD_DIM
    kpos = lax.broadcasted_iota(jnp.int32, (1, _ATTN_WIN), 1)

    def body(qb, carry):
        r0 = pl.multiple_of(qb * qn, qn)
        kidx = t * lq + r0 - hb + kpos
        in_seq = (kidx >= 0) & (kidx < sub_len)
        lse_tile = jnp.zeros((qn, LANES), F32)
        for hp in range(ATTN_HEADS // 2):
            cs = slice(hp * LANES, (hp + 1) * LANES)
            q = q_ref[pl.ds(r0, qn), cs]
            zero = jnp.zeros_like(q)
            q2 = jnp.concatenate([jnp.where(first_head, q, zero), jnp.where(first_head, zero, q)], axis=0)
            s = lax.dot_general(q2, kw_ref[pl.ds(r0, _ATTN_WIN), cs], (((1,), (1,)), ((), ())),
                                preferred_element_type=F32) + bias_ref[hp]
            s = jnp.where(in_seq, s, _NEG)
            m = jnp.max(s, axis=-1, keepdims=True)
            p = jnp.exp(s - m)
            den = jnp.sum(p, axis=-1, keepdims=True)
            pv = _dot(p.astype(BF16), vw_ref[pl.ds(r0, _ATTN_WIN), cs]) * (1.0 / den)
            o_ref[pl.ds(r0, qn), cs] = jnp.where(first_head, pv[0:qn], pv[qn:]).astype(BF16)
            lse = m + jnp.log(den)
            lse_tile = jnp.where(lane == 2 * hp, lse[0:qn], jnp.where(lane == 2 * hp + 1, lse[qn:], lse_tile))
        lse_ref[pl.ds(r0, qn), :] = lse_tile
        return carry

    lax.fori_loop(0, lq // qn, body, 0)


def _attn_bias(dilation):
    slopes = jnp.exp2(-8.0 * jnp.arange(1, ATTN_HEADS + 1, dtype=F32) / ATTN_HEADS)
    rel = jnp.abs(jnp.arange(_ATTN_WIN)[None, :] - ATTN_HALF - jnp.arange(_ATTN_QB)[:, None])
    dist = (rel * dilation).astype(F32)
    b = jnp.where(rel <= ATTN_HALF, -slopes[:, None, None] * dist, _NEG)
    return b.reshape(ATTN_HEADS // 2, 2 * _ATTN_QB, _ATTN_WIN)


def _dilated_attention(q, k, v, dilation, *, seq):
    t_total = q.shape[0]
    bsz = t_total // seq
    sub_len = seq // dilation
    lq = min(_ATTN_LQ, sub_len)
    nt = sub_len // lq
    hpb = lq // ATTN_HALF
    n_hblk = t_total // ATTN_HALF
    tile = lambda b, r, t: (b * dilation + r) * nt + t
    main = lambda b, r, t: (tile(b, r, t), 0)
    prev = lambda b, r, t: (jnp.maximum(tile(b, r, t) * hpb - 1, 0), 0)
    nxt = lambda b, r, t: (jnp.minimum((tile(b, r, t) + 1) * hpb, n_hblk - 1), 0)
    blk = pl.BlockSpec((lq, ATTN_WIDTH), main)
    hblk_p = pl.BlockSpec((ATTN_HALF, ATTN_WIDTH), prev)
    hblk_n = pl.BlockSpec((ATTN_HALF, ATTN_WIDTH), nxt)
    window = pltpu.VMEM((lq + 2 * ATTN_HALF, ATTN_WIDTH), BF16)
    return pl.pallas_call(
        functools.partial(_attn_kernel, lq=lq, sub_len=sub_len),
        grid=(bsz, dilation, nt),
        in_specs=[blk, blk, hblk_p, hblk_n, blk, hblk_p, hblk_n,
                  _const_spec((ATTN_HEADS // 2, 2 * _ATTN_QB, _ATTN_WIN))],
        out_specs=[blk, pl.BlockSpec((lq, LANES), main)],
        out_shape=[jax.ShapeDtypeStruct((t_total, ATTN_WIDTH), BF16),
                   jax.ShapeDtypeStruct((t_total, LANES), F32)],
        scratch_shapes=[window, window],
        compiler_params=_cparams("parallel", "parallel", "parallel"),
        name=f"attn_d{dilation}",
    )(q, k, k, k, v, v, v, _attn_bias(dilation))


def _mix_kernel(o0_ref, o1_ref, o2_ref, l0_ref, l1_ref, l2_ref, yf_ref, yb_ref, xs_ref, z_ref, x_ref,
                mod_ref, e_ref, dskip_ref, gssd_ref, wout_ref, gpost_ref, gpre_ref, wr_ref, br_ref,
                res_ref, h2_ref, rank_ref, wts_ref, cnt_ref, lnat_ref, onat_ref, *, tm):
    n_slab = ATTN_WIDTH // LANES

    dilated = ((DILATIONS[1], l1_ref, o1_ref), (DILATIONS[2], l2_ref, o2_ref))
    for gi, (dil, l_ref, _) in enumerate(dilated):
        for rr in range(dil):
            lnat_ref[gi, pl.ds(rr, tm // dil, stride=dil), :] = l_ref[0, rr]

    l0, l1, l2 = l0_ref[...], lnat_ref[0], lnat_ref[1]
    m = jnp.maximum(jnp.maximum(l0, l1), l2)
    es = [jnp.exp(l - m) for l in (l0, l1, l2)]
    inv = 1.0 / (es[0] + es[1] + es[2])
    expand = e_ref[...]

    def widen(w):
        wh, wl = _split2(w)
        return _dot(wh, expand) + _dot(wl, expand)

    attn = widen(es[0] * inv) * o0_ref[...].astype(F32)
    for gi, (dil, _, o_ref) in enumerate(dilated):
        for rr in range(dil):
            for s in range(n_slab):
                onat_ref[s, pl.ds(rr, tm // dil, stride=dil), :] = (
                    o_ref[0, rr, :, s * LANES:(s + 1) * LANES].astype(F32))
        o_nat = jnp.concatenate([onat_ref[s] for s in range(n_slab)], axis=1)
        attn = attn + widen(es[gi + 1] * inv) * o_nat

    xs = xs_ref[...].astype(F32)
    z = z_ref[...].astype(F32)
    y = yf_ref[...].astype(F32) + yb_ref[...].astype(F32) + dskip_ref[...] * xs
    y = _rms(y * (z * _sigmoid(z))) * gssd_ref[...]
    mix = _dot(attn.astype(BF16), wout_ref[0:ATTN_WIDTH, :]) + _dot(y.astype(BF16), wout_ref[ATTN_WIDTH:, :])
    gate_m = mod_ref[0, 2:3, :]
    shift_f = mod_ref[0, 3:4, :]
    scale_f = mod_ref[0, 4:5, :]
    res = x_ref[...] + gate_m * (_rms(mix) * gpost_ref[...])
    res_ref[...] = res
    h2 = _rms(res) * gpre_ref[...] * (1.0 + scale_f) + shift_f
    h2_ref[...] = h2.astype(BF16)

    vals = _dot_f32(h2, wr_ref[...]) + br_ref[...]
    lane = lax.broadcasted_iota(jnp.int32, (tm, LANES), 1)
    sels, tops = [], []
    for _ in range(TOP_K):
        mx = jnp.max(vals, axis=-1, keepdims=True)
        idx = jnp.min(jnp.where(vals == mx, lane, LANES), axis=-1, keepdims=True)
        sel = lane == idx
        sels.append(sel)
        tops.append((mx, idx))
        vals = jnp.where(sel, -jnp.inf, vals)
    ex = [jnp.exp(tv - tops[0][0]) for tv, _ in tops]
    den = ex[0] + ex[1] + ex[2] + ex[3]
    hit = jnp.zeros((tm, LANES), F32)
    wts = jnp.zeros((tm, LANES), F32)
    for kk, sel in enumerate(sels):
        hit = jnp.where(sel, 1.0, hit)
        wts = jnp.where(sel, ex[kk] / den, wts)
    ri = lax.broadcasted_iota(jnp.int32, (tm, tm), 0)
    ci = lax.broadcasted_iota(jnp.int32, (tm, tm), 1)
    before = jnp.where(ci < ri, 1.0, 0.0).astype(BF16)
    rank = _dot(before, hit.astype(BF16))
    rank_ref[...] = jnp.where(hit > 0.0, rank, -1.0)
    wts_ref[...] = wts
    cnt = jnp.sum(hit, axis=0, keepdims=True)
    cnt_ref[0] = jnp.broadcast_to(cnt, (SUBLANES, LANES)).astype(jnp.int32)


def _mix_and_route(outs, lses, yf, yb, xc, z, x2, mod, d_skip, g_ssd, w_out, g_post, g_pre_ffn,
                   w_router, b_router, *, seq, tm=256):
    t_total = x2.shape[0]
    nt = t_total // tm
    tiles_per_seq = seq // tm
    expand = (jnp.arange(LANES)[:, None] == jnp.arange(ATTN_WIDTH)[None, :] // HEAD_DIM).astype(BF16)
    wr = jnp.zeros((D_MODEL, LANES), F32).at[:, :N_EXPERTS].set(w_router)
    br = jnp.full((1, LANES), _NEG, F32).at[0, :N_EXPERTS].set(b_router)
    row = lambda i: (i, 0)
    wide = pl.BlockSpec((tm, D_MODEL), row)
    narrow = pl.BlockSpec((tm, LANES), row)
    vec = _const_spec((1, D_MODEL))
    bsz = t_total // seq

    def dilated(a, dil):
        a = a.reshape(bsz, dil, seq // dil, a.shape[-1])
        spec = pl.BlockSpec((1, dil, tm // dil, a.shape[-1]),
                            lambda i: (i // tiles_per_seq, 0, i % tiles_per_seq, 0))
        return a, spec

    o_args, o_specs = [outs[0]], [wide]
    l_args, l_specs = [lses[0]], [narrow]
    for dil, o, l in zip(DILATIONS[1:], outs[1:], lses[1:]):
        a, spec = dilated(o, dil)
        o_args.append(a)
        o_specs.append(spec)
        a, spec = dilated(l, dil)
        l_args.append(a)
        l_specs.append(spec)
    outs, lses = o_args, l_args
    return pl.pallas_call(
        functools.partial(_mix_kernel, tm=tm),
        grid=(nt,),
        in_specs=o_specs + l_specs + [wide] * 5 + [
            pl.BlockSpec((1, N_MOD, D_MODEL), lambda i: (i // tiles_per_seq, 0, 0)),
            _const_spec((LANES, ATTN_WIDTH)), vec, vec,
            _const_spec((ATTN_WIDTH + SSD_WIDTH, D_MODEL)), vec, vec,
            _const_spec((D_MODEL, LANES)), _const_spec((1, LANES))],
        out_specs=[wide, wide, narrow, narrow, pl.BlockSpec((1, SUBLANES, LANES), lambda i: (i, 0, 0))],
        out_shape=[jax.ShapeDtypeStruct((t_total, D_MODEL), F32),
                   jax.ShapeDtypeStruct((t_total, D_MODEL), BF16),
                   jax.ShapeDtypeStruct((t_total, LANES), F32),
                   jax.ShapeDtypeStruct((t_total, LANES), F32),
                   jax.ShapeDtypeStruct((nt, SUBLANES, LANES), jnp.int32)],
        scratch_shapes=[pltpu.VMEM((len(DILATIONS) - 1, tm, LANES), F32),
                        pltpu.VMEM((ATTN_WIDTH // LANES, tm, LANES), F32)],
        compiler_params=_cparams("parallel"),
        name="mix_route",
    )(*outs, *lses, yf, yb, xc, z, x2, mod, expand, jnp.repeat(d_skip, SSD_HEAD_DIM).reshape(1, -1),
      g_ssd.reshape(1, -1), w_out.astype(BF16), g_post.reshape(1, -1), g_pre_ffn.reshape(1, -1), wr, br)


_MOE_TM = 256
_ROUTE_TM = 256
_MOE_CAP = 48
_BF16_ROWS = 2 * SUBLANES
_MOE_WIN = _MOE_CAP + _BF16_ROWS
_MOE_TAIL = _MOE_TM
_MOE_GROUP = 8
_ROW_SPLIT = D_MODEL // LANES


def _store_rows(ref, lead, row0, vals):
    n = vals.shape[0]
    for q in range(_ROW_SPLIT):
        ref[lead, pl.ds(row0 * _ROW_SPLIT + q, n, stride=_ROW_SPLIT), :] = vals[:, q * LANES:(q + 1) * LANES]


def _load_rows(ref, lead, row0, n):
    parts = [ref[lead, pl.ds(row0 * _ROW_SPLIT + q, n, stride=_ROW_SPLIT), :] for q in range(_ROW_SPLIT)]
    return jnp.concatenate(parts, axis=1)


def _sorted_rows(t_total):
    return t_total * TOP_K + N_EXPERTS * _MOE_CAP + _MOE_TAIL


def _dispatch_kernel(seg_ref, n_ref, nch_ref, rank_ref, h_ref, xs_ref, stage_ref, sem, *, nt):
    i = pl.program_id(0)
    slot = i % 2
    cap = _MOE_CAP
    rank = jnp.where(i < nt, rank_ref[...], -1.0)
    rank_t = rank.T
    jrow = lax.broadcasted_iota(jnp.int32, (cap, _ROUTE_TM), 0).astype(F32)
    hb = h_ref[...]

    def compact(c, dst_slot):
        target = jrow + jnp.asarray(c * cap, F32)
        for g in range(N_EXPERTS // _MOE_GROUP):
            sel = [jnp.where(rank_t[e:e + 1, :] == target, 1.0, 0.0).astype(BF16)
                   for e in range(g * _MOE_GROUP, (g + 1) * _MOE_GROUP)]
            rows = _dot(jnp.concatenate(sel, axis=0), hb)
            _store_rows(stage_ref, dst_slot, g * _MOE_GROUP * cap, rows)

    def seg_copy(step, e, c, s):
        dst = pl.multiple_of((seg_ref[step * N_EXPERTS + e] + c * cap) * _ROW_SPLIT, _ROW_SPLIT)
        return pltpu.make_async_copy(stage_ref.at[s, pl.ds(e * cap * _ROW_SPLIT, cap * _ROW_SPLIT)],
                                     xs_ref.at[pl.ds(dst, cap * _ROW_SPLIT)], sem.at[s])

    compact(0, slot)

    @pl.when(i > 0)
    def _():
        for e in range(N_EXPERTS):
            seg_copy(i - 1, e, 0, 1 - slot).wait()

    for e in range(N_EXPERTS):
        seg_copy(i, e, 0, slot).start()

    def overflow_chunk(c, carry):
        compact(c, 1 - slot)
        for e in range(N_EXPERTS):
            @pl.when(n_ref[i * N_EXPERTS + e] > c * cap)
            def _():
                seg_copy(i, e, c, 1 - slot).start()
        for e in range(N_EXPERTS):
            @pl.when(n_ref[i * N_EXPERTS + e] > c * cap)
            def _():
                seg_copy(i, e, c, 1 - slot).wait()
        return carry

    lax.fori_loop(1, nch_ref[i], overflow_chunk, 0)

    @pl.when(i == nt)
    def _():
        for e in range(N_EXPERTS):
            seg_copy(i, e, 0, slot).wait()
        n_rows = xs_ref.shape[0] // _ROW_SPLIT
        zeros = stage_ref.at[1 - slot, pl.ds(0, cap * _ROW_SPLIT)]
        zeros[...] = jnp.zeros((cap * _ROW_SPLIT, LANES), F32)
        for j in range(-(-_MOE_TAIL // cap)):
            row = min(n_rows - _MOE_TAIL + j * cap, n_rows - cap)
            tail = pltpu.make_async_copy(zeros, xs_ref.at[pl.ds(row * _ROW_SPLIT, cap * _ROW_SPLIT)],
                                         sem.at[1 - slot])
            tail.start()
            tail.wait()


def _dispatch(h2, rank, seg, n_seg, n_chunks):
    t_total = h2.shape[0]
    nt = t_total // _ROUTE_TM
    n_rows = _sorted_rows(t_total)
    tile = lambda i, *_: (jnp.minimum(i, nt - 1), 0)
    grid_spec = pltpu.PrefetchScalarGridSpec(
        num_scalar_prefetch=3,
        grid=(nt + 1,),
        in_specs=[pl.BlockSpec((_ROUTE_TM, LANES), tile),
                  pl.BlockSpec((_ROUTE_TM, D_MODEL), tile)],
        out_specs=pl.BlockSpec(memory_space=pl.ANY),
        scratch_shapes=[pltpu.VMEM((2, N_EXPERTS * _MOE_CAP * _ROW_SPLIT, LANES), F32),
                        pltpu.SemaphoreType.DMA((2,))])
    return pl.pallas_call(
        functools.partial(_dispatch_kernel, nt=nt),
        grid_spec=grid_spec,
        out_shape=jax.ShapeDtypeStruct((n_rows * _ROW_SPLIT, LANES), F32),
        compiler_params=_cparams("arbitrary"),
        name="moe_dispatch",
    )(seg, n_seg, n_chunks, rank, h2)


def _gmm_kernel(tile_ref, exp_ref, valid_ref, offs_ref, x_ref, wgu_ref, bgu_ref, wdn_ref, bdn_ref, y_ref,
                wgu_bf, wdn_bf):
    j = pl.program_id(0)
    e = exp_ref[j]
    tile = tile_ref[j]
    prev_j = jnp.maximum(j - 1, 0)
    new_expert = (j == 0) | (exp_ref[prev_j] != e)
    new_tile = (j == 0) | (tile_ref[prev_j] != tile)

    @pl.when(new_expert)
    def _():
        wgu_bf[...] = wgu_ref[0].astype(BF16)
        wdn_bf[...] = wdn_ref[0].astype(BF16)

    @pl.when(valid_ref[j] != 0)
    def _():
        half = D_MODEL // 2
        gu = bgu_ref[0]
        for kc in range(2):
            parts = [x_ref[0, pl.ds(q, _MOE_TM, stride=_ROW_SPLIT), :]
                     for q in range(kc * _ROW_SPLIT // 2, (kc + 1) * _ROW_SPLIT // 2)]
            xk = jnp.concatenate(parts, axis=1).astype(BF16)
            gu = gu + _dot(xk, wgu_bf[kc * half:(kc + 1) * half, :])
        gate = jnp.minimum(gu[:, :EXPERT_FF], SWIGLU_LIMIT)
        up = jnp.clip(gu[:, EXPERT_FF:], -SWIGLU_LIMIT, SWIGLU_LIMIT)
        act = (up + 1.0) * gate * _sigmoid(SWIGLU_ALPHA * gate)
        y = (_dot(act.astype(BF16), wdn_bf[...]) + bdn_ref[0]).astype(BF16)
        rows = tile * _MOE_TM + lax.broadcasted_iota(jnp.int32, (_MOE_TM, 1), 0)
        mine = (rows >= offs_ref[e]) & (rows < offs_ref[e + 1])

        @pl.when(new_tile)
        def _():
            y_ref[...] = jnp.where(mine, y, jnp.zeros_like(y))

        @pl.when(jnp.logical_not(new_tile))
        def _():
            y_ref[...] = jnp.where(mine, y, y_ref[...])


def _grouped_mlp(xs, offs, w_gate_up, b_gate_up, w_down, b_down):
    n_tiles = xs.shape[0] // (_MOE_TM * _ROW_SPLIT)
    n_visits = n_tiles + N_EXPERTS - 1
    first = offs[:-1] // _MOE_TM
    last = (offs[1:] - 1) // _MOE_TM
    per_expert = last - first + 1
    vstart = jnp.concatenate([jnp.zeros((1,), jnp.int32), jnp.cumsum(per_expert)]).astype(jnp.int32)
    total = vstart[-1]
    vis = jnp.minimum(jnp.arange(n_visits, dtype=jnp.int32), total - 1)
    exp_ids = (jnp.sum(vstart[None, :] <= vis[:, None], axis=1) - 1).astype(jnp.int32)
    tile_ids = (first[exp_ids] + vis - vstart[exp_ids]).astype(jnp.int32)
    valid = (jnp.arange(n_visits) < total).astype(jnp.int32)
    grid_spec = pltpu.PrefetchScalarGridSpec(
        num_scalar_prefetch=4,
        grid=(n_visits,),
        in_specs=[pl.BlockSpec((1, _MOE_TM * _ROW_SPLIT, LANES), lambda j, t, e, v, o: (t[j], 0, 0)),
                  pl.BlockSpec((1, D_MODEL, 2 * EXPERT_FF), lambda j, t, e, v, o: (e[j], 0, 0)),
                  pl.BlockSpec((1, 1, 2 * EXPERT_FF), lambda j, t, e, v, o: (e[j], 0, 0)),
                  pl.BlockSpec((1, EXPERT_FF, D_MODEL), lambda j, t, e, v, o: (e[j], 0, 0)),
                  pl.BlockSpec((1, 1, D_MODEL), lambda j, t, e, v, o: (e[j], 0, 0))],
        out_specs=pl.BlockSpec((_MOE_TM, D_MODEL), lambda j, t, e, v, o: (t[j], 0)),
        scratch_shapes=[pltpu.VMEM((D_MODEL, 2 * EXPERT_FF), BF16), pltpu.VMEM((EXPERT_FF, D_MODEL), BF16)])
    tiled = (n_tiles, _MOE_TM * _ROW_SPLIT, LANES)
    return pl.pallas_call(
        _gmm_kernel,
        grid_spec=grid_spec,
        out_shape=jax.ShapeDtypeStruct((n_tiles * _MOE_TM, D_MODEL), BF16),
        compiler_params=_cparams("arbitrary"),
        name="moe_mlp",
    )(tile_ids, exp_ids, valid, offs, xs.reshape(tiled), w_gate_up, b_gate_up.reshape(N_EXPERTS, 1, -1),
      w_down, b_down.reshape(N_EXPERTS, 1, -1))


def _combine_kernel(seg_ref, n_ref, nch_ref, ys_ref, rank_ref, wts_ref, shift_ref, e_ref, res_ref, mod_ref,
                    g_ref, o_ref, ybuf_ref, sem, *, nt):
    i = pl.program_id(0)
    slot = i % 2
    cap = _MOE_CAP
    win = _MOE_WIN
    gcols = _MOE_GROUP * win

    def seg_copy(step, e, c, s):
        src = seg_ref[step * N_EXPERTS + e] + c * cap
        src = pl.multiple_of((src // _BF16_ROWS) * _BF16_ROWS, _BF16_ROWS)
        return pltpu.make_async_copy(ys_ref.at[pl.ds(src, win)], ybuf_ref.at[s, pl.ds(e * win, win)], sem.at[s])

    @pl.when(i == 0)
    def _():
        for e in range(N_EXPERTS):
            seg_copy(0, e, 0, 0).start()

    @pl.when(i + 1 < nt)
    def _():
        for e in range(N_EXPERTS):
            seg_copy(i + 1, e, 0, 1 - slot).start()

    for e in range(N_EXPERTS):
        seg_copy(i, e, 0, slot).wait()

    tm = _ROUTE_TM
    rank = rank_ref[...]
    shift = shift_ref[0, 0:1, :]
    wh, wl = _split2(wts_ref[...])
    jcol = (lax.broadcasted_iota(jnp.int32, (tm, gcols), 1) % win).astype(F32)

    def accumulate(c, ffn):
        rel = rank - jnp.asarray(c * cap, F32)
        col = jnp.where((rel >= 0.0) & (rel < cap), rel + shift, -1.0)
        stacked = jnp.concatenate([col.astype(BF16), wh, wl], axis=0)
        for g in range(N_EXPERTS // _MOE_GROUP):
            wide = _dot(stacked, e_ref[:, g * gcols:(g + 1) * gcols])
            hit = wide[0:tm] == jcol
            p2 = jnp.concatenate([jnp.where(hit, wide[tm:2 * tm], 0.0).astype(BF16),
                                  jnp.where(hit, wide[2 * tm:], 0.0).astype(BF16)], axis=0)
            part = _dot(p2, ybuf_ref[slot, g * gcols:(g + 1) * gcols, :])
            ffn = ffn + (part[0:tm] + part[tm:])
        return ffn

    ffn = accumulate(0, jnp.zeros((_ROUTE_TM, D_MODEL), F32))

    def overflow_chunk(c, ffn):
        for e in range(N_EXPERTS):
            @pl.when(n_ref[i * N_EXPERTS + e] > c * cap)
            def _():
                seg_copy(i, e, c, slot).start()
        for e in range(N_EXPERTS):
            @pl.when(n_ref[i * N_EXPERTS + e] > c * cap)
            def _():
                seg_copy(i, e, c, slot).wait()
        return accumulate(c, ffn)

    ffn = lax.fori_loop(1, nch_ref[i], overflow_chunk, ffn)
    gate_f = mod_ref[0, 5:6, :]
    o_ref[...] = res_ref[...] + gate_f * (_rms(ffn) * g_ref[...])


def _combine(ys, rank, wts, seg, n_seg, n_chunks, res, mod, g_post_ffn, *, seq):
    t_total = res.shape[0]
    nt = t_total // _ROUTE_TM
    tiles_per_seq = seq // _ROUTE_TM
    row = lambda i, *_: (i, 0)
    n_cols = N_EXPERTS * _MOE_WIN
    widen = (jnp.arange(LANES)[:, None] == jnp.arange(n_cols)[None, :] // _MOE_WIN).astype(BF16)
    shift = (seg.reshape(nt + 1, N_EXPERTS)[:nt] % _BF16_ROWS).astype(F32)
    shift = jnp.zeros((nt, SUBLANES, LANES), F32).at[:, :, :N_EXPERTS].set(shift[:, None, :])
    grid_spec = pltpu.PrefetchScalarGridSpec(
        num_scalar_prefetch=3,
        grid=(nt,),
        in_specs=[pl.BlockSpec(memory_space=pl.ANY),
                  pl.BlockSpec((_ROUTE_TM, LANES), row),
                  pl.BlockSpec((_ROUTE_TM, LANES), row),
                  pl.BlockSpec((1, SUBLANES, LANES), lambda i, *_: (i, 0, 0)),
                  pl.BlockSpec((LANES, n_cols), lambda i, *_: (0, 0)),
                  pl.BlockSpec((_ROUTE_TM, D_MODEL), row),
                  pl.BlockSpec((1, N_MOD, D_MODEL), lambda i, *_: (i // tiles_per_seq, 0, 0)),
                  pl.BlockSpec((1, D_MODEL), lambda i, *_: (0, 0))],
        out_specs=pl.BlockSpec((_ROUTE_TM, D_MODEL), row),
        scratch_shapes=[pltpu.VMEM((2, n_cols, D_MODEL), BF16),
                        pltpu.SemaphoreType.DMA((2,))])
    return pl.pallas_call(
        functools.partial(_combine_kernel, nt=nt),
        grid_spec=grid_spec,
        out_shape=jax.ShapeDtypeStruct((t_total, D_MODEL), F32),
        compiler_params=_cparams("arbitrary"),
        name="moe_combine",
    )(seg, n_seg, n_chunks, ys, rank, wts, shift, widen, res, mod, g_post_ffn.reshape(1, -1))


def _forward(x, c, w_ada, b_ada, g_pre_mix, g_post_mix, w_in, conv_w, conv_b, dt_bias, a_log, d_skip,
             g_ssd_norm, w_out, g_pre_ffn, g_post_ffn, w_router, b_router, w_gate_up, b_gate_up, w_down, b_down):
    bsz, seq, _ = x.shape
    x2 = x.astype(F32).reshape(bsz * seq, D_MODEL)
    mod = _ada_mod(c.astype(F32), w_ada, b_ada)
    *qkv, z, xc, dt = _in_proj(x2, mod, g_pre_mix, w_in, conv_w, conv_b, dt_bias, seq=seq)
    attn = []
    for gi, dil in enumerate(DILATIONS):
        q, k, v = (a.reshape(bsz * seq, ATTN_WIDTH) for a in qkv[3 * gi:3 * gi + 3])
        attn.append(_dilated_attention(q, k, v, dil, seq=seq))
    yf, yb = _ssd(xc, dt, a_log, seq=seq)
    res, h2, rank, wts, cnt = _mix_and_route(
        [o for o, _ in attn], [l for _, l in attn], yf, yb, xc, z, x2, mod, d_skip, g_ssd_norm, w_out,
        g_post_mix, g_pre_ffn, w_router, b_router, seq=seq, tm=_ROUTE_TM)
    n_te = cnt[:, 0, :N_EXPERTS]
    zero_row = jnp.zeros((1, N_EXPERTS), jnp.int32)
    region = jnp.sum(n_te, axis=0) + _MOE_CAP
    region = region.at[N_EXPERTS - 1].add(_MOE_TAIL)
    offs =jnp.concatenate([jnp.zeros((1,), jnp.int32), jnp.cumsum(region)]).astype(jnp.int32)
    before = jnp.concatenate([zero_row, jnp.cumsum(n_te, axis=0)], axis=0)
    seg = (offs[None, :N_EXPERTS] + before).astype(jnp.int32).reshape(-1)
    n_seg = jnp.concatenate([n_te, zero_row], axis=0)
    n_chunks = jnp.maximum((jnp.max(n_seg, axis=1) + _MOE_CAP - 1) // _MOE_CAP, 1).astype(jnp.int32)
    n_seg = n_seg.reshape(-1)
    xs = _dispatch(h2, rank, seg, n_seg, n_chunks)
    ys = _grouped_mlp(xs, offs, w_gate_up, b_gate_up, w_down, b_down)
    out = _combine(ys, rank, wts, seg, n_seg, n_chunks, res, mod, g_post_ffn, seq=seq)
    return out.reshape(bsz, seq, D_MODEL)


def kernel(x, c, w_ada, b_ada, g_pre_mix, g_post_mix, w_in, conv_w, conv_b, dt_bias, a_log, d_skip, g_ssd_norm, w_out, g_pre_ffn, g_post_ffn, w_router, b_router, w_gate_up, b_gate_up, w_down, b_down):
    layer = lambda t: t[0].astype(F32)
    out = _forward(x, c, layer(w_ada), layer(b_ada), layer(g_pre_mix), layer(g_post_mix), layer(w_in),
                   layer(conv_w), layer(conv_b), layer(dt_bias), layer(a_log), layer(d_skip),
                   layer(g_ssd_norm), layer(w_out), layer(g_pre_ffn), layer(g_post_ffn), layer(w_router),
                   layer(b_router), layer(w_gate_up), layer(b_gate_up), layer(w_down), layer(b_down))
    return out.astype(x.dtype)
```

```python
import functools

import jax
import jax.numpy as jnp
from jax import lax
from jax.experimental import pallas as pl
from jax.experimental.pallas import tpu as pltpu

F32 = jnp.float32
BF16 = jnp.bfloat16

D_MODEL = 1024
ATTN_HEADS = 16
HEAD_DIM = 64
ATTN_WIDTH = ATTN_HEADS * HEAD_DIM
DILATIONS = (1, 4, 16)
ATTN_HALF = 64
SSD_HEADS = 16
SSD_HEAD_DIM = 64
SSD_WIDTH = SSD_HEADS * SSD_HEAD_DIM
SSD_GROUPS = 2
SSD_STATE = 128
SSD_CHUNK = 128
CONV_WIDTH = 5
XBC_WIDTH = SSD_WIDTH + 2 * SSD_GROUPS * SSD_STATE
N_EXPERTS = 32
TOP_K = 4
EXPERT_FF = 1024
SWIGLU_ALPHA = 1.702
SWIGLU_LIMIT = 7.0
RMS_EPS = 1e-6
N_MOD = 6
LANES = 128
SUBLANES = 8
VMEM_LIMIT = 56 * 1024 * 1024


def _cparams(*sem):
    return pltpu.CompilerParams(dimension_semantics=sem, vmem_limit_bytes=VMEM_LIMIT)


def _const_spec(shape):
    nd = len(shape)
    return pl.BlockSpec(shape, lambda *_: (0,) * nd)


def _split2(a):
    hi = a.astype(BF16)
    lo = (a - hi.astype(F32)).astype(BF16)
    return hi, lo


def _dot(a, b):
    return jnp.dot(a, b, preferred_element_type=F32)


def _dot_f32(a, b):
    ah, al = _split2(a)
    bh, bl = _split2(b)
    return _dot(ah, bh) + (_dot(ah, bl) + _dot(al, bh))


def _sigmoid(x):
    return 1.0 / (1.0 + jnp.exp(-x))


def _rms(x):
    return x * lax.rsqrt(jnp.mean(x * x, axis=-1, keepdims=True) + RMS_EPS)


def _ada_kernel(c_ref, w_ref, b_ref, o_ref):
    c = c_ref[...]
    o_ref[...] = _dot_f32(c * _sigmoid(c), w_ref[...]) + b_ref[...]


def _ada_mod(c, w_ada, b_ada):
    bsz = c.shape[0]
    c8 = jnp.zeros((SUBLANES, D_MODEL), F32).at[:bsz].set(c)
    out = pl.pallas_call(
        _ada_kernel,
        grid=(N_MOD,),
        in_specs=[_const_spec((SUBLANES, D_MODEL)),
                  pl.BlockSpec((D_MODEL, D_MODEL), lambda j: (0, j)),
                  pl.BlockSpec((1, D_MODEL), lambda j: (0, j))],
        out_specs=pl.BlockSpec((SUBLANES, D_MODEL), lambda j: (0, j)),
        out_shape=jax.ShapeDtypeStruct((SUBLANES, N_MOD * D_MODEL), F32),
        compiler_params=_cparams("parallel"),
        name="ada_mod",
    )(c8, w_ada, b_ada.reshape(1, -1))
    return out[:bsz].reshape(bsz, N_MOD, D_MODEL)


def _inproj_kernel(x_ref, xp_ref, xn_ref, mod_ref, g_ref, wqkv_ref, wz_ref, wxbc_ref, wdt_ref,
                   cw_ref, cb_ref, dtb_ref,
                   q1_ref, k1_ref, v1_ref, q4_ref, k4_ref, v4_ref, q16_ref, k16_ref, v16_ref,
                   z_ref, xc_ref, dt_ref, buf_ref, slab_ref, *, tm, tiles_per_seq):
    i = pl.program_id(0)
    shift = mod_ref[0, 0:1, :]
    scale = mod_ref[0, 1:2, :]
    g = g_ref[...]

    def norm_mod(x):
        return _rms(x) * g * (1.0 + scale) + shift

    h = norm_mod(x_ref[...])
    hb = h.astype(BF16)
    n_slab = ATTN_WIDTH // LANES
    outs = ((q1_ref, q4_ref, q16_ref), (k1_ref, k4_ref, k16_ref), (v1_ref, v4_ref, v16_ref))
    for c, (nat_ref, d4_ref, d16_ref) in enumerate(outs):
        r = _dot(hb, wqkv_ref[:, c * ATTN_WIDTH:(c + 1) * ATTN_WIDTH])
        if c == 0:
            r = r * HEAD_DIM ** -0.5
        nat_ref[...] = r.astype(BF16)
        for s in range(n_slab):
            slab_ref[s] = r[:, s * LANES:(s + 1) * LANES]
        for dil, d_ref in ((DILATIONS[1], d4_ref), (DILATIONS[2], d16_ref)):
            for rr in range(dil):
                for s in range(n_slab):
                    d_ref[0, rr, :, s * LANES:(s + 1) * LANES] = (
                        slab_ref[s, pl.ds(rr, tm // dil, stride=dil), :].astype(BF16))
    z_ref[...] = _dot(hb, wz_ref[...]).astype(BF16)
    dt_ref[...] = jax.nn.softplus(_dot_f32(h, wdt_ref[...]) + dtb_ref[...])

    hh = norm_mod(jnp.concatenate([xp_ref[...], xn_ref[...]], axis=0)).astype(BF16)
    halo = _dot(hh, wxbc_ref[...])
    t_in_seq = i % tiles_per_seq
    prev_ok = (t_in_seq != 0).astype(F32)
    next_ok = (t_in_seq != tiles_per_seq - 1).astype(F32)
    buf_ref[0:SUBLANES, :] = halo[0:SUBLANES] * prev_ok
    buf_ref[SUBLANES + tm:, :] = halo[SUBLANES:] * next_ok
    buf_ref[SUBLANES:SUBLANES + tm, :] = _dot(hb, wxbc_ref[...])
    acc = cb_ref[...]
    for j in range(CONV_WIDTH):
        off = SUBLANES - CONV_WIDTH // 2 + j
        acc = acc + buf_ref[off:off + tm, :] * cw_ref[j:j + 1, :]
    xc_ref[...] = (acc * _sigmoid(acc)).astype(BF16)


def _in_proj(x2, mod, g_pre, w_in, conv_w, conv_b, dt_bias, *, seq, tm=512):
    t_total = x2.shape[0]
    nt = t_total // tm
    tiles_per_seq = seq // tm
    hb = tm // SUBLANES
    n_hblk = t_total // SUBLANES
    qkv_w = 3 * ATTN_WIDTH
    w_qkv = w_in[:, :qkv_w].astype(BF16)
    w_z = w_in[:, qkv_w:qkv_w + SSD_WIDTH].astype(BF16)
    w_xbc = w_in[:, qkv_w + SSD_WIDTH:qkv_w + SSD_WIDTH + XBC_WIDTH].astype(BF16)
    w_dt_raw = w_in[:, qkv_w + SSD_WIDTH + XBC_WIDTH:]
    w_dt = jnp.zeros((D_MODEL, 2 * LANES), F32)
    dtb = jnp.zeros((1, 2 * LANES), F32)
    for dr in range(2):
        w_dt = w_dt.at[:, dr * LANES:dr * LANES + SSD_HEADS].set(w_dt_raw[:, dr * SSD_HEADS:(dr + 1) * SSD_HEADS])
        dtb = dtb.at[0, dr * LANES:dr * LANES + SSD_HEADS].set(dt_bias[dr])
    row = lambda i: (i, 0)
    bsz = t_total // seq
    dil_shapes, dil_specs = [], []
    for dil in DILATIONS[1:]:
        dil_shapes += [jax.ShapeDtypeStruct((bsz, dil, seq // dil, ATTN_WIDTH), BF16)] * 3
        dil_specs += [pl.BlockSpec((1, dil, tm // dil, ATTN_WIDTH),
                                   lambda i: (i // tiles_per_seq, 0, i % tiles_per_seq, 0))] * 3
    out_shape = [jax.ShapeDtypeStruct((t_total, ATTN_WIDTH), BF16)] * 3 + dil_shapes + [
        jax.ShapeDtypeStruct((t_total, SSD_WIDTH), BF16),
        jax.ShapeDtypeStruct((t_total, XBC_WIDTH), BF16),
        jax.ShapeDtypeStruct((t_total, 2 * LANES), F32)]
    return pl.pallas_call(
        functools.partial(_inproj_kernel, tm=tm, tiles_per_seq=tiles_per_seq),
        grid=(nt,),
        in_specs=[pl.BlockSpec((tm, D_MODEL), row),
                  pl.BlockSpec((SUBLANES, D_MODEL), lambda i: (jnp.maximum(i * hb - 1, 0), 0)),
                  pl.BlockSpec((SUBLANES, D_MODEL), lambda i: (jnp.minimum((i + 1) * hb, n_hblk - 1), 0)),
                  pl.BlockSpec((1, N_MOD, D_MODEL), lambda i: (i // tiles_per_seq, 0, 0)),
                  _const_spec((1, D_MODEL)),
                  _const_spec((D_MODEL, qkv_w)),
                  _const_spec((D_MODEL, SSD_WIDTH)),
                  _const_spec((D_MODEL, XBC_WIDTH)),
                  _const_spec((D_MODEL, 2 * LANES)),
                  _const_spec((CONV_WIDTH, XBC_WIDTH)),
                  _const_spec((1, XBC_WIDTH)),
                  _const_spec((1, 2 * LANES))],
        out_specs=[pl.BlockSpec((tm, ATTN_WIDTH), row)] * 3 + dil_specs + [
            pl.BlockSpec((tm, SSD_WIDTH), row),
            pl.BlockSpec((tm, XBC_WIDTH), row),
            pl.BlockSpec((tm, 2 * LANES), row)],
        out_shape=out_shape,
        scratch_shapes=[pltpu.VMEM((tm + 2 * SUBLANES, XBC_WIDTH), F32),
                        pltpu.VMEM((ATTN_WIDTH // LANES, tm, LANES), F32)],
        compiler_params=_cparams("parallel"),
        name="in_proj",
    )(x2, x2, x2, mod, g_pre.reshape(1, -1), w_qkv, w_z, w_xbc, w_dt, conv_w, conv_b.reshape(1, -1), dtb)


_NEG = -1e30


def _ssd_direction(xc_ref, dt_ref, a_ref, e_ref, h_ref, y_ref, reverse):
    qn = SSD_CHUNK
    gw = SSD_WIDTH // SSD_GROUPS
    li = lax.broadcasted_iota(jnp.int32, (qn, qn), 0)
    si = lax.broadcasted_iota(jnp.int32, (qn, qn), 1)
    mask = (si >= li) if reverse else (si <= li)
    tri = jnp.where(mask, 1.0, 0.0).astype(BF16)
    dt = dt_ref[...]
    adt = dt * a_ref[...]
    p0 = adt.astype(BF16)
    r0 = adt - p0.astype(F32)
    p1 = r0.astype(BF16)
    p2 = (r0 - p1.astype(F32)).astype(BF16)
    acum = _dot(tri, p0) + (_dot(tri, p1) + _dot(tri, p2))
    last = 0 if reverse else qn - 1
    eo = jnp.exp(acum)
    ds = jnp.exp(acum[last:last + 1, :] - acum)
    sh, sl = _split2(jnp.concatenate([dt, ds, eo], axis=0))
    ex = _dot(sh, e_ref[...]) + _dot(sl, e_ref[...])
    dt_x, ds_x, eo_x = ex[0:qn], ex[qn:2 * qn], ex[2 * qn:3 * qn]
    xd = xc_ref[:, 0:SSD_WIDTH].astype(F32) * dt_x
    xdb = xd.astype(BF16)
    xwb = (xd * ds_x).astype(BF16)
    acum_t = acum.T
    lane = lax.broadcasted_iota(jnp.int32, (qn, LANES), 1)
    for g in range(SSD_GROUPS):
        b0 = SSD_WIDTH + g * SSD_STATE
        c0 = SSD_WIDTH + SSD_GROUPS * SSD_STATE + g * SSD_STATE
        bm = xc_ref[:, b0:b0 + SSD_STATE]
        cm = xc_ref[:, c0:c0 + SSD_STATE]
        cb = lax.dot_general(cm, bm, (((1,), (1,)), ((), ())), preferred_element_type=F32)
        hg = h_ref[g]
        yoff = _dot(cm, hg.astype(BF16))
        st = lax.dot_general(bm, xwb[:, g * gw:(g + 1) * gw], (((0,), (0,)), ((), ())),
                             preferred_element_type=F32)
        h_ref[g] = hg * eo_x[last:last + 1, g * gw:(g + 1) * gw] + st
        for pr in range(gw // LANES):
            col = g * gw + pr * LANES
            xp = xdb[:, col:col + LANES]
            res = []
            for e in (col // SSD_HEAD_DIM, col // SSD_HEAD_DIM + 1):
                seg = acum[:, e:e + 1] - acum_t[e:e + 1, :]
                m = (cb * jnp.exp(jnp.where(mask, seg, _NEG))).astype(BF16)
                res.append(_dot(m, xp))
            yd = jnp.where(lane < SSD_HEAD_DIM, res[0], res[1])
            y = yd + yoff[:, pr * LANES:(pr + 1) * LANES] * eo_x[:, col:col + LANES]
            y_ref[:, col:col + LANES] = y.astype(BF16)


def _ssd_kernel(xcf_ref, xcb_ref, dtf_ref, dtb_ref, a_ref, e_ref, yf_ref, yb_ref, hf_ref, hb_ref):
    @pl.when(pl.program_id(1) == 0)
    def _():
        hf_ref[...] = jnp.zeros_like(hf_ref)
        hb_ref[...] = jnp.zeros_like(hb_ref)

    _ssd_direction(xcf_ref, dtf_ref, a_ref.at[0], e_ref, hf_ref, yf_ref, False)
    _ssd_direction(xcb_ref, dtb_ref, a_ref.at[1], e_ref, hb_ref, yb_ref, True)


def _ssd(xc, dt, a_log, *, seq):
    t_total = xc.shape[0]
    bsz = t_total // seq
    nc = seq // SSD_CHUNK
    a_rows = jnp.zeros((2, 1, LANES), F32).at[:, 0, :SSD_HEADS].set(-jnp.exp(a_log))
    expand = (jnp.arange(LANES)[:, None] == jnp.arange(SSD_WIDTH)[None, :] // SSD_HEAD_DIM).astype(BF16)
    fwd = lambda b, c: (b * nc + c, 0)
    bwd = lambda b, c: (b * nc + nc - 1 - c, 0)
    state = pltpu.VMEM((SSD_GROUPS, SSD_STATE, SSD_WIDTH // SSD_GROUPS), F32)
    return pl.pallas_call(
        _ssd_kernel,
        grid=(bsz, nc),
        in_specs=[pl.BlockSpec((SSD_CHUNK, XBC_WIDTH), fwd),
                  pl.BlockSpec((SSD_CHUNK, XBC_WIDTH), bwd),
                  pl.BlockSpec((SSD_CHUNK, LANES), fwd),
                  pl.BlockSpec((SSD_CHUNK, LANES), lambda b, c: (b * nc + nc - 1 - c, 1)),
                  _const_spec((2, 1, LANES)),
                  _const_spec((LANES, SSD_WIDTH))],
        out_specs=[pl.BlockSpec((SSD_CHUNK, SSD_WIDTH), fwd),
                   pl.BlockSpec((SSD_CHUNK, SSD_WIDTH), bwd)],
        out_shape=[jax.ShapeDtypeStruct((t_total, SSD_WIDTH), BF16)] * 2,
        scratch_shapes=[state, state],
        compiler_params=_cparams("parallel", "arbitrary"),
        name="ssd",
    )(xc, xc, dt, dt, a_rows, expand)


_ATTN_QB = 2 * ATTN_HALF
_ATTN_WIN = _ATTN_QB + 2 * ATTN_HALF
_ATTN_LQ = 512


def _attn_kernel(q_ref, k_ref, kp_ref, kn_ref, v_ref, vp_ref, vn_ref, bias_ref, o_ref, lse_ref,
                 kw_ref, vw_ref, *, lq, sub_len):
    t = pl.program_id(2)
    hb = ATTN_HALF
    for src, halo_p, halo_n, win in ((k_ref, kp_ref, kn_ref, kw_ref), (v_ref, vp_ref, vn_ref, vw_ref)):
        win[0:hb, :] = halo_p[...]
        win[hb:hb + lq, :] = src[...]
        win[hb + lq:, :] = halo_n[...]
    qn = _ATTN_QB
    lane = lax.broadcasted_iota(jnp.int32, (qn, LANES), 1)
    first_head = lane < HEAD_DIM
    kpos = lax.broadcasted_iota(jnp.int32, (1, _ATTN_WIN), 1)

    def body(qb, carry):
        r0 = pl.multiple_of(qb * qn, qn)
        kidx = t * lq + r0 - hb + kpos
        in_seq = (kidx >= 0) & (kidx < sub_len)
        lse_tile = jnp.zeros((qn, LANES), F32)
        for hp in range(ATTN_HEADS // 2):
            cs = slice(hp * LANES, (hp + 1) * LANES)
            q = q_ref[pl.ds(r0, qn), cs]
            zero = jnp.zeros_like(q)
            q2 = jnp.concatenate([jnp.where(first_head, q, zero), jnp.where(first_head, zero, q)], axis=0)
            s = lax.dot_general(q2, kw_ref[pl.ds(r0, _ATTN_WIN), cs], (((1,), (1,)), ((), ())),
                                preferred_element_type=F32) + bias_ref[hp]
            s = jnp.where(in_seq, s, _NEG)
            m = jnp.max(s, axis=-1, keepdims=True)
            p = jnp.exp(s - m)
            den = jnp.sum(p, axis=-1, keepdims=True)
            pv = _dot(p.astype(BF16), vw_ref[pl.ds(r0, _ATTN_WIN), cs]) * (1.0 / den)
            o_ref[pl.ds(r0, qn), cs] = jnp.where(first_head, pv[0:qn], pv[qn:]).astype(BF16)
            lse = m + jnp.log(den)
            lse_tile = jnp.where(lane == 2 * hp, lse[0:qn], jnp.where(lane == 2 * hp + 1, lse[qn:], lse_tile))
        lse_ref[pl.ds(r0, qn), :] = lse_tile
        return carry

    lax.fori_loop(0, lq // qn, body, 0)


def _attn_bias(dilation):
    slopes = jnp.exp2(-8.0 * jnp.arange(1, ATTN_HEADS + 1, dtype=F32) / ATTN_HEADS)
    rel = jnp.abs(jnp.arange(_ATTN_WIN)[None, :] - ATTN_HALF - jnp.arange(_ATTN_QB)[:, None])
    dist = (rel * dilation).astype(F32)
    b = jnp.where(rel <= ATTN_HALF, -slopes[:, None, None] * dist, _NEG)
    return b.reshape(ATTN_HEADS // 2, 2 * _ATTN_QB, _ATTN_WIN)


def _dilated_attention(q, k, v, dilation, *, seq):
    t_total = q.shape[0]
    bsz = t_total // seq
    sub_len = seq // dilation
    lq = min(_ATTN_LQ, sub_len)
    nt = sub_len // lq
    hpb = lq // ATTN_HALF
    n_hblk = t_total // ATTN_HALF
    tile = lambda b, r, t: (b * dilation + r) * nt + t
    main = lambda b, r, t: (tile(b, r, t), 0)
    prev = lambda b, r, t: (jnp.maximum(tile(b, r, t) * hpb - 1, 0), 0)
    nxt = lambda b, r, t: (jnp.minimum((tile(b, r, t) + 1) * hpb, n_hblk - 1), 0)
    blk = pl.BlockSpec((lq, ATTN_WIDTH), main)
    hblk_p = pl.BlockSpec((ATTN_HALF, ATTN_WIDTH), prev)
    hblk_n = pl.BlockSpec((ATTN_HALF, ATTN_WIDTH), nxt)
    window = pltpu.VMEM((lq + 2 * ATTN_HALF, ATTN_WIDTH), BF16)
    return pl.pallas_call(
        functools.partial(_attn_kernel, lq=lq, sub_len=sub_len),
        grid=(bsz, dilation, nt),
        in_specs=[blk, blk, hblk_p, hblk_n, blk, hblk_p, hblk_n,
                  _const_spec((ATTN_HEADS // 2, 2 * _ATTN_QB, _ATTN_WIN))],
        out_specs=[blk, pl.BlockSpec((lq, LANES), main)],
        out_shape=[jax.ShapeDtypeStruct((t_total, ATTN_WIDTH), BF16),
                   jax.ShapeDtypeStruct((t_total, LANES), F32)],
        scratch_shapes=[window, window],
        compiler_params=_cparams("parallel", "parallel", "parallel"),
        name=f"attn_d{dilation}",
    )(q, k, k, k, v, v, v, _attn_bias(dilation))


def _mix_kernel(o0_ref, o1_ref, o2_ref, l0_ref, l1_ref, l2_ref, yf_ref, yb_ref, xs_ref, z_ref, x_ref,
                mod_ref, e_ref, dskip_ref, gssd_ref, wout_ref, gpost_ref, gpre_ref, wr_ref, br_ref,
                res_ref, h2_ref, route_ref, cnt_ref, lnat_ref, onat_ref, *, tm):
    n_slab = ATTN_WIDTH // LANES

    dilated = ((DILATIONS[1], l1_ref, o1_ref), (DILATIONS[2], l2_ref, o2_ref))
    for gi, (dil, l_ref, _) in enumerate(dilated):
        for rr in range(dil):
            lnat_ref[gi, pl.ds(rr, tm // dil, stride=dil), :] = l_ref[0, rr]

    l0, l1, l2 = l0_ref[...], lnat_ref[0], lnat_ref[1]
    m = jnp.maximum(jnp.maximum(l0, l1), l2)
    es = [jnp.exp(l - m) for l in (l0, l1, l2)]
    inv = 1.0 / (es[0] + es[1] + es[2])
    expand = e_ref[...]

    def widen(w):
        wh, wl = _split2(w)
        return _dot(wh, expand) + _dot(wl, expand)

    attn = widen(es[0] * inv) * o0_ref[...].astype(F32)
    for gi, (dil, _, o_ref) in enumerate(dilated):
        for rr in range(dil):
            for s in range(n_slab):
                onat_ref[s, pl.ds(rr, tm // dil, stride=dil), :] = (
                    o_ref[0, rr, :, s * LANES:(s + 1) * LANES].astype(F32))
        o_nat = jnp.concatenate([onat_ref[s] for s in range(n_slab)], axis=1)
        attn = attn + widen(es[gi + 1] * inv) * o_nat

    xs = xs_ref[...].astype(F32)
    z = z_ref[...].astype(F32)
    y = yf_ref[...].astype(F32) + yb_ref[...].astype(F32) + dskip_ref[...] * xs
    y = _rms(y * (z * _sigmoid(z))) * gssd_ref[...]
    mix = _dot(attn.astype(BF16), wout_ref[0:ATTN_WIDTH, :]) + _dot(y.astype(BF16), wout_ref[ATTN_WIDTH:, :])
    gate_m = mod_ref[0, 2:3, :]
    shift_f = mod_ref[0, 3:4, :]
    scale_f = mod_ref[0, 4:5, :]
    res = x_ref[...] + gate_m * (_rms(mix) * gpost_ref[...])
    res_ref[...] = res
    h2 = _rms(res) * gpre_ref[...] * (1.0 + scale_f) + shift_f
    h2_ref[...] = h2.astype(BF16)

    vals = _dot_f32(h2, wr_ref[...]) + br_ref[...]
    lane = lax.broadcasted_iota(jnp.int32, (tm, LANES), 1)
    sels, tops = [], []
    for _ in range(TOP_K):
        mx = jnp.max(vals, axis=-1, keepdims=True)
        idx = jnp.min(jnp.where(vals == mx, lane, LANES), axis=-1, keepdims=True)
        sel = lane == idx
        sels.append(sel)
        tops.append((mx, idx))
        vals = jnp.where(sel, -jnp.inf, vals)
    ex = [jnp.exp(tv - tops[0][0]) for tv, _ in tops]
    den = ex[0] + ex[1] + ex[2] + ex[3]
    hit = jnp.zeros((tm, LANES), F32)
    for sel in sels:
        hit = jnp.where(sel, 1.0, hit)
    ri = lax.broadcasted_iota(jnp.int32, (tm, tm), 0)
    ci = lax.broadcasted_iota(jnp.int32, (tm, tm), 1)
    before = jnp.where(ci < ri, 1.0, 0.0).astype(BF16)
    rank = _dot(before, hit.astype(BF16))
    cnt = jnp.sum(hit, axis=0, keepdims=True)
    ei = lax.broadcasted_iota(jnp.int32, (LANES, LANES), 0)
    ej = lax.broadcasted_iota(jnp.int32, (LANES, LANES), 1)
    lower = jnp.where(ei < ej, 1.0, 0.0).astype(BF16)
    cnt_h, cnt_l = _split2(jnp.broadcast_to(cnt, (SUBLANES, LANES)))
    first_row = (_dot(cnt_h, lower) + _dot(cnt_l, lower))[0:1, :]
    route = jnp.zeros((tm, LANES), F32)
    for kk, sel in enumerate(sels):
        rk = jnp.sum(jnp.where(sel, rank, 0.0), axis=-1, keepdims=True)
        dest = jnp.sum(jnp.where(sel, rank + first_row, 0.0), axis=-1, keepdims=True)
        for base, val in ((0, tops[kk][1].astype(F32)), (TOP_K, dest), (2 * TOP_K, rk), (3 * TOP_K, ex[kk] / den)):
            route = jnp.where(lane == base + kk, val, route)
    route_ref[...] = route
    cnt_ref[0] = jnp.broadcast_to(cnt, (SUBLANES, LANES)).astype(jnp.int32)


def _mix_and_route(outs, lses, yf, yb, xc, z, x2, mod, d_skip, g_ssd, w_out, g_post, g_pre_ffn,
                   w_router, b_router, *, seq, tm=256):
    t_total = x2.shape[0]
    nt = t_total // tm
    tiles_per_seq = seq // tm
    expand = (jnp.arange(LANES)[:, None] == jnp.arange(ATTN_WIDTH)[None, :] // HEAD_DIM).astype(BF16)
    wr = jnp.zeros((D_MODEL, LANES), F32).at[:, :N_EXPERTS].set(w_router)
    br = jnp.full((1, LANES), _NEG, F32).at[0, :N_EXPERTS].set(b_router)
    row = lambda i: (i, 0)
    wide = pl.BlockSpec((tm, D_MODEL), row)
    narrow = pl.BlockSpec((tm, LANES), row)
    vec = _const_spec((1, D_MODEL))
    bsz = t_total // seq

    def dilated(a, dil):
        a = a.reshape(bsz, dil, seq // dil, a.shape[-1])
        spec = pl.BlockSpec((1, dil, tm // dil, a.shape[-1]),
                            lambda i: (i // tiles_per_seq, 0, i % tiles_per_seq, 0))
        return a, spec

    o_args, o_specs = [outs[0]], [wide]
    l_args, l_specs = [lses[0]], [narrow]
    for dil, o, l in zip(DILATIONS[1:], outs[1:], lses[1:]):
        a, spec = dilated(o, dil)
        o_args.append(a)
        o_specs.append(spec)
        a, spec = dilated(l, dil)
        l_args.append(a)
        l_specs.append(spec)
    outs, lses = o_args, l_args
    return pl.pallas_call(
        functools.partial(_mix_kernel, tm=tm),
        grid=(nt,),
        in_specs=o_specs + l_specs + [wide] * 5 + [
            pl.BlockSpec((1, N_MOD, D_MODEL), lambda i: (i // tiles_per_seq, 0, 0)),
            _const_spec((LANES, ATTN_WIDTH)), vec, vec,
            _const_spec((ATTN_WIDTH + SSD_WIDTH, D_MODEL)), vec, vec,
            _const_spec((D_MODEL, LANES)), _const_spec((1, LANES))],
        out_specs=[wide, wide, narrow, pl.BlockSpec((1, SUBLANES, LANES), lambda i: (i, 0, 0))],
        out_shape=[jax.ShapeDtypeStruct((t_total, D_MODEL), F32),
                   jax.ShapeDtypeStruct((t_total, D_MODEL), BF16),
                   jax.ShapeDtypeStruct((t_total, LANES), F32),
                   jax.ShapeDtypeStruct((nt, SUBLANES, LANES), jnp.int32)],
        scratch_shapes=[pltpu.VMEM((len(DILATIONS) - 1, tm, LANES), F32),
                        pltpu.VMEM((ATTN_WIDTH // LANES, tm, LANES), F32)],
        compiler_params=_cparams("parallel"),
        name="mix_route",
    )(*outs, *lses, yf, yb, xc, z, x2, mod, expand, jnp.repeat(d_skip, SSD_HEAD_DIM).reshape(1, -1),
      g_ssd.reshape(1, -1), w_out.astype(BF16), g_post.reshape(1, -1), g_pre_ffn.reshape(1, -1), wr, br)


_MOE_TM = 256
_ROUTE_TM = 256
_MOE_CAP = 32
_BF16_ROWS = 2 * SUBLANES
_MOE_WIN = _MOE_CAP + _BF16_ROWS
_MOE_TAIL = _MOE_TM
_MOE_SLOTS = N_EXPERTS + _ROUTE_TM * TOP_K // _MOE_CAP
_MOE_KBLOCK = 16
_ROW_SPLIT = D_MODEL // LANES


def _store_rows(ref, lead, row0, vals):
    n = vals.shape[0]
    for q in range(_ROW_SPLIT):
        ref[lead, pl.ds(row0 * _ROW_SPLIT + q, n, stride=_ROW_SPLIT), :] = vals[:, q * LANES:(q + 1) * LANES]


def _load_rows(ref, lead, row0, n):
    parts = [ref[lead, pl.ds(row0 * _ROW_SPLIT + q, n, stride=_ROW_SPLIT), :] for q in range(_ROW_SPLIT)]
    return jnp.concatenate(parts, axis=1)


def _sorted_rows(t_total):
    return t_total * TOP_K + N_EXPERTS * _MOE_CAP + _MOE_TAIL


def _dispatch_kernel(src_ref, dst_ref, num_ref, pad_ref, route_ref, h_ref, xs_ref, stage_ref, sem, *, nt):
    i = pl.program_id(0)
    slot = i % 2
    cap = _MOE_CAP
    n_pairs = _ROUTE_TM * TOP_K
    chunk_rows = cap * _ROW_SPLIT

    def chunk_copy(step, j, s):
        src = pl.multiple_of(src_ref[step * _MOE_SLOTS + j] * _ROW_SPLIT, _ROW_SPLIT)
        dst = pl.multiple_of(dst_ref[step * _MOE_SLOTS + j] * _ROW_SPLIT, _ROW_SPLIT)
        return pltpu.make_async_copy(stage_ref.at[s, pl.ds(src, chunk_rows)], xs_ref.at[pl.ds(dst, chunk_rows)],
                                     sem.at[s])

    @pl.when(i == 0)
    def _():
        for s in range(2):
            stage_ref[s, n_pairs * _ROW_SPLIT:, :] = jnp.zeros((chunk_rows, LANES), F32)

    @pl.when(i < nt)
    def _():
        pos_t = route_ref[...].T[TOP_K:2 * TOP_K, :]
        hb = h_ref[...]
        blk = _ROUTE_TM
        for b in range(n_pairs // blk):
            jrow = (lax.broadcasted_iota(jnp.int32, (blk, _ROUTE_TM), 0) + b * blk).astype(F32)
            onehot = jnp.zeros((blk, _ROUTE_TM), F32)
            for kk in range(TOP_K):
                onehot = onehot + jnp.where(pos_t[kk:kk + 1, :] == jrow, 1.0, 0.0)
            _store_rows(stage_ref, slot, b * blk, _dot(onehot.astype(BF16), hb))

    @pl.when(i > 0)
    def _():
        def wait_prev(j, carry):
            chunk_copy(i - 1, j, 1 - slot).wait()
            return carry
        lax.fori_loop(0, num_ref[i - 1], wait_prev, 0)

    def start(j, carry):
        chunk_copy(i, j, slot).start()
        return carry

    lax.fori_loop(0, num_ref[i], start, 0)

    @pl.when(i == nt)
    def _():
        n_rows = xs_ref.shape[0] // _ROW_SPLIT
        zeros = stage_ref.at[slot, pl.ds(0, chunk_rows)]
        zeros[...] = jnp.zeros((chunk_rows, LANES), F32)
        fills = [pad_ref[e] for e in range(N_EXPERTS)]
        fills += [min(n_rows - _MOE_TAIL + j * cap, n_rows - cap) for j in range(-(-_MOE_TAIL // cap))]
        copies = [pltpu.make_async_copy(zeros, xs_ref.at[pl.ds(pl.multiple_of(r * _ROW_SPLIT, _ROW_SPLIT),
                                                               chunk_rows)], sem.at[slot]) for r in fills]
        for cp in copies:
            cp.start()
        for cp in copies:
            cp.wait()


def _dispatch(h2, route, lists):
    t_total = h2.shape[0]
    nt = t_total // _ROUTE_TM
    n_rows = _sorted_rows(t_total)
    tile = lambda i, *_: (jnp.minimum(i, nt - 1), 0)
    grid_spec = pltpu.PrefetchScalarGridSpec(
        num_scalar_prefetch=4,
        grid=(nt + 1,),
        in_specs=[pl.BlockSpec((_ROUTE_TM, LANES), tile),
                  pl.BlockSpec((_ROUTE_TM, D_MODEL), tile)],
        out_specs=pl.BlockSpec(memory_space=pl.ANY),
        scratch_shapes=[pltpu.VMEM((2, (_ROUTE_TM * TOP_K + _MOE_CAP) * _ROW_SPLIT, LANES), F32),
                        pltpu.SemaphoreType.DMA((2,))])
    return pl.pallas_call(
        functools.partial(_dispatch_kernel, nt=nt),
        grid_spec=grid_spec,
        out_shape=jax.ShapeDtypeStruct((n_rows * _ROW_SPLIT, LANES), F32),
        compiler_params=_cparams("arbitrary"),
        name="moe_dispatch",
    )(lists["stage_row"], lists["sorted_row"], lists["num"], lists["pad_row"], route, h2)


def _gmm_kernel(tile_ref, exp_ref, valid_ref, offs_ref, x_ref, wgu_ref, bgu_ref, wdn_ref, bdn_ref, y_ref,
                wgu_bf, wdn_bf):
    j = pl.program_id(0)
    e = exp_ref[j]
    tile = tile_ref[j]
    prev_j = jnp.maximum(j - 1, 0)
    new_expert = (j == 0) | (exp_ref[prev_j] != e)
    new_tile = (j == 0) | (tile_ref[prev_j] != tile)

    @pl.when(new_expert)
    def _():
        wgu_bf[...] = wgu_ref[0].astype(BF16)
        wdn_bf[...] = wdn_ref[0].astype(BF16)

    @pl.when(valid_ref[j] != 0)
    def _():
        half = D_MODEL // 2
        gu = bgu_ref[0]
        for kc in range(2):
            parts = [x_ref[0, pl.ds(q, _MOE_TM, stride=_ROW_SPLIT), :]
                     for q in range(kc * _ROW_SPLIT // 2, (kc + 1) * _ROW_SPLIT // 2)]
            xk = jnp.concatenate(parts, axis=1).astype(BF16)
            gu = gu + _dot(xk, wgu_bf[kc * half:(kc + 1) * half, :])
        gate = jnp.minimum(gu[:, :EXPERT_FF], SWIGLU_LIMIT)
        up = jnp.clip(gu[:, EXPERT_FF:], -SWIGLU_LIMIT, SWIGLU_LIMIT)
        act = (up + 1.0) * gate * _sigmoid(SWIGLU_ALPHA * gate)
        y = (_dot(act.astype(BF16), wdn_bf[...]) + bdn_ref[0]).astype(BF16)
        rows = tile * _MOE_TM + lax.broadcasted_iota(jnp.int32, (_MOE_TM, 1), 0)
        mine = (rows >= offs_ref[e]) & (rows < offs_ref[e + 1])

        @pl.when(new_tile)
        def _():
            y_ref[...] = jnp.where(mine, y, jnp.zeros_like(y))

        @pl.when(jnp.logical_not(new_tile))
        def _():
            y_ref[...] = jnp.where(mine, y, y_ref[...])


def _grouped_mlp(xs, offs, w_gate_up, b_gate_up, w_down, b_down):
    n_tiles = xs.shape[0] // (_MOE_TM * _ROW_SPLIT)
    n_visits = n_tiles + N_EXPERTS - 1
    first = offs[:-1] // _MOE_TM
    last = (offs[1:] - 1) // _MOE_TM
    per_expert = last - first + 1
    vstart = jnp.concatenate([jnp.zeros((1,), jnp.int32), jnp.cumsum(per_expert)]).astype(jnp.int32)
    total = vstart[-1]
    vis = jnp.minimum(jnp.arange(n_visits, dtype=jnp.int32), total - 1)
    exp_ids = (jnp.sum(vstart[None, :] <= vis[:, None], axis=1) - 1).astype(jnp.int32)
    tile_ids = (first[exp_ids] + vis - vstart[exp_ids]).astype(jnp.int32)
    valid = (jnp.arange(n_visits) < total).astype(jnp.int32)
    grid_spec = pltpu.PrefetchScalarGridSpec(
        num_scalar_prefetch=4,
        grid=(n_visits,),
        in_specs=[pl.BlockSpec((1, _MOE_TM * _ROW_SPLIT, LANES), lambda j, t, e, v, o: (t[j], 0, 0)),
                  pl.BlockSpec((1, D_MODEL, 2 * EXPERT_FF), lambda j, t, e, v, o: (e[j], 0, 0)),
                  pl.BlockSpec((1, 1, 2 * EXPERT_FF), lambda j, t, e, v, o: (e[j], 0, 0)),
                  pl.BlockSpec((1, EXPERT_FF, D_MODEL), lambda j, t, e, v, o: (e[j], 0, 0)),
                  pl.BlockSpec((1, 1, D_MODEL), lambda j, t, e, v, o: (e[j], 0, 0))],
        out_specs=pl.BlockSpec((_MOE_TM, D_MODEL), lambda j, t, e, v, o: (t[j], 0)),
        scratch_shapes=[pltpu.VMEM((D_MODEL, 2 * EXPERT_FF), BF16), pltpu.VMEM((EXPERT_FF, D_MODEL), BF16)])
    tiled = (n_tiles, _MOE_TM * _ROW_SPLIT, LANES)
    return pl.pallas_call(
        _gmm_kernel,
        grid_spec=grid_spec,
        out_shape=jax.ShapeDtypeStruct((n_tiles * _MOE_TM, D_MODEL), BF16),
        compiler_params=_cparams("arbitrary"),
        name="moe_mlp",
    )(tile_ids, exp_ids, valid, offs, xs.reshape(tiled), w_gate_up, b_gate_up.reshape(N_EXPERTS, 1, -1),
      w_down, b_down.reshape(N_EXPERTS, 1, -1))


def _combine_kernel(win_ref, num_ref, ys_ref, route_ref, base_ref, res_ref, mod_ref, g_ref,
                    o_ref, ybuf_ref, acc_ref, sem, *, nt):
    i = pl.program_id(0)
    slot = i % 2
    cap = _MOE_CAP
    win = _MOE_WIN
    tm = _ROUTE_TM
    kcols = _MOE_KBLOCK * win

    def window_copy(step, j, s):
        src = pl.multiple_of(win_ref[step * _MOE_SLOTS + j], _BF16_ROWS)
        return pltpu.make_async_copy(ys_ref.at[pl.ds(src, win)],
                                     ybuf_ref.at[s, pl.ds(pl.multiple_of(j * win, _BF16_ROWS), win)], sem.at[s])

    def start_all(step, s):
        def body(j, carry):
            window_copy(step, j, s).start()
            return carry
        lax.fori_loop(0, num_ref[step], body, 0)

    @pl.when(i == 0)
    def _():
        ybuf_ref[...] = jnp.zeros_like(ybuf_ref)
        start_all(0, 0)

    @pl.when(i + 1 < nt)
    def _():
        start_all(i + 1, 1 - slot)

    def wait_own(j, carry):
        window_copy(i, j, slot).wait()
        return carry

    lax.fori_loop(0, num_ref[i], wait_own, 0)

    route = route_ref[...]
    lane = lax.broadcasted_iota(jnp.int32, (tm, LANES), 1).astype(F32)
    base = base_ref[0, 0:1, :]
    cols, wts = [], []
    for kk in range(TOP_K):
        expert = route[:, kk:kk + 1]
        rank = route[:, 2 * TOP_K + kk:2 * TOP_K + kk + 1]
        chunk = jnp.floor(rank * (1.0 / cap))
        first = jnp.sum(jnp.where(lane == expert, base, 0.0), axis=-1, keepdims=True)
        cols.append(first + chunk * win + (rank - chunk * cap))
        wts.append(route[:, 3 * TOP_K + kk:3 * TOP_K + kk + 1])

    for b in range(_MOE_SLOTS // _MOE_KBLOCK):
        @pl.when(b * _MOE_KBLOCK < num_ref[i])
        def _():
            col = (lax.broadcasted_iota(jnp.int32, (tm, kcols), 1) + b * kcols).astype(F32)
            pw = jnp.zeros((tm, kcols), F32)
            for kk in range(TOP_K):
                pw = pw + jnp.where(col == cols[kk], wts[kk], 0.0)
            hi, lo = _split2(pw)
            part = _dot(jnp.concatenate([hi, lo], axis=0), ybuf_ref[slot, b * kcols:(b + 1) * kcols, :])
            part = part[0:tm] + part[tm:]
            if b == 0:
                acc_ref[...] = part
            else:
                acc_ref[...] += part

    gate_f = mod_ref[0, 5:6, :]
    o_ref[...] = res_ref[...] + gate_f * (_rms(acc_ref[...]) * g_ref[...])


def _combine(ys, route, lists, res, mod, g_post_ffn, *, seq):
    t_total = res.shape[0]
    nt = t_total // _ROUTE_TM
    tiles_per_seq = seq // _ROUTE_TM
    row = lambda i, *_: (i, 0)
    grid_spec = pltpu.PrefetchScalarGridSpec(
        num_scalar_prefetch=2,
        grid=(nt,),
        in_specs=[pl.BlockSpec(memory_space=pl.ANY),
                  pl.BlockSpec((_ROUTE_TM, LANES), row),
                  pl.BlockSpec((1, SUBLANES, LANES), lambda i, *_: (i, 0, 0)),
                  pl.BlockSpec((_ROUTE_TM, D_MODEL), row),
                  pl.BlockSpec((1, N_MOD, D_MODEL), lambda i, *_: (i // tiles_per_seq, 0, 0)),
                  pl.BlockSpec((1, D_MODEL), lambda i, *_: (0, 0))],
        out_specs=pl.BlockSpec((_ROUTE_TM, D_MODEL), row),
        scratch_shapes=[pltpu.VMEM((2, _MOE_SLOTS * _MOE_WIN, D_MODEL), BF16),
                        pltpu.VMEM((_ROUTE_TM, D_MODEL), F32),
                        pltpu.SemaphoreType.DMA((2,))])
    return pl.pallas_call(
        functools.partial(_combine_kernel, nt=nt),
        grid_spec=grid_spec,
        out_shape=jax.ShapeDtypeStruct((t_total, D_MODEL), F32),
        compiler_params=_cparams("arbitrary"),
        name="moe_combine",
    )(lists["window_row"], lists["num"], ys, route, lists["window_base"], res, mod, g_post_ffn.reshape(1, -1))


def _chunk_lists(n_te, offs):
    nt = n_te.shape[0]
    cap = _MOE_CAP
    zero_col = jnp.zeros((nt, 1), jnp.int32)
    zero_row = jnp.zeros((1, N_EXPERTS), jnp.int32)
    before = jnp.concatenate([zero_row, jnp.cumsum(n_te, axis=0)[:-1]], axis=0)
    seg = offs[None, :N_EXPERTS] + before
    first = jnp.concatenate([zero_col, jnp.cumsum(n_te, axis=1)[:, :-1]], axis=1)
    chunk_base = jnp.concatenate([zero_col, jnp.cumsum((n_te + cap - 1) // cap, axis=1)], axis=1)
    slots = jnp.arange(_MOE_SLOTS, dtype=jnp.int32)
    expert = jnp.minimum(jnp.sum(chunk_base[:, 1:, None] <= slots[None, None, :], axis=1), N_EXPERTS - 1)
    take = lambda a: jnp.take_along_axis(a, expert, axis=1)
    chunk = slots[None, :] - take(chunk_base[:, :N_EXPERTS])
    sorted_row = take(seg) + chunk * cap
    num = jnp.concatenate([chunk_base[:, -1], jnp.zeros((1,), jnp.int32)])
    window_base = chunk_base[:, :N_EXPERTS] * _MOE_WIN + seg % _BF16_ROWS
    window_base = jnp.zeros((nt, SUBLANES, LANES), F32).at[:, :, :N_EXPERTS].set(
        window_base.astype(F32)[:, None, :])
    as_i32 = lambda a: a.astype(jnp.int32).reshape(-1)
    return {"stage_row": as_i32(take(first) + chunk * cap),
            "sorted_row": as_i32(sorted_row),
            "window_row": as_i32(sorted_row // _BF16_ROWS * _BF16_ROWS),
            "num": as_i32(num),
            "pad_row": as_i32(offs[1:] - cap).at[N_EXPERTS - 1].add(-_MOE_TAIL),
            "window_base": window_base}


def _forward(x, c, w_ada, b_ada, g_pre_mix, g_post_mix, w_in, conv_w, conv_b, dt_bias, a_log, d_skip,
             g_ssd_norm, w_out, g_pre_ffn, g_post_ffn, w_router, b_router, w_gate_up, b_gate_up, w_down, b_down):
    bsz, seq, _ = x.shape
    x2 = x.astype(F32).reshape(bsz * seq, D_MODEL)
    mod = _ada_mod(c.astype(F32), w_ada, b_ada)
    *qkv, z, xc, dt = _in_proj(x2, mod, g_pre_mix, w_in, conv_w, conv_b, dt_bias, seq=seq)
    attn = []
    for gi, dil in enumerate(DILATIONS):
        q, k, v = (a.reshape(bsz * seq, ATTN_WIDTH) for a in qkv[3 * gi:3 * gi + 3])
        attn.append(_dilated_attention(q, k, v, dil, seq=seq))
    yf, yb = _ssd(xc, dt, a_log, seq=seq)
    res, h2, route, cnt = _mix_and_route(
        [o for o, _ in attn], [l for _, l in attn], yf, yb, xc, z, x2, mod, d_skip, g_ssd_norm, w_out,
        g_post_mix, g_pre_ffn, w_router, b_router, seq=seq, tm=_ROUTE_TM)
    n_te = cnt[:, 0, :N_EXPERTS]
    region = jnp.sum(n_te, axis=0) + _MOE_CAP
    region = region.at[N_EXPERTS - 1].add(_MOE_TAIL)
    offs = jnp.concatenate([jnp.zeros((1,), jnp.int32), jnp.cumsum(region)]).astype(jnp.int32)
    lists = _chunk_lists(n_te, offs)
    xs = _dispatch(h2, route, lists)
    ys = _grouped_mlp(xs, offs, w_gate_up, b_gate_up, w_down, b_down)
    out = _combine(ys, route, lists, res, mod, g_post_ffn, seq=seq)
    return out.reshape(bsz, seq, D_MODEL)


def kernel(x, c, w_ada, b_ada, g_pre_mix, g_post_mix, w_in, conv_w, conv_b, dt_bias, a_log, d_skip, g_ssd_norm, w_out, g_pre_ffn, g_post_ffn, w_router, b_router, w_gate_up, b_gate_up, w_down, b_down):
    layer = lambda t: t[0].astype(F32)
    out = _forward(x, c, layer(w_ada), layer(b_ada), layer(g_pre_mix), layer(g_post_mix), layer(w_in),
                   layer(conv_w), layer(conv_b), layer(dt_bias), layer(a_log), layer(d_skip),
                   layer(g_ssd_norm), layer(w_out), layer(g_pre_ffn), layer(g_post_ffn), layer(w_router),
                   layer(b_router), layer(w_gate_up), layer(b_gate_up), layer(w_down), layer(b_down))
    return out.astype(x.dtype)
```

```python
import functools

import numpy as np
import jax
import jax.numpy as jnp
from jax import lax
from jax.experimental import pallas as pl
from jax.experimental.pallas import tpu as pltpu

F32 = jnp.float32
BF16 = jnp.bfloat16

D_MODEL = 1024
ATTN_HEADS = 16
HEAD_DIM = 64
ATTN_WIDTH = ATTN_HEADS * HEAD_DIM
DILATIONS = (1, 4, 16)
ATTN_HALF = 64
SSD_HEADS = 16
SSD_HEAD_DIM = 64
SSD_WIDTH = SSD_HEADS * SSD_HEAD_DIM
SSD_GROUPS = 2
SSD_STATE = 128
SSD_CHUNK = 128
CONV_WIDTH = 5
XBC_WIDTH = SSD_WIDTH + 2 * SSD_GROUPS * SSD_STATE
N_EXPERTS = 32
TOP_K = 4
EXPERT_FF = 1024
SWIGLU_ALPHA = 1.702
SWIGLU_LIMIT = 7.0
RMS_EPS = 1e-6
N_MOD = 6
LANES = 128
SUBLANES = 8
VMEM_LIMIT = 56 * 1024 * 1024


def _cparams(*sem):
    return pltpu.CompilerParams(dimension_semantics=sem, vmem_limit_bytes=VMEM_LIMIT)


def _const_spec(shape):
    nd = len(shape)
    return pl.BlockSpec(shape, lambda *_: (0,) * nd)


def _split2(a):
    hi = a.astype(BF16)
    lo = (a - hi.astype(F32)).astype(BF16)
    return hi, lo


def _dot(a, b):
    return jnp.dot(a, b, preferred_element_type=F32)


def _dot_f32(a, b):
    ah, al = _split2(a)
    bh, bl = _split2(b)
    return _dot(ah, bh) + (_dot(ah, bl) + _dot(al, bh))


def _sigmoid(x):
    return 1.0 / (1.0 + jnp.exp(-x))


def _rms(x):
    return x * lax.rsqrt(jnp.mean(x * x, axis=-1, keepdims=True) + RMS_EPS)


def _ada_kernel(c_ref, w_ref, b_ref, o_ref):
    c = c_ref[...]
    o_ref[...] = _dot_f32(c * _sigmoid(c), w_ref[...]) + b_ref[...]


def _ada_mod(c, w_ada, b_ada):
    bsz = c.shape[0]
    c8 = jnp.zeros((SUBLANES, D_MODEL), F32).at[:bsz].set(c)
    out = pl.pallas_call(
        _ada_kernel,
        grid=(N_MOD,),
        in_specs=[_const_spec((SUBLANES, D_MODEL)),
                  pl.BlockSpec((D_MODEL, D_MODEL), lambda j: (0, j)),
                  pl.BlockSpec((1, D_MODEL), lambda j: (0, j))],
        out_specs=pl.BlockSpec((SUBLANES, D_MODEL), lambda j: (0, j)),
        out_shape=jax.ShapeDtypeStruct((SUBLANES, N_MOD * D_MODEL), F32),
        compiler_params=_cparams("parallel"),
        name="ada_mod",
    )(c8, w_ada, b_ada.reshape(1, -1))
    return out[:bsz].reshape(bsz, N_MOD, D_MODEL)


def _inproj_kernel(x_ref, xp_ref, xn_ref, mod_ref, g_ref, wqkv_ref, wz_ref, wxbc_ref, wdt_ref,
                   cw_ref, cb_ref, dtb_ref,
                   q1_ref, k1_ref, v1_ref, q4_ref, k4_ref, v4_ref, q16_ref, k16_ref, v16_ref,
                   z_ref, xc_ref, dt_ref, buf_ref, slab_ref, *, tm, tiles_per_seq):
    i = pl.program_id(0)
    shift = mod_ref[0, 0:1, :]
    scale = mod_ref[0, 1:2, :]
    g = g_ref[...]

    def norm_mod(x):
        return _rms(x) * g * (1.0 + scale) + shift

    h = norm_mod(x_ref[...])
    hb = h.astype(BF16)
    n_slab = ATTN_WIDTH // LANES
    outs = ((q1_ref, q4_ref, q16_ref), (k1_ref, k4_ref, k16_ref), (v1_ref, v4_ref, v16_ref))
    for c, (nat_ref, d4_ref, d16_ref) in enumerate(outs):
        r = _dot(hb, wqkv_ref[:, c * ATTN_WIDTH:(c + 1) * ATTN_WIDTH])
        if c == 0:
            r = r * HEAD_DIM ** -0.5
        nat_ref[...] = r.astype(BF16)
        for s in range(n_slab):
            slab_ref[0, s] = r[:, s * LANES:(s + 1) * LANES]
        step = DILATIONS[1]
        sub = tm // step
        for r4 in range(step):
            for s in range(n_slab):
                rows = slab_ref[0, s, pl.ds(r4, sub, stride=step), :]
                d4_ref[0, r4, :, s * LANES:(s + 1) * LANES] = rows.astype(BF16)
                slab_ref[1, s, r4 * sub:(r4 + 1) * sub, :] = rows
        for r4 in range(step):
            for s4 in range(step):
                for s in range(n_slab):
                    rows = slab_ref[1, s, pl.ds(r4 * sub + s4, sub // step, stride=step), :]
                    d16_ref[0, step * s4 + r4, :, s * LANES:(s + 1) * LANES] = rows.astype(BF16)
    z_ref[...] = _dot(hb, wz_ref[...]).astype(BF16)
    dt_ref[...] = jax.nn.softplus(_dot_f32(h, wdt_ref[...]) + dtb_ref[...])

    hh = norm_mod(jnp.concatenate([xp_ref[...], xn_ref[...]], axis=0)).astype(BF16)
    halo = _dot(hh, wxbc_ref[...])
    t_in_seq = i % tiles_per_seq
    prev_ok = (t_in_seq != 0).astype(F32)
    next_ok = (t_in_seq != tiles_per_seq - 1).astype(F32)
    buf_ref[0:SUBLANES, :] = halo[0:SUBLANES] * prev_ok
    buf_ref[SUBLANES + tm:, :] = halo[SUBLANES:] * next_ok
    buf_ref[SUBLANES:SUBLANES + tm, :] = _dot(hb, wxbc_ref[...])
    acc = cb_ref[...]
    for j in range(CONV_WIDTH):
        off = SUBLANES - CONV_WIDTH // 2 + j
        acc = acc + buf_ref[off:off + tm, :] * cw_ref[j:j + 1, :]
    xc_ref[...] = (acc * _sigmoid(acc)).astype(BF16)


def _in_proj(x2, mod, g_pre, w_in, conv_w, conv_b, dt_bias, *, seq, tm=512):
    t_total = x2.shape[0]
    nt = t_total // tm
    tiles_per_seq = seq // tm
    hb = tm // SUBLANES
    n_hblk = t_total // SUBLANES
    qkv_w = 3 * ATTN_WIDTH
    w_qkv = w_in[:, :qkv_w].astype(BF16)
    w_z = w_in[:, qkv_w:qkv_w + SSD_WIDTH].astype(BF16)
    w_xbc = w_in[:, qkv_w + SSD_WIDTH:qkv_w + SSD_WIDTH + XBC_WIDTH].astype(BF16)
    w_dt_raw = w_in[:, qkv_w + SSD_WIDTH + XBC_WIDTH:]
    w_dt = jnp.zeros((D_MODEL, 2 * LANES), F32)
    dtb = jnp.zeros((1, 2 * LANES), F32)
    for dr in range(2):
        w_dt = w_dt.at[:, dr * LANES:dr * LANES + SSD_HEADS].set(w_dt_raw[:, dr * SSD_HEADS:(dr + 1) * SSD_HEADS])
        dtb = dtb.at[0, dr * LANES:dr * LANES + SSD_HEADS].set(dt_bias[dr])
    row = lambda i: (i, 0)
    bsz = t_total // seq
    dil_shapes, dil_specs = [], []
    for dil in DILATIONS[1:]:
        dil_shapes += [jax.ShapeDtypeStruct((bsz, dil, seq // dil, ATTN_WIDTH), BF16)] * 3
        dil_specs += [pl.BlockSpec((1, dil, tm // dil, ATTN_WIDTH),
                                   lambda i: (i // tiles_per_seq, 0, i % tiles_per_seq, 0))] * 3
    out_shape = [jax.ShapeDtypeStruct((t_total, ATTN_WIDTH), BF16)] * 3 + dil_shapes + [
        jax.ShapeDtypeStruct((t_total, SSD_WIDTH), BF16),
        jax.ShapeDtypeStruct((t_total, XBC_WIDTH), BF16),
        jax.ShapeDtypeStruct((t_total, 2 * LANES), F32)]
    return pl.pallas_call(
        functools.partial(_inproj_kernel, tm=tm, tiles_per_seq=tiles_per_seq),
        grid=(nt,),
        in_specs=[pl.BlockSpec((tm, D_MODEL), row),
                  pl.BlockSpec((SUBLANES, D_MODEL), lambda i: (jnp.maximum(i * hb - 1, 0), 0)),
                  pl.BlockSpec((SUBLANES, D_MODEL), lambda i: (jnp.minimum((i + 1) * hb, n_hblk - 1), 0)),
                  pl.BlockSpec((1, N_MOD, D_MODEL), lambda i: (i // tiles_per_seq, 0, 0)),
                  _const_spec((1, D_MODEL)),
                  _const_spec((D_MODEL, qkv_w)),
                  _const_spec((D_MODEL, SSD_WIDTH)),
                  _const_spec((D_MODEL, XBC_WIDTH)),
                  _const_spec((D_MODEL, 2 * LANES)),
                  _const_spec((CONV_WIDTH, XBC_WIDTH)),
                  _const_spec((1, XBC_WIDTH)),
                  _const_spec((1, 2 * LANES))],
        out_specs=[pl.BlockSpec((tm, ATTN_WIDTH), row)] * 3 + dil_specs + [
            pl.BlockSpec((tm, SSD_WIDTH), row),
            pl.BlockSpec((tm, XBC_WIDTH), row),
            pl.BlockSpec((tm, 2 * LANES), row)],
        out_shape=out_shape,
        scratch_shapes=[pltpu.VMEM((tm + 2 * SUBLANES, XBC_WIDTH), F32),
                        pltpu.VMEM((2, ATTN_WIDTH // LANES, tm, LANES), F32)],
        compiler_params=_cparams("parallel"),
        name="in_proj",
    )(x2, x2, x2, mod, g_pre.reshape(1, -1), w_qkv, w_z, w_xbc, w_dt, conv_w, conv_b.reshape(1, -1), dtb)


_NEG = -1e30


def _ssd_direction(xc_ref, dt_ref, a_ref, e_ref, h_ref, y_ref, reverse):
    qn = SSD_CHUNK
    gw = SSD_WIDTH // SSD_GROUPS
    li = lax.broadcasted_iota(jnp.int32, (qn, qn), 0)
    si = lax.broadcasted_iota(jnp.int32, (qn, qn), 1)
    mask = (si >= li) if reverse else (si <= li)
    tri = jnp.where(mask, 1.0, 0.0).astype(BF16)
    dt = dt_ref[...]
    adt = dt * a_ref[...]
    p0 = adt.astype(BF16)
    r0 = adt - p0.astype(F32)
    p1 = r0.astype(BF16)
    p2 = (r0 - p1.astype(F32)).astype(BF16)
    acum = _dot(tri, p0) + (_dot(tri, p1) + _dot(tri, p2))
    last = 0 if reverse else qn - 1
    eo = jnp.exp(acum)
    ds = jnp.exp(acum[last:last + 1, :] - acum)
    sh, sl = _split2(jnp.concatenate([dt, ds, eo], axis=0))
    ex = _dot(sh, e_ref[...]) + _dot(sl, e_ref[...])
    dt_x, ds_x, eo_x = ex[0:qn], ex[qn:2 * qn], ex[2 * qn:3 * qn]
    xd = xc_ref[:, 0:SSD_WIDTH].astype(F32) * dt_x
    xdb = xd.astype(BF16)
    xwb = (xd * ds_x).astype(BF16)
    acum_t = acum.T
    lane = lax.broadcasted_iota(jnp.int32, (qn, LANES), 1)
    for g in range(SSD_GROUPS):
        b0 = SSD_WIDTH + g * SSD_STATE
        c0 = SSD_WIDTH + SSD_GROUPS * SSD_STATE + g * SSD_STATE
        bm = xc_ref[:, b0:b0 + SSD_STATE]
        cm = xc_ref[:, c0:c0 + SSD_STATE]
        cb = lax.dot_general(cm, bm, (((1,), (1,)), ((), ())), preferred_element_type=F32)
        hg = h_ref[g]
        yoff = _dot(cm, hg.astype(BF16))
        st = lax.dot_general(bm, xwb[:, g * gw:(g + 1) * gw], (((0,), (0,)), ((), ())),
                             preferred_element_type=F32)
        h_ref[g] = hg * eo_x[last:last + 1, g * gw:(g + 1) * gw] + st
        for pr in range(gw // LANES):
            col = g * gw + pr * LANES
            xp = xdb[:, col:col + LANES]
            res = []
            for e in (col // SSD_HEAD_DIM, col // SSD_HEAD_DIM + 1):
                seg = acum[:, e:e + 1] - acum_t[e:e + 1, :]
                m = (cb * jnp.exp(jnp.where(mask, seg, _NEG))).astype(BF16)
                res.append(_dot(m, xp))
            yd = jnp.where(lane < SSD_HEAD_DIM, res[0], res[1])
            y = yd + yoff[:, pr * LANES:(pr + 1) * LANES] * eo_x[:, col:col + LANES]
            y_ref[:, col:col + LANES] = y.astype(BF16)


def _ssd_kernel(xcf_ref, xcb_ref, dtf_ref, dtb_ref, a_ref, e_ref, yf_ref, yb_ref, hf_ref, hb_ref):
    @pl.when(pl.program_id(1) == 0)
    def _():
        hf_ref[...] = jnp.zeros_like(hf_ref)
        hb_ref[...] = jnp.zeros_like(hb_ref)

    _ssd_direction(xcf_ref, dtf_ref, a_ref.at[0], e_ref, hf_ref, yf_ref, False)
    _ssd_direction(xcb_ref, dtb_ref, a_ref.at[1], e_ref, hb_ref, yb_ref, True)


def _ssd(xc, dt, a_log, *, seq):
    t_total = xc.shape[0]
    bsz = t_total // seq
    nc = seq // SSD_CHUNK
    a_rows = jnp.zeros((2, 1, LANES), F32).at[:, 0, :SSD_HEADS].set(-jnp.exp(a_log))
    expand = (jnp.arange(LANES)[:, None] == jnp.arange(SSD_WIDTH)[None, :] // SSD_HEAD_DIM).astype(BF16)
    fwd = lambda b, c: (b * nc + c, 0)
    bwd = lambda b, c: (b * nc + nc - 1 - c, 0)
    state = pltpu.VMEM((SSD_GROUPS, SSD_STATE, SSD_WIDTH // SSD_GROUPS), F32)
    return pl.pallas_call(
        _ssd_kernel,
        grid=(bsz, nc),
        in_specs=[pl.BlockSpec((SSD_CHUNK, XBC_WIDTH), fwd),
                  pl.BlockSpec((SSD_CHUNK, XBC_WIDTH), bwd),
                  pl.BlockSpec((SSD_CHUNK, LANES), fwd),
                  pl.BlockSpec((SSD_CHUNK, LANES), lambda b, c: (b * nc + nc - 1 - c, 1)),
                  _const_spec((2, 1, LANES)),
                  _const_spec((LANES, SSD_WIDTH))],
        out_specs=[pl.BlockSpec((SSD_CHUNK, SSD_WIDTH), fwd),
                   pl.BlockSpec((SSD_CHUNK, SSD_WIDTH), bwd)],
        out_shape=[jax.ShapeDtypeStruct((t_total, SSD_WIDTH), BF16)] * 2,
        scratch_shapes=[state, state],
        compiler_params=_cparams("parallel", "arbitrary"),
        name="ssd",
    )(xc, xc, dt, dt, a_rows, expand)


_ATTN_QB = 2 * ATTN_HALF
_ATTN_WIN = _ATTN_QB + 2 * ATTN_HALF
_ATTN_LQ = 512


def _attn_kernel(q_ref, k_ref, kp_ref, kn_ref, v_ref, vp_ref, vn_ref, bias_ref, o_ref, lse_ref,
                 kw_ref, vw_ref, *, lq, sub_len):
    t = pl.program_id(2)
    hb = ATTN_HALF
    for src, halo_p, halo_n, win in ((k_ref, kp_ref, kn_ref, kw_ref), (v_ref, vp_ref, vn_ref, vw_ref)):
        win[0:hb, :] = halo_p[...]
        win[hb:hb + lq, :] = src[...]
        win[hb + lq:, :] = halo_n[...]
    qn = _ATTN_QB
    lane = lax.broadcasted_iota(jnp.int32, (qn, LANES), 1)
    first_head = lane < HEAD_DIM
    kpos = lax.broadcasted_iota(jnp.int32, (1, _ATTN_WIN), 1)

    def body(qb, carry):
        r0 = pl.multiple_of(qb * qn, qn)
        kidx = t * lq + r0 - hb + kpos
        in_seq = (kidx >= 0) & (kidx < sub_len)
        lse_tile = jnp.zeros((qn, LANES), F32)
        for hp in range(ATTN_HEADS // 2):
            cs = slice(hp * LANES, (hp + 1) * LANES)
            q = q_ref[pl.ds(r0, qn), cs]
            zero = jnp.zeros_like(q)
            q2 = jnp.concatenate([jnp.where(first_head, q, zero), jnp.where(first_head, zero, q)], axis=0)
            s = lax.dot_general(q2, kw_ref[pl.ds(r0, _ATTN_WIN), cs], (((1,), (1,)), ((), ())),
                                preferred_element_type=F32) + bias_ref[hp]
            s = jnp.where(in_seq, s, _NEG)
            m = jnp.max(s, axis=-1, keepdims=True)
            p = jnp.exp(s - m)
            den = jnp.sum(p, axis=-1, keepdims=True)
            pv = _dot(p.astype(BF16), vw_ref[pl.ds(r0, _ATTN_WIN), cs]) * (1.0 / den)
            o_ref[pl.ds(r0, qn), cs] = jnp.where(first_head, pv[0:qn], pv[qn:]).astype(BF16)
            lse = m + jnp.log(den)
            lse_tile = jnp.where(lane == 2 * hp, lse[0:qn], jnp.where(lane == 2 * hp + 1, lse[qn:], lse_tile))
        lse_ref[pl.ds(r0, qn), :] = lse_tile
        return carry

    lax.fori_loop(0, lq // qn, body, 0)


def _attn_bias(dilation):
    slopes = jnp.exp2(-8.0 * jnp.arange(1, ATTN_HEADS + 1, dtype=F32) / ATTN_HEADS)
    rel = np.abs(np.arange(_ATTN_WIN)[None, :] - ATTN_HALF - np.arange(_ATTN_QB)[:, None])
    dist = jnp.asarray((rel * dilation).astype(np.float32))
    b = jnp.where(jnp.asarray(rel <= ATTN_HALF), -slopes[:, None, None] * dist, _NEG)
    return b.reshape(ATTN_HEADS // 2, 2 * _ATTN_QB, _ATTN_WIN)


def _dilated_attention(q, k, v, dilation, *, seq):
    t_total = q.shape[0]
    bsz = t_total // seq
    sub_len = seq // dilation
    lq = min(_ATTN_LQ, sub_len)
    nt = sub_len // lq
    hpb = lq // ATTN_HALF
    n_hblk = t_total // ATTN_HALF
    tile = lambda b, r, t: (b * dilation + r) * nt + t
    main = lambda b, r, t: (tile(b, r, t), 0)
    prev = lambda b, r, t: (jnp.maximum(tile(b, r, t) * hpb - 1, 0), 0)
    nxt = lambda b, r, t: (jnp.minimum((tile(b, r, t) + 1) * hpb, n_hblk - 1), 0)
    blk = pl.BlockSpec((lq, ATTN_WIDTH), main)
    hblk_p = pl.BlockSpec((ATTN_HALF, ATTN_WIDTH), prev)
    hblk_n = pl.BlockSpec((ATTN_HALF, ATTN_WIDTH), nxt)
    window = pltpu.VMEM((lq + 2 * ATTN_HALF, ATTN_WIDTH), BF16)
    return pl.pallas_call(
        functools.partial(_attn_kernel, lq=lq, sub_len=sub_len),
        grid=(bsz, dilation, nt),
        in_specs=[blk, blk, hblk_p, hblk_n, blk, hblk_p, hblk_n,
                  _const_spec((ATTN_HEADS // 2, 2 * _ATTN_QB, _ATTN_WIN))],
        out_specs=[blk, pl.BlockSpec((lq, LANES), main)],
        out_shape=[jax.ShapeDtypeStruct((t_total, ATTN_WIDTH), BF16),
                   jax.ShapeDtypeStruct((t_total, LANES), F32)],
        scratch_shapes=[window, window],
        compiler_params=_cparams("parallel", "parallel", "parallel"),
        name=f"attn_d{dilation}",
    )(q, k, k, k, v, v, v, _attn_bias(dilation))


def _mix_kernel(o0_ref, o1_ref, o2_ref, l0_ref, l1_ref, l2_ref, yf_ref, yb_ref, xs_ref, z_ref, x_ref,
                mod_ref, e_ref, dskip_ref, gssd_ref, wout_ref, gpost_ref, gpre_ref, wr_ref, br_ref,
                res_ref, h2_ref, route_ref, cnt_ref, lnat_ref, onat_ref, *, tm):
    n_slab = ATTN_WIDTH // LANES

    dilated = ((DILATIONS[1], l1_ref, o1_ref), (DILATIONS[2], l2_ref, o2_ref))
    for gi, (dil, l_ref, _) in enumerate(dilated):
        for rr in range(dil):
            lnat_ref[gi, pl.ds(rr, tm // dil, stride=dil), :] = l_ref[0, rr]

    l0, l1, l2 = l0_ref[...], lnat_ref[0], lnat_ref[1]
    m = jnp.maximum(jnp.maximum(l0, l1), l2)
    es = [jnp.exp(l - m) for l in (l0, l1, l2)]
    inv = 1.0 / (es[0] + es[1] + es[2])
    expand = e_ref[...]

    def widen(w):
        wh, wl = _split2(w)
        return _dot(wh, expand) + _dot(wl, expand)

    attn = widen(es[0] * inv) * o0_ref[...].astype(F32)
    for gi, (dil, _, o_ref) in enumerate(dilated):
        for rr in range(dil):
            for s in range(n_slab):
                onat_ref[s, pl.ds(rr, tm // dil, stride=dil), :] = (
                    o_ref[0, rr, :, s * LANES:(s + 1) * LANES].astype(F32))
        o_nat = jnp.concatenate([onat_ref[s] for s in range(n_slab)], axis=1)
        attn = attn + widen(es[gi + 1] * inv) * o_nat

    xs = xs_ref[...].astype(F32)
    z = z_ref[...].astype(F32)
    y = yf_ref[...].astype(F32) + yb_ref[...].astype(F32) + dskip_ref[...] * xs
    y = _rms(y * (z * _sigmoid(z))) * gssd_ref[...]
    mix = _dot(attn.astype(BF16), wout_ref[0:ATTN_WIDTH, :]) + _dot(y.astype(BF16), wout_ref[ATTN_WIDTH:, :])
    gate_m = mod_ref[0, 2:3, :]
    shift_f = mod_ref[0, 3:4, :]
    scale_f = mod_ref[0, 4:5, :]
    res = x_ref[...] + gate_m * (_rms(mix) * gpost_ref[...])
    res_ref[...] = res
    h2 = _rms(res) * gpre_ref[...] * (1.0 + scale_f) + shift_f
    h2_ref[...] = h2.astype(BF16)

    vals = _dot_f32(h2, wr_ref[...]) + br_ref[...]
    lane = lax.broadcasted_iota(jnp.int32, (tm, LANES), 1)
    sels, tops = [], []
    for _ in range(TOP_K):
        mx = jnp.max(vals, axis=-1, keepdims=True)
        idx = jnp.min(jnp.where(vals == mx, lane, LANES), axis=-1, keepdims=True)
        sel = lane == idx
        sels.append(sel)
        tops.append((mx, idx))
        vals = jnp.where(sel, -jnp.inf, vals)
    ex = [jnp.exp(tv - tops[0][0]) for tv, _ in tops]
    den = ex[0] + ex[1] + ex[2] + ex[3]
    hit = jnp.zeros((tm, LANES), F32)
    for sel in sels:
        hit = jnp.where(sel, 1.0, hit)
    ri = lax.broadcasted_iota(jnp.int32, (tm, tm), 0)
    ci = lax.broadcasted_iota(jnp.int32, (tm, tm), 1)
    before = jnp.where(ci < ri, 1.0, 0.0).astype(BF16)
    rank = _dot(before, hit.astype(BF16))
    cnt = jnp.sum(hit, axis=0, keepdims=True)
    ei = lax.broadcasted_iota(jnp.int32, (LANES, LANES), 0)
    ej = lax.broadcasted_iota(jnp.int32, (LANES, LANES), 1)
    lower = jnp.where(ei < ej, 1.0, 0.0).astype(BF16)
    cnt_h, cnt_l = _split2(jnp.broadcast_to(cnt, (SUBLANES, LANES)))
    first_row = (_dot(cnt_h, lower) + _dot(cnt_l, lower))[0:1, :]
    route = jnp.zeros((tm, LANES), F32)
    for kk, sel in enumerate(sels):
        rk = jnp.sum(jnp.where(sel, rank, 0.0), axis=-1, keepdims=True)
        dest = jnp.sum(jnp.where(sel, rank + first_row, 0.0), axis=-1, keepdims=True)
        for base, val in ((0, tops[kk][1].astype(F32)), (TOP_K, dest), (2 * TOP_K, rk), (3 * TOP_K, ex[kk] / den)):
            route = jnp.where(lane == base + kk, val, route)
    route_ref[...] = route
    cnt_ref[0] = jnp.broadcast_to(cnt, (SUBLANES, LANES)).astype(jnp.int32)


def _mix_and_route(outs, lses, yf, yb, xc, z, x2, mod, d_skip, g_ssd, w_out, g_post, g_pre_ffn,
                   w_router, b_router, *, seq, tm=256):
    t_total = x2.shape[0]
    nt = t_total // tm
    tiles_per_seq = seq // tm
    expand = (jnp.arange(LANES)[:, None] == jnp.arange(ATTN_WIDTH)[None, :] // HEAD_DIM).astype(BF16)
    wr = jnp.zeros((D_MODEL, LANES), F32).at[:, :N_EXPERTS].set(w_router)
    br = jnp.full((1, LANES), _NEG, F32).at[0, :N_EXPERTS].set(b_router)
    row = lambda i: (i, 0)
    wide = pl.BlockSpec((tm, D_MODEL), row)
    narrow = pl.BlockSpec((tm, LANES), row)
    vec = _const_spec((1, D_MODEL))
    bsz = t_total // seq

    def dilated(a, dil):
        a = a.reshape(bsz, dil, seq // dil, a.shape[-1])
        spec = pl.BlockSpec((1, dil, tm // dil, a.shape[-1]),
                            lambda i: (i // tiles_per_seq, 0, i % tiles_per_seq, 0))
        return a, spec

    o_args, o_specs = [outs[0]], [wide]
    l_args, l_specs = [lses[0]], [narrow]
    for dil, o, l in zip(DILATIONS[1:], outs[1:], lses[1:]):
        a, spec = dilated(o, dil)
        o_args.append(a)
        o_specs.append(spec)
        a, spec = dilated(l, dil)
        l_args.append(a)
        l_specs.append(spec)
    outs, lses = o_args, l_args
    return pl.pallas_call(
        functools.partial(_mix_kernel, tm=tm),
        grid=(nt,),
        in_specs=o_specs + l_specs + [wide] * 5 + [
            pl.BlockSpec((1, N_MOD, D_MODEL), lambda i: (i // tiles_per_seq, 0, 0)),
            _const_spec((LANES, ATTN_WIDTH)), vec, vec,
            _const_spec((ATTN_WIDTH + SSD_WIDTH, D_MODEL)), vec, vec,
            _const_spec((D_MODEL, LANES)), _const_spec((1, LANES))],
        out_specs=[wide, wide, narrow, pl.BlockSpec((1, SUBLANES, LANES), lambda i: (i, 0, 0))],
        out_shape=[jax.ShapeDtypeStruct((t_total, D_MODEL), F32),
                   jax.ShapeDtypeStruct((t_total, D_MODEL), BF16),
                   jax.ShapeDtypeStruct((t_total, LANES), F32),
                   jax.ShapeDtypeStruct((nt, SUBLANES, LANES), jnp.int32)],
        scratch_shapes=[pltpu.VMEM((len(DILATIONS) - 1, tm, LANES), F32),
                        pltpu.VMEM((ATTN_WIDTH // LANES, tm, LANES), F32)],
        compiler_params=_cparams("parallel"),
        name="mix_route",
    )(*outs, *lses, yf, yb, xc, z, x2, mod, expand, jnp.repeat(d_skip, SSD_HEAD_DIM).reshape(1, -1),
      g_ssd.reshape(1, -1), w_out.astype(BF16), g_post.reshape(1, -1), g_pre_ffn.reshape(1, -1), wr, br)


_MOE_TM = 256
_ROUTE_TM = 256
_MOE_CAP = 32
_BF16_ROWS = 2 * SUBLANES
_MOE_WIN = _MOE_CAP + _BF16_ROWS
_MOE_TAIL = _MOE_TM
_MOE_SLOTS = N_EXPERTS + _ROUTE_TM * TOP_K // _MOE_CAP
_MOE_KBLOCK = 16
_ROW_SPLIT = D_MODEL // LANES


def _store_rows(ref, lead, row0, vals):
    n = vals.shape[0]
    for q in range(_ROW_SPLIT):
        ref[lead, pl.ds(row0 * _ROW_SPLIT + q, n, stride=_ROW_SPLIT), :] = vals[:, q * LANES:(q + 1) * LANES]


def _load_rows(ref, lead, row0, n):
    parts = [ref[lead, pl.ds(row0 * _ROW_SPLIT + q, n, stride=_ROW_SPLIT), :] for q in range(_ROW_SPLIT)]
    return jnp.concatenate(parts, axis=1)


def _sorted_rows(t_total):
    return t_total * TOP_K + N_EXPERTS * _MOE_CAP + _MOE_TAIL


def _dispatch_kernel(src_ref, dst_ref, num_ref, pad_ref, route_ref, h_ref, xs_ref, stage_ref, sem, *, nt):
    i = pl.program_id(0)
    slot = i % 2
    cap = _MOE_CAP
    n_pairs = _ROUTE_TM * TOP_K
    chunk_rows = cap * _ROW_SPLIT

    def chunk_copy(step, j, s):
        src = pl.multiple_of(src_ref[step * _MOE_SLOTS + j] * _ROW_SPLIT, _ROW_SPLIT)
        dst = pl.multiple_of(dst_ref[step * _MOE_SLOTS + j] * _ROW_SPLIT, _ROW_SPLIT)
        return pltpu.make_async_copy(stage_ref.at[s, pl.ds(src, chunk_rows)], xs_ref.at[pl.ds(dst, chunk_rows)],
                                     sem.at[s])

    @pl.when(i == 0)
    def _():
        for s in range(2):
            stage_ref[s, n_pairs * _ROW_SPLIT:, :] = jnp.zeros((chunk_rows, LANES), F32)

    @pl.when(i < nt)
    def _():
        pos_t = route_ref[...].T[TOP_K:2 * TOP_K, :]
        hb = h_ref[...]
        blk = _ROUTE_TM
        for b in range(n_pairs // blk):
            jrow = (lax.broadcasted_iota(jnp.int32, (blk, _ROUTE_TM), 0) + b * blk).astype(F32)
            onehot = jnp.zeros((blk, _ROUTE_TM), F32)
            for kk in range(TOP_K):
                onehot = onehot + jnp.where(pos_t[kk:kk + 1, :] == jrow, 1.0, 0.0)
            _store_rows(stage_ref, slot, b * blk, _dot(onehot.astype(BF16), hb))

    @pl.when(i > 0)
    def _():
        def wait_prev(j, carry):
            chunk_copy(i - 1, j, 1 - slot).wait()
            return carry
        lax.fori_loop(0, num_ref[i - 1], wait_prev, 0)

    def start(j, carry):
        chunk_copy(i, j, slot).start()
        return carry

    lax.fori_loop(0, num_ref[i], start, 0)

    @pl.when(i == nt)
    def _():
        n_rows = xs_ref.shape[0] // _ROW_SPLIT
        zeros = stage_ref.at[slot, pl.ds(0, chunk_rows)]
        zeros[...] = jnp.zeros((chunk_rows, LANES), F32)
        fills = [pad_ref[e] for e in range(N_EXPERTS)]
        fills += [min(n_rows - _MOE_TAIL + j * cap, n_rows - cap) for j in range(-(-_MOE_TAIL // cap))]
        copies = [pltpu.make_async_copy(zeros, xs_ref.at[pl.ds(pl.multiple_of(r * _ROW_SPLIT, _ROW_SPLIT),
                                                               chunk_rows)], sem.at[slot]) for r in fills]
        for cp in copies:
            cp.start()
        for cp in copies:
            cp.wait()


def _dispatch(h2, route, lists):
    t_total = h2.shape[0]
    nt = t_total // _ROUTE_TM
    n_rows = _sorted_rows(t_total)
    tile = lambda i, *_: (jnp.minimum(i, nt - 1), 0)
    grid_spec = pltpu.PrefetchScalarGridSpec(
        num_scalar_prefetch=4,
        grid=(nt + 1,),
        in_specs=[pl.BlockSpec((_ROUTE_TM, LANES), tile),
                  pl.BlockSpec((_ROUTE_TM, D_MODEL), tile)],
        out_specs=pl.BlockSpec(memory_space=pl.ANY),
        scratch_shapes=[pltpu.VMEM((2, (_ROUTE_TM * TOP_K + _MOE_CAP) * _ROW_SPLIT, LANES), F32),
                        pltpu.SemaphoreType.DMA((2,))])
    return pl.pallas_call(
        functools.partial(_dispatch_kernel, nt=nt),
        grid_spec=grid_spec,
        out_shape=jax.ShapeDtypeStruct((n_rows * _ROW_SPLIT, LANES), F32),
        compiler_params=_cparams("arbitrary"),
        name="moe_dispatch",
    )(lists["stage_row"], lists["sorted_row"], lists["num"], lists["pad_row"], route, h2)


def _gmm_kernel(tile_ref, exp_ref, valid_ref, offs_ref, x_ref, wgu_ref, bgu_ref, wdn_ref, bdn_ref, y_ref,
                wgu_bf, wdn_bf):
    j = pl.program_id(0)
    e = exp_ref[j]
    tile = tile_ref[j]
    prev_j = jnp.maximum(j - 1, 0)
    new_expert = (j == 0) | (exp_ref[prev_j] != e)
    new_tile = (j == 0) | (tile_ref[prev_j] != tile)

    @pl.when(new_expert)
    def _():
        wgu_bf[...] = wgu_ref[0].astype(BF16)
        wdn_bf[...] = wdn_ref[0].astype(BF16)

    @pl.when(valid_ref[j] != 0)
    def _():
        half = D_MODEL // 2
        gu = bgu_ref[0]
        for kc in range(2):
            parts = [x_ref[0, pl.ds(q, _MOE_TM, stride=_ROW_SPLIT), :]
                     for q in range(kc * _ROW_SPLIT // 2, (kc + 1) * _ROW_SPLIT // 2)]
            xk = jnp.concatenate(parts, axis=1).astype(BF16)
            gu = gu + _dot(xk, wgu_bf[kc * half:(kc + 1) * half, :])
        gate = jnp.minimum(gu[:, :EXPERT_FF], SWIGLU_LIMIT)
        up = jnp.clip(gu[:, EXPERT_FF:], -SWIGLU_LIMIT, SWIGLU_LIMIT)
        act = (up + 1.0) * gate * _sigmoid(SWIGLU_ALPHA * gate)
        y = (_dot(act.astype(BF16), wdn_bf[...]) + bdn_ref[0]).astype(BF16)
        rows = tile * _MOE_TM + lax.broadcasted_iota(jnp.int32, (_MOE_TM, 1), 0)
        mine = (rows >= offs_ref[e]) & (rows < offs_ref[e + 1])

        @pl.when(new_tile)
        def _():
            y_ref[...] = jnp.where(mine, y, jnp.zeros_like(y))

        @pl.when(jnp.logical_not(new_tile))
        def _():
            y_ref[...] = jnp.where(mine, y, y_ref[...])


def _grouped_mlp(xs, offs, w_gate_up, b_gate_up, w_down, b_down):
    n_tiles = xs.shape[0] // (_MOE_TM * _ROW_SPLIT)
    n_visits = n_tiles + N_EXPERTS - 1
    first = offs[:-1] // _MOE_TM
    last = (offs[1:] - 1) // _MOE_TM
    per_expert = last - first + 1
    vstart = jnp.concatenate([jnp.zeros((1,), jnp.int32), jnp.cumsum(per_expert)]).astype(jnp.int32)
    total = vstart[-1]
    vis = jnp.minimum(jnp.arange(n_visits, dtype=jnp.int32), total - 1)
    exp_ids = (jnp.sum(vstart[None, :] <= vis[:, None], axis=1) - 1).astype(jnp.int32)
    tile_ids = (first[exp_ids] + vis - vstart[exp_ids]).astype(jnp.int32)
    valid = (jnp.arange(n_visits) < total).astype(jnp.int32)
    grid_spec = pltpu.PrefetchScalarGridSpec(
        num_scalar_prefetch=4,
        grid=(n_visits,),
        in_specs=[pl.BlockSpec((1, _MOE_TM * _ROW_SPLIT, LANES), lambda j, t, e, v, o: (t[j], 0, 0)),
                  pl.BlockSpec((1, D_MODEL, 2 * EXPERT_FF), lambda j, t, e, v, o: (e[j], 0, 0)),
                  pl.BlockSpec((1, 1, 2 * EXPERT_FF), lambda j, t, e, v, o: (e[j], 0, 0)),
                  pl.BlockSpec((1, EXPERT_FF, D_MODEL), lambda j, t, e, v, o: (e[j], 0, 0)),
                  pl.BlockSpec((1, 1, D_MODEL), lambda j, t, e, v, o: (e[j], 0, 0))],
        out_specs=pl.BlockSpec((_MOE_TM, D_MODEL), lambda j, t, e, v, o: (t[j], 0)),
        scratch_shapes=[pltpu.VMEM((D_MODEL, 2 * EXPERT_FF), BF16), pltpu.VMEM((EXPERT_FF, D_MODEL), BF16)])
    tiled = (n_tiles, _MOE_TM * _ROW_SPLIT, LANES)
    return pl.pallas_call(
        _gmm_kernel,
        grid_spec=grid_spec,
        out_shape=jax.ShapeDtypeStruct((n_tiles * _MOE_TM, D_MODEL), BF16),
        compiler_params=_cparams("arbitrary"),
        name="moe_mlp",
    )(tile_ids, exp_ids, valid, offs, xs.reshape(tiled), w_gate_up, b_gate_up.reshape(N_EXPERTS, 1, -1),
      w_down, b_down.reshape(N_EXPERTS, 1, -1))


def _combine_kernel(win_ref, num_ref, ys_ref, route_ref, base_ref, res_ref, mod_ref, g_ref,
                    o_ref, ybuf_ref, acc_ref, sem, *, nt):
    i = pl.program_id(0)
    slot = i % 2
    cap = _MOE_CAP
    win = _MOE_WIN
    tm = _ROUTE_TM
    kcols = _MOE_KBLOCK * win

    def window_copy(step, j, s):
        src = pl.multiple_of(win_ref[step * _MOE_SLOTS + j], _BF16_ROWS)
        return pltpu.make_async_copy(ys_ref.at[pl.ds(src, win)],
                                     ybuf_ref.at[s, pl.ds(pl.multiple_of(j * win, _BF16_ROWS), win)], sem.at[s])

    def start_all(step, s):
        def body(j, carry):
            window_copy(step, j, s).start()
            return carry
        lax.fori_loop(0, num_ref[step], body, 0)

    @pl.when(i == 0)
    def _():
        ybuf_ref[...] = jnp.zeros_like(ybuf_ref)
        start_all(0, 0)

    @pl.when(i + 1 < nt)
    def _():
        start_all(i + 1, 1 - slot)

    def wait_own(j, carry):
        window_copy(i, j, slot).wait()
        return carry

    lax.fori_loop(0, num_ref[i], wait_own, 0)

    route = route_ref[...]
    lane = lax.broadcasted_iota(jnp.int32, (tm, LANES), 1).astype(F32)
    base = base_ref[0, 0:1, :]
    cols, wts = [], []
    for kk in range(TOP_K):
        expert = route[:, kk:kk + 1]
        rank = route[:, 2 * TOP_K + kk:2 * TOP_K + kk + 1]
        chunk = jnp.floor(rank * (1.0 / cap))
        first = jnp.sum(jnp.where(lane == expert, base, 0.0), axis=-1, keepdims=True)
        cols.append(first + chunk * win + (rank - chunk * cap))
        wts.append(route[:, 3 * TOP_K + kk:3 * TOP_K + kk + 1])

    for b in range(_MOE_SLOTS // _MOE_KBLOCK):
        @pl.when(b * _MOE_KBLOCK < num_ref[i])
        def _():
            col = (lax.broadcasted_iota(jnp.int32, (tm, kcols), 1) + b * kcols).astype(F32)
            pw = jnp.zeros((tm, kcols), F32)
            for kk in range(TOP_K):
                pw = pw + jnp.where(col == cols[kk], wts[kk], 0.0)
            hi, lo = _split2(pw)
            part = _dot(jnp.concatenate([hi, lo], axis=0), ybuf_ref[slot, b * kcols:(b + 1) * kcols, :])
            part = part[0:tm] + part[tm:]
            if b == 0:
                acc_ref[...] = part
            else:
                acc_ref[...] += part

    gate_f = mod_ref[0, 5:6, :]
    o_ref[...] = res_ref[...] + gate_f * (_rms(acc_ref[...]) * g_ref[...])


def _combine(ys, route, lists, res, mod, g_post_ffn, *, seq):
    t_total = res.shape[0]
    nt = t_total // _ROUTE_TM
    tiles_per_seq = seq // _ROUTE_TM
    row = lambda i, *_: (i, 0)
    grid_spec = pltpu.PrefetchScalarGridSpec(
        num_scalar_prefetch=2,
        grid=(nt,),
        in_specs=[pl.BlockSpec(memory_space=pl.ANY),
                  pl.BlockSpec((_ROUTE_TM, LANES), row),
                  pl.BlockSpec((1, SUBLANES, LANES), lambda i, *_: (i, 0, 0)),
                  pl.BlockSpec((_ROUTE_TM, D_MODEL), row),
                  pl.BlockSpec((1, N_MOD, D_MODEL), lambda i, *_: (i // tiles_per_seq, 0, 0)),
                  pl.BlockSpec((1, D_MODEL), lambda i, *_: (0, 0))],
        out_specs=pl.BlockSpec((_ROUTE_TM, D_MODEL), row),
        scratch_shapes=[pltpu.VMEM((2, _MOE_SLOTS * _MOE_WIN, D_MODEL), BF16),
                        pltpu.VMEM((_ROUTE_TM, D_MODEL), F32),
                        pltpu.SemaphoreType.DMA((2,))])
    return pl.pallas_call(
        functools.partial(_combine_kernel, nt=nt),
        grid_spec=grid_spec,
        out_shape=jax.ShapeDtypeStruct((t_total, D_MODEL), F32),
        compiler_params=_cparams("arbitrary"),
        name="moe_combine",
    )(lists["window_row"], lists["num"], ys, route, lists["window_base"], res, mod, g_post_ffn.reshape(1, -1))


def _chunk_lists(n_te, offs):
    nt = n_te.shape[0]
    cap = _MOE_CAP
    zero_col = jnp.zeros((nt, 1), jnp.int32)
    zero_row = jnp.zeros((1, N_EXPERTS), jnp.int32)
    before = jnp.concatenate([zero_row, jnp.cumsum(n_te, axis=0)[:-1]], axis=0)
    seg = offs[None, :N_EXPERTS] + before
    first = jnp.concatenate([zero_col, jnp.cumsum(n_te, axis=1)[:, :-1]], axis=1)
    chunk_base = jnp.concatenate([zero_col, jnp.cumsum((n_te + cap - 1) // cap, axis=1)], axis=1)
    slots = jnp.arange(_MOE_SLOTS, dtype=jnp.int32)
    expert = jnp.minimum(jnp.sum(chunk_base[:, 1:, None] <= slots[None, None, :], axis=1), N_EXPERTS - 1)
    pick = expert[:, None, :] == jnp.arange(N_EXPERTS, dtype=jnp.int32)[None, :, None]
    take = lambda a: jnp.sum(jnp.where(pick, a[:, :, None], 0), axis=1)
    chunk = slots[None, :] - take(chunk_base[:, :N_EXPERTS])
    sorted_row = take(seg) + chunk * cap
    num = jnp.concatenate([chunk_base[:, -1], jnp.zeros((1,), jnp.int32)])
    window_base = chunk_base[:, :N_EXPERTS] * _MOE_WIN + seg % _BF16_ROWS
    window_base = jnp.zeros((nt, SUBLANES, LANES), F32).at[:, :, :N_EXPERTS].set(
        window_base.astype(F32)[:, None, :])
    as_i32 = lambda a: a.astype(jnp.int32).reshape(-1)
    return {"stage_row": as_i32(take(first) + chunk * cap),
            "sorted_row": as_i32(sorted_row),
            "window_row": as_i32(sorted_row // _BF16_ROWS * _BF16_ROWS),
            "num": as_i32(num),
            "pad_row": as_i32(offs[1:] - cap).at[N_EXPERTS - 1].add(-_MOE_TAIL),
            "window_base": window_base}


def _forward(x, c, w_ada, b_ada, g_pre_mix, g_post_mix, w_in, conv_w, conv_b, dt_bias, a_log, d_skip,
             g_ssd_norm, w_out, g_pre_ffn, g_post_ffn, w_router, b_router, w_gate_up, b_gate_up, w_down, b_down):
    bsz, seq, _ = x.shape
    x2 = x.astype(F32).reshape(bsz * seq, D_MODEL)
    mod = _ada_mod(c.astype(F32), w_ada, b_ada)
    *qkv, z, xc, dt = _in_proj(x2, mod, g_pre_mix, w_in, conv_w, conv_b, dt_bias, seq=seq)
    attn = []
    for gi, dil in enumerate(DILATIONS):
        q, k, v = (a.reshape(bsz * seq, ATTN_WIDTH) for a in qkv[3 * gi:3 * gi + 3])
        attn.append(_dilated_attention(q, k, v, dil, seq=seq))
    yf, yb = _ssd(xc, dt, a_log, seq=seq)
    res, h2, route, cnt = _mix_and_route(
        [o for o, _ in attn], [l for _, l in attn], yf, yb, xc, z, x2, mod, d_skip, g_ssd_norm, w_out,
        g_post_mix, g_pre_ffn, w_router, b_router, seq=seq, tm=_ROUTE_TM)
    n_te = cnt[:, 0, :N_EXPERTS]
    region = jnp.sum(n_te, axis=0) + _MOE_CAP
    region = region.at[N_EXPERTS - 1].add(_MOE_TAIL)
    offs = jnp.concatenate([jnp.zeros((1,), jnp.int32), jnp.cumsum(region)]).astype(jnp.int32)
    lists = _chunk_lists(n_te, offs)
    xs = _dispatch(h2, route, lists)
    ys = _grouped_mlp(xs, offs, w_gate_up, b_gate_up, w_down, b_down)
    out = _combine(ys, route, lists, res, mod, g_post_ffn, seq=seq)
    return out.reshape(bsz, seq, D_MODEL)


def kernel(x, c, w_ada, b_ada, g_pre_mix, g_post_mix, w_in, conv_w, conv_b, dt_bias, a_log, d_skip, g_ssd_norm, w_out, g_pre_ffn, g_post_ffn, w_router, b_router, w_gate_up, b_gate_up, w_down, b_down):
    layer = lambda t: t[0].astype(F32)
    out = _forward(x, c, layer(w_ada), layer(b_ada), layer(g_pre_mix), layer(g_post_mix), layer(w_in),
                   layer(conv_w), layer(conv_b), layer(dt_bias), layer(a_log), layer(d_skip),
                   layer(g_ssd_norm), layer(w_out), layer(g_pre_ffn), layer(g_post_ffn), layer(w_router),
                   layer(b_router), layer(w_gate_up), layer(b_gate_up), layer(w_down), layer(b_down))
    return out.astype(x.dtype)
```

```python
import functools

import numpy as np
import jax
import jax.numpy as jnp
from jax import lax
from jax.experimental import pallas as pl
from jax.experimental.pallas import tpu as pltpu

F32 = jnp.float32
BF16 = jnp.bfloat16

D_MODEL = 1024
ATTN_HEADS = 16
HEAD_DIM = 64
ATTN_WIDTH = ATTN_HEADS * HEAD_DIM
DILATIONS = (1, 4, 16)
ATTN_HALF = 64
SSD_HEADS = 16
SSD_HEAD_DIM = 64
SSD_WIDTH = SSD_HEADS * SSD_HEAD_DIM
SSD_GROUPS = 2
SSD_STATE = 128
SSD_CHUNK = 128
CONV_WIDTH = 5
XBC_WIDTH = SSD_WIDTH + 2 * SSD_GROUPS * SSD_STATE
N_EXPERTS = 32
TOP_K = 4
EXPERT_FF = 1024
SWIGLU_ALPHA = 1.702
SWIGLU_LIMIT = 7.0
RMS_EPS = 1e-6
N_MOD = 6
LANES = 128
SUBLANES = 8
VMEM_LIMIT = 56 * 1024 * 1024


def _cparams(*sem):
    return pltpu.CompilerParams(dimension_semantics=sem, vmem_limit_bytes=VMEM_LIMIT)


def _const_spec(shape):
    nd = len(shape)
    return pl.BlockSpec(shape, lambda *_: (0,) * nd)


def _split2(a):
    hi = a.astype(BF16)
    lo = (a - hi.astype(F32)).astype(BF16)
    return hi, lo


def _dot(a, b):
    return jnp.dot(a, b, preferred_element_type=F32)


def _dot_f32(a, b):
    ah, al = _split2(a)
    bh, bl = _split2(b)
    return _dot(ah, bh) + (_dot(ah, bl) + _dot(al, bh))


def _sigmoid(x):
    return 1.0 / (1.0 + jnp.exp(-x))


def _rms(x):
    return x * lax.rsqrt(jnp.mean(x * x, axis=-1, keepdims=True) + RMS_EPS)


def _ada_kernel(c_ref, w_ref, b_ref, o_ref):
    c = c_ref[...]
    o_ref[...] = _dot_f32(c * _sigmoid(c), w_ref[...]) + b_ref[...]


def _ada_mod(c, w_ada, b_ada):
    bsz = c.shape[0]
    c8 = jnp.zeros((SUBLANES, D_MODEL), F32).at[:bsz].set(c)
    out = pl.pallas_call(
        _ada_kernel,
        grid=(N_MOD,),
        in_specs=[_const_spec((SUBLANES, D_MODEL)),
                  pl.BlockSpec((D_MODEL, D_MODEL), lambda j: (0, j)),
                  pl.BlockSpec((1, D_MODEL), lambda j: (0, j))],
        out_specs=pl.BlockSpec((SUBLANES, D_MODEL), lambda j: (0, j)),
        out_shape=jax.ShapeDtypeStruct((SUBLANES, N_MOD * D_MODEL), F32),
        compiler_params=_cparams("parallel"),
        name="ada_mod",
    )(c8, w_ada, b_ada.reshape(1, -1))
    return out[:bsz].reshape(bsz, N_MOD, D_MODEL)


def _inproj_kernel(x_ref, xp_ref, xn_ref, mod_ref, g_ref, wqkv_ref, wz_ref, wxbc_ref, wdt_ref,
                   cw_ref, cb_ref, dtb_ref,
                   q1_ref, k1_ref, v1_ref, q4_ref, k4_ref, v4_ref, q16_ref, k16_ref, v16_ref,
                   z_ref, xc_ref, dt_ref, buf_ref, slab_ref, *, tm, tiles_per_seq):
    i = pl.program_id(0)
    shift = mod_ref[0, 0:1, :]
    scale = mod_ref[0, 1:2, :]
    g = g_ref[...]

    def norm_mod(x):
        return _rms(x) * g * (1.0 + scale) + shift

    h = norm_mod(x_ref[...])
    hb = h.astype(BF16)
    n_slab = ATTN_WIDTH // LANES
    outs = ((q1_ref, q4_ref, q16_ref), (k1_ref, k4_ref, k16_ref), (v1_ref, v4_ref, v16_ref))
    for c, (nat_ref, d4_ref, d16_ref) in enumerate(outs):
        r = _dot(hb, wqkv_ref[:, c * ATTN_WIDTH:(c + 1) * ATTN_WIDTH])
        if c == 0:
            r = r * HEAD_DIM ** -0.5
        nat_ref[...] = r.astype(BF16)
        for s in range(n_slab):
            slab_ref[0, s] = r[:, s * LANES:(s + 1) * LANES]
        step = DILATIONS[1]
        sub = tm // step
        for r4 in range(step):
            for s in range(n_slab):
                rows = slab_ref[0, s, pl.ds(r4, sub, stride=step), :]
                d4_ref[0, r4, :, s * LANES:(s + 1) * LANES] = rows.astype(BF16)
                slab_ref[1, s, r4 * sub:(r4 + 1) * sub, :] = rows
        for r4 in range(step):
            for s4 in range(step):
                for s in range(n_slab):
                    rows = slab_ref[1, s, pl.ds(r4 * sub + s4, sub // step, stride=step), :]
                    d16_ref[0, step * s4 + r4, :, s * LANES:(s + 1) * LANES] = rows.astype(BF16)
    z_ref[...] = _dot(hb, wz_ref[...]).astype(BF16)
    dt_ref[...] = jax.nn.softplus(_dot_f32(h, wdt_ref[...]) + dtb_ref[...])

    hh = norm_mod(jnp.concatenate([xp_ref[...], xn_ref[...]], axis=0)).astype(BF16)
    halo = _dot(hh, wxbc_ref[...])
    t_in_seq = i % tiles_per_seq
    prev_ok = (t_in_seq != 0).astype(F32)
    next_ok = (t_in_seq != tiles_per_seq - 1).astype(F32)
    buf_ref[0:SUBLANES, :] = halo[0:SUBLANES] * prev_ok
    buf_ref[SUBLANES + tm:, :] = halo[SUBLANES:] * next_ok
    buf_ref[SUBLANES:SUBLANES + tm, :] = _dot(hb, wxbc_ref[...])
    acc = cb_ref[...]
    for j in range(CONV_WIDTH):
        off = SUBLANES - CONV_WIDTH // 2 + j
        acc = acc + buf_ref[off:off + tm, :] * cw_ref[j:j + 1, :]
    xc_ref[...] = (acc * _sigmoid(acc)).astype(BF16)


def _in_proj(x2, mod, g_pre, w_in, conv_w, conv_b, dt_bias, *, seq, tm=512):
    t_total = x2.shape[0]
    nt = t_total // tm
    tiles_per_seq = seq // tm
    hb = tm // SUBLANES
    n_hblk = t_total // SUBLANES
    qkv_w = 3 * ATTN_WIDTH
    w_qkv = w_in[:, :qkv_w].astype(BF16)
    w_z = w_in[:, qkv_w:qkv_w + SSD_WIDTH].astype(BF16)
    w_xbc = w_in[:, qkv_w + SSD_WIDTH:qkv_w + SSD_WIDTH + XBC_WIDTH].astype(BF16)
    w_dt_raw = w_in[:, qkv_w + SSD_WIDTH + XBC_WIDTH:]
    w_dt = jnp.zeros((D_MODEL, 2 * LANES), F32)
    dtb = jnp.zeros((1, 2 * LANES), F32)
    for dr in range(2):
        w_dt = w_dt.at[:, dr * LANES:dr * LANES + SSD_HEADS].set(w_dt_raw[:, dr * SSD_HEADS:(dr + 1) * SSD_HEADS])
        dtb = dtb.at[0, dr * LANES:dr * LANES + SSD_HEADS].set(dt_bias[dr])
    row = lambda i: (i, 0)
    bsz = t_total // seq
    dil_shapes, dil_specs = [], []
    for dil in DILATIONS[1:]:
        dil_shapes += [jax.ShapeDtypeStruct((bsz, dil, seq // dil, ATTN_WIDTH), BF16)] * 3
        dil_specs += [pl.BlockSpec((1, dil, tm // dil, ATTN_WIDTH),
                                   lambda i: (i // tiles_per_seq, 0, i % tiles_per_seq, 0))] * 3
    out_shape = [jax.ShapeDtypeStruct((t_total, ATTN_WIDTH), BF16)] * 3 + dil_shapes + [
        jax.ShapeDtypeStruct((t_total, SSD_WIDTH), BF16),
        jax.ShapeDtypeStruct((t_total, XBC_WIDTH), BF16),
        jax.ShapeDtypeStruct((t_total, 2 * LANES), F32)]
    return pl.pallas_call(
        functools.partial(_inproj_kernel, tm=tm, tiles_per_seq=tiles_per_seq),
        grid=(nt,),
        in_specs=[pl.BlockSpec((tm, D_MODEL), row),
                  pl.BlockSpec((SUBLANES, D_MODEL), lambda i: (jnp.maximum(i * hb - 1, 0), 0)),
                  pl.BlockSpec((SUBLANES, D_MODEL), lambda i: (jnp.minimum((i + 1) * hb, n_hblk - 1), 0)),
                  pl.BlockSpec((1, N_MOD, D_MODEL), lambda i: (i // tiles_per_seq, 0, 0)),
                  _const_spec((1, D_MODEL)),
                  _const_spec((D_MODEL, qkv_w)),
                  _const_spec((D_MODEL, SSD_WIDTH)),
                  _const_spec((D_MODEL, XBC_WIDTH)),
                  _const_spec((D_MODEL, 2 * LANES)),
                  _const_spec((CONV_WIDTH, XBC_WIDTH)),
                  _const_spec((1, XBC_WIDTH)),
                  _const_spec((1, 2 * LANES))],
        out_specs=[pl.BlockSpec((tm, ATTN_WIDTH), row)] * 3 + dil_specs + [
            pl.BlockSpec((tm, SSD_WIDTH), row),
            pl.BlockSpec((tm, XBC_WIDTH), row),
            pl.BlockSpec((tm, 2 * LANES), row)],
        out_shape=out_shape,
        scratch_shapes=[pltpu.VMEM((tm + 2 * SUBLANES, XBC_WIDTH), F32),
                        pltpu.VMEM((2, ATTN_WIDTH // LANES, tm, LANES), F32)],
        compiler_params=_cparams("parallel"),
        name="in_proj",
    )(x2, x2, x2, mod, g_pre.reshape(1, -1), w_qkv, w_z, w_xbc, w_dt, conv_w, conv_b.reshape(1, -1), dtb)


_NEG = -1e30


def _ssd_direction(xc_ref, dt_ref, a_ref, e_ref, h_ref, y_ref, reverse):
    qn = SSD_CHUNK
    gw = SSD_WIDTH // SSD_GROUPS
    li = lax.broadcasted_iota(jnp.int32, (qn, qn), 0)
    si = lax.broadcasted_iota(jnp.int32, (qn, qn), 1)
    mask = (si >= li) if reverse else (si <= li)
    tri = jnp.where(mask, 1.0, 0.0).astype(BF16)
    dt = dt_ref[...]
    adt = dt * a_ref[...]
    p0 = adt.astype(BF16)
    r0 = adt - p0.astype(F32)
    p1 = r0.astype(BF16)
    p2 = (r0 - p1.astype(F32)).astype(BF16)
    acum = _dot(tri, p0) + (_dot(tri, p1) + _dot(tri, p2))
    last = 0 if reverse else qn - 1
    eo = jnp.exp(acum)
    ds = jnp.exp(acum[last:last + 1, :] - acum)
    dt_h, dt_l = _split2(dt)
    eo_h, eo_l = _split2(eo)
    ex = _dot(jnp.concatenate([dt_h, eo_h, ds.astype(BF16)], axis=0), e_ref[...])
    ex_l = _dot(jnp.concatenate([dt_l, eo_l], axis=0), e_ref[...])
    dt_x = ex[0:qn] + ex_l[0:qn]
    eo_x = ex[qn:2 * qn] + ex_l[qn:]
    ds_x = ex[2 * qn:]
    xd = xc_ref[:, 0:SSD_WIDTH].astype(F32) * dt_x
    xdb = xd.astype(BF16)
    xwb = (xd * ds_x).astype(BF16)
    acum_t = acum.T
    lane = lax.broadcasted_iota(jnp.int32, (qn, LANES), 1)
    for g in range(SSD_GROUPS):
        b0 = SSD_WIDTH + g * SSD_STATE
        c0 = SSD_WIDTH + SSD_GROUPS * SSD_STATE + g * SSD_STATE
        bm = xc_ref[:, b0:b0 + SSD_STATE]
        cm = xc_ref[:, c0:c0 + SSD_STATE]
        cb = lax.dot_general(cm, bm, (((1,), (1,)), ((), ())), preferred_element_type=F32)
        hg = h_ref[g]
        yoff = _dot(cm, hg.astype(BF16))
        st = lax.dot_general(bm, xwb[:, g * gw:(g + 1) * gw], (((0,), (0,)), ((), ())),
                             preferred_element_type=F32)
        h_ref[g] = hg * eo_x[last:last + 1, g * gw:(g + 1) * gw] + st
        for pr in range(gw // LANES):
            col = g * gw + pr * LANES
            xp = xdb[:, col:col + LANES]
            res = []
            for e in (col // SSD_HEAD_DIM, col // SSD_HEAD_DIM + 1):
                seg = acum[:, e:e + 1] - acum_t[e:e + 1, :]
                m = (cb * jnp.exp(jnp.where(mask, seg, _NEG))).astype(BF16)
                res.append(_dot(m, xp))
            yd = jnp.where(lane < SSD_HEAD_DIM, res[0], res[1])
            y = yd + yoff[:, pr * LANES:(pr + 1) * LANES] * eo_x[:, col:col + LANES]
            y_ref[:, col:col + LANES] = y.astype(BF16)


def _ssd_kernel(xcf_ref, xcb_ref, dtf_ref, dtb_ref, a_ref, e_ref, yf_ref, yb_ref, hf_ref, hb_ref):
    @pl.when(pl.program_id(1) == 0)
    def _():
        hf_ref[...] = jnp.zeros_like(hf_ref)
        hb_ref[...] = jnp.zeros_like(hb_ref)

    qn = SSD_CHUNK
    for c in range(_SSD_STEP_CHUNKS):
        f0 = c * qn
        b0 = (_SSD_STEP_CHUNKS - 1 - c) * qn
        _ssd_direction(xcf_ref.at[pl.ds(f0, qn)], dtf_ref.at[pl.ds(f0, qn)], a_ref.at[0], e_ref, hf_ref,
                       yf_ref.at[pl.ds(f0, qn)], False)
        _ssd_direction(xcb_ref.at[pl.ds(b0, qn)], dtb_ref.at[pl.ds(b0, qn)], a_ref.at[1], e_ref, hb_ref,
                       yb_ref.at[pl.ds(b0, qn)], True)


_SSD_STEP_CHUNKS = 2


def _ssd(xc, dt, a_log, *, seq):
    t_total = xc.shape[0]
    bsz = t_total // seq
    rows = _SSD_STEP_CHUNKS * SSD_CHUNK
    nc = seq // rows
    a_rows = jnp.zeros((2, 1, LANES), F32).at[:, 0, :SSD_HEADS].set(-jnp.exp(a_log))
    expand = (jnp.arange(LANES)[:, None] == jnp.arange(SSD_WIDTH)[None, :] // SSD_HEAD_DIM).astype(BF16)
    fwd = lambda b, c: (b * nc + c, 0)
    bwd = lambda b, c: (b * nc + nc - 1 - c, 0)
    state = pltpu.VMEM((SSD_GROUPS, SSD_STATE, SSD_WIDTH // SSD_GROUPS), F32)
    return pl.pallas_call(
        _ssd_kernel,
        grid=(bsz, nc),
        in_specs=[pl.BlockSpec((rows, XBC_WIDTH), fwd),
                  pl.BlockSpec((rows, XBC_WIDTH), bwd),
                  pl.BlockSpec((rows, LANES), fwd),
                  pl.BlockSpec((rows, LANES), lambda b, c: (b * nc + nc - 1 - c, 1)),
                  _const_spec((2, 1, LANES)),
                  _const_spec((LANES, SSD_WIDTH))],
        out_specs=[pl.BlockSpec((rows, SSD_WIDTH), fwd),
                   pl.BlockSpec((rows, SSD_WIDTH), bwd)],
        out_shape=[jax.ShapeDtypeStruct((t_total, SSD_WIDTH), BF16)] * 2,
        scratch_shapes=[state, state],
        compiler_params=_cparams("parallel", "arbitrary"),
        name="ssd",
    )(xc, xc, dt, dt, a_rows, expand)


_ATTN_QB = 2 * ATTN_HALF
_ATTN_WIN = _ATTN_QB + 2 * ATTN_HALF
_ATTN_LQ = 512


def _attn_kernel(q_ref, k_ref, kp_ref, kn_ref, v_ref, vp_ref, vn_ref, bias_ref, o_ref, lse_ref,
                 kw_ref, vw_ref, *, lq, sub_len):
    t = pl.program_id(2)
    hb = ATTN_HALF
    for src, halo_p, halo_n, win in ((k_ref, kp_ref, kn_ref, kw_ref), (v_ref, vp_ref, vn_ref, vw_ref)):
        win[0:hb, :] = halo_p[...]
        win[hb:hb + lq, :] = src[...]
        win[hb + lq:, :] = halo_n[...]
    qn = _ATTN_QB
    lane = lax.broadcasted_iota(jnp.int32, (qn, LANES), 1)
    first_head = lane < HEAD_DIM
    kpos = lax.broadcasted_iota(jnp.int32, (1, _ATTN_WIN), 1)

    def body(qb, carry):
        r0 = pl.multiple_of(qb * qn, qn)
        kidx = t * lq + r0 - hb + kpos
        in_seq = (kidx >= 0) & (kidx < sub_len)
        lse_tile = jnp.zeros((qn, LANES), F32)
        for hp in range(ATTN_HEADS // 2):
            cs = slice(hp * LANES, (hp + 1) * LANES)
            q = q_ref[pl.ds(r0, qn), cs]
            zero = jnp.zeros_like(q)
            q2 = jnp.concatenate([jnp.where(first_head, q, zero), jnp.where(first_head, zero, q)], axis=0)
            s = lax.dot_general(q2, kw_ref[pl.ds(r0, _ATTN_WIN), cs], (((1,), (1,)), ((), ())),
                                preferred_element_type=F32) + bias_ref[hp]
            s = jnp.where(in_seq, s, _NEG)
            m = jnp.max(s, axis=-1, keepdims=True)
            p = jnp.exp(s - m)
            den = jnp.sum(p, axis=-1, keepdims=True)
            pv = _dot(p.astype(BF16), vw_ref[pl.ds(r0, _ATTN_WIN), cs]) * (1.0 / den)
            o_ref[pl.ds(r0, qn), cs] = jnp.where(first_head, pv[0:qn], pv[qn:]).astype(BF16)
            lse = m + jnp.log(den)
            lse_tile = jnp.where(lane == 2 * hp, lse[0:qn], jnp.where(lane == 2 * hp + 1, lse[qn:], lse_tile))
        lse_ref[pl.ds(r0, qn), :] = lse_tile
        return carry

    lax.fori_loop(0, lq // qn, body, 0, unroll=2)


def _attn_bias(dilation):
    slopes = jnp.exp2(-8.0 * jnp.arange(1, ATTN_HEADS + 1, dtype=F32) / ATTN_HEADS)
    rel = np.abs(np.arange(_ATTN_WIN)[None, :] - ATTN_HALF - np.arange(_ATTN_QB)[:, None])
    dist = jnp.asarray((rel * dilation).astype(np.float32))
    b = jnp.where(jnp.asarray(rel <= ATTN_HALF), -slopes[:, None, None] * dist, _NEG)
    return b.reshape(ATTN_HEADS // 2, 2 * _ATTN_QB, _ATTN_WIN)


def _dilated_attention(q, k, v, dilation, *, seq):
    t_total = q.shape[0]
    bsz = t_total // seq
    sub_len = seq // dilation
    lq = min(_ATTN_LQ, sub_len)
    nt = sub_len // lq
    hpb = lq // ATTN_HALF
    n_hblk = t_total // ATTN_HALF
    tile = lambda b, r, t: (b * dilation + r) * nt + t
    main = lambda b, r, t: (tile(b, r, t), 0)
    prev = lambda b, r, t: (jnp.maximum(tile(b, r, t) * hpb - 1, 0), 0)
    nxt = lambda b, r, t: (jnp.minimum((tile(b, r, t) + 1) * hpb, n_hblk - 1), 0)
    blk = pl.BlockSpec((lq, ATTN_WIDTH), main)
    hblk_p = pl.BlockSpec((ATTN_HALF, ATTN_WIDTH), prev)
    hblk_n = pl.BlockSpec((ATTN_HALF, ATTN_WIDTH), nxt)
    window = pltpu.VMEM((lq + 2 * ATTN_HALF, ATTN_WIDTH), BF16)
    return pl.pallas_call(
        functools.partial(_attn_kernel, lq=lq, sub_len=sub_len),
        grid=(bsz, dilation, nt),
        in_specs=[blk, blk, hblk_p, hblk_n, blk, hblk_p, hblk_n,
                  _const_spec((ATTN_HEADS // 2, 2 * _ATTN_QB, _ATTN_WIN))],
        out_specs=[blk, pl.BlockSpec((lq, LANES), main)],
        out_shape=[jax.ShapeDtypeStruct((t_total, ATTN_WIDTH), BF16),
                   jax.ShapeDtypeStruct((t_total, LANES), F32)],
        scratch_shapes=[window, window],
        compiler_params=_cparams("parallel", "parallel", "parallel"),
        name=f"attn_d{dilation}",
    )(q, k, k, k, v, v, v, _attn_bias(dilation))


def _mix_kernel(o0_ref, o1_ref, o2_ref, l0_ref, l1_ref, l2_ref, yf_ref, yb_ref, xs_ref, z_ref, x_ref,
                mod_ref, e_ref, dskip_ref, gssd_ref, wout_ref, gpost_ref, gpre_ref, wr_ref, br_ref,
                res_ref, h2_ref, route_ref, cnt_ref, lnat_ref, onat_ref, *, tm):
    n_slab = ATTN_WIDTH // LANES

    dilated = ((DILATIONS[1], l1_ref, o1_ref), (DILATIONS[2], l2_ref, o2_ref))
    for gi, (dil, l_ref, _) in enumerate(dilated):
        for rr in range(dil):
            lnat_ref[gi, pl.ds(rr, tm // dil, stride=dil), :] = l_ref[0, rr]

    l0, l1, l2 = l0_ref[...], lnat_ref[0], lnat_ref[1]
    m = jnp.maximum(jnp.maximum(l0, l1), l2)
    es = [jnp.exp(l - m) for l in (l0, l1, l2)]
    inv = 1.0 / (es[0] + es[1] + es[2])
    expand = e_ref[...]

    def widen(w):
        wh, wl = _split2(w)
        return _dot(wh, expand) + _dot(wl, expand)

    attn = widen(es[0] * inv) * o0_ref[...].astype(F32)
    for gi, (dil, _, o_ref) in enumerate(dilated):
        for rr in range(dil):
            for s in range(n_slab):
                onat_ref[s, pl.ds(rr, tm // dil, stride=dil), :] = (
                    o_ref[0, rr, :, s * LANES:(s + 1) * LANES].astype(F32))
        o_nat = jnp.concatenate([onat_ref[s] for s in range(n_slab)], axis=1)
        attn = attn + widen(es[gi + 1] * inv) * o_nat

    xs = xs_ref[...].astype(F32)
    z = z_ref[...].astype(F32)
    y = yf_ref[...].astype(F32) + yb_ref[...].astype(F32) + dskip_ref[...] * xs
    y = _rms(y * (z * _sigmoid(z))) * gssd_ref[...]
    mix = _dot(attn.astype(BF16), wout_ref[0:ATTN_WIDTH, :]) + _dot(y.astype(BF16), wout_ref[ATTN_WIDTH:, :])
    gate_m = mod_ref[0, 2:3, :]
    shift_f = mod_ref[0, 3:4, :]
    scale_f = mod_ref[0, 4:5, :]
    res = x_ref[...] + gate_m * (_rms(mix) * gpost_ref[...])
    res_ref[...] = res
    h2 = _rms(res) * gpre_ref[...] * (1.0 + scale_f) + shift_f
    h2_ref[...] = h2.astype(BF16)

    vals = _dot_f32(h2, wr_ref[...]) + br_ref[...]
    lane = lax.broadcasted_iota(jnp.int32, (tm, LANES), 1)
    sels, tops = [], []
    for _ in range(TOP_K):
        mx = jnp.max(vals, axis=-1, keepdims=True)
        idx = jnp.min(jnp.where(vals == mx, lane, LANES), axis=-1, keepdims=True)
        sel = lane == idx
        sels.append(sel)
        tops.append((mx, idx))
        vals = jnp.where(sel, -jnp.inf, vals)
    ex = [jnp.exp(tv - tops[0][0]) for tv, _ in tops]
    hit = jnp.zeros((tm, LANES), F32)
    for sel in sels:
        hit = jnp.where(sel, 1.0, hit)
    ri = lax.broadcasted_iota(jnp.int32, (tm, tm), 0)
    ci = lax.broadcasted_iota(jnp.int32, (tm, tm), 1)
    before = jnp.where(ci < ri, 1.0, 0.0).astype(BF16)
    rank = _dot(before, hit.astype(BF16))
    cnt = jnp.sum(hit, axis=0, keepdims=True)
    ei = lax.broadcasted_iota(jnp.int32, (LANES, LANES), 0)
    ej = lax.broadcasted_iota(jnp.int32, (LANES, LANES), 1)
    lower = jnp.where(ei < ej, 1.0, 0.0).astype(BF16)
    cnt_h, cnt_l = _split2(jnp.broadcast_to(cnt, (SUBLANES, LANES)))
    first_row = (_dot(cnt_h, lower) + _dot(cnt_l, lower))[0:1, :]
    route = jnp.zeros((tm, LANES), F32)
    for kk, sel in enumerate(sels):
        rk = jnp.sum(jnp.where(sel, rank, 0.0), axis=-1, keepdims=True)
        dest = jnp.sum(jnp.where(sel, rank + first_row, 0.0), axis=-1, keepdims=True)
        for base, val in ((0, tops[kk][1].astype(F32)), (TOP_K, dest), (2 * TOP_K, rk), (3 * TOP_K, ex[kk])):
            route = jnp.where(lane == base + kk, val, route)
    route_ref[...] = route
    cnt_ref[0] = jnp.broadcast_to(cnt, (SUBLANES, LANES)).astype(jnp.int32)


def _mix_and_route(outs, lses, yf, yb, xc, z, x2, mod, d_skip, g_ssd, w_out, g_post, g_pre_ffn,
                   w_router, b_router, *, seq, tm=256):
    t_total = x2.shape[0]
    nt = t_total // tm
    tiles_per_seq = seq // tm
    expand = (jnp.arange(LANES)[:, None] == jnp.arange(ATTN_WIDTH)[None, :] // HEAD_DIM).astype(BF16)
    wr = jnp.zeros((D_MODEL, LANES), F32).at[:, :N_EXPERTS].set(w_router)
    br = jnp.full((1, LANES), _NEG, F32).at[0, :N_EXPERTS].set(b_router)
    row = lambda i: (i, 0)
    wide = pl.BlockSpec((tm, D_MODEL), row)
    narrow = pl.BlockSpec((tm, LANES), row)
    vec = _const_spec((1, D_MODEL))
    bsz = t_total // seq

    def dilated(a, dil):
        a = a.reshape(bsz, dil, seq // dil, a.shape[-1])
        spec = pl.BlockSpec((1, dil, tm // dil, a.shape[-1]),
                            lambda i: (i // tiles_per_seq, 0, i % tiles_per_seq, 0))
        return a, spec

    o_args, o_specs = [outs[0]], [wide]
    l_args, l_specs = [lses[0]], [narrow]
    for dil, o, l in zip(DILATIONS[1:], outs[1:], lses[1:]):
        a, spec = dilated(o, dil)
        o_args.append(a)
        o_specs.append(spec)
        a, spec = dilated(l, dil)
        l_args.append(a)
        l_specs.append(spec)
    outs, lses = o_args, l_args
    return pl.pallas_call(
        functools.partial(_mix_kernel, tm=tm),
        grid=(nt,),
        in_specs=o_specs + l_specs + [wide] * 5 + [
            pl.BlockSpec((1, N_MOD, D_MODEL), lambda i: (i // tiles_per_seq, 0, 0)),
            _const_spec((LANES, ATTN_WIDTH)), vec, vec,
            _const_spec((ATTN_WIDTH + SSD_WIDTH, D_MODEL)), vec, vec,
            _const_spec((D_MODEL, LANES)), _const_spec((1, LANES))],
        out_specs=[wide, wide, narrow, pl.BlockSpec((1, SUBLANES, LANES), lambda i: (i, 0, 0))],
        out_shape=[jax.ShapeDtypeStruct((t_total, D_MODEL), F32),
                   jax.ShapeDtypeStruct((t_total, D_MODEL), BF16),
                   jax.ShapeDtypeStruct((t_total, LANES), F32),
                   jax.ShapeDtypeStruct((nt, SUBLANES, LANES), jnp.int32)],
        scratch_shapes=[pltpu.VMEM((len(DILATIONS) - 1, tm, LANES), F32),
                        pltpu.VMEM((ATTN_WIDTH // LANES, tm, LANES), F32)],
        compiler_params=_cparams("parallel"),
        name="mix_route",
    )(*outs, *lses, yf, yb, xc, z, x2, mod, expand, jnp.repeat(d_skip, SSD_HEAD_DIM).reshape(1, -1),
      g_ssd.reshape(1, -1), w_out.astype(BF16), g_post.reshape(1, -1), g_pre_ffn.reshape(1, -1), wr, br)


_MOE_TM = 256
_ROUTE_TM = 256
_MOE_CAP = 32
_BF16_ROWS = 2 * SUBLANES
_MOE_WIN = _MOE_CAP + _BF16_ROWS
_MOE_TAIL = _MOE_TM
_MOE_SLOTS = N_EXPERTS + _ROUTE_TM * TOP_K // _MOE_CAP
_MOE_KBLOCK = 16
_ROW_SPLIT = D_MODEL // LANES


def _store_rows(ref, lead, row0, vals):
    n = vals.shape[0]
    for q in range(_ROW_SPLIT):
        ref[lead, pl.ds(row0 * _ROW_SPLIT + q, n, stride=_ROW_SPLIT), :] = vals[:, q * LANES:(q + 1) * LANES]


def _load_rows(ref, lead, row0, n):
    parts = [ref[lead, pl.ds(row0 * _ROW_SPLIT + q, n, stride=_ROW_SPLIT), :] for q in range(_ROW_SPLIT)]
    return jnp.concatenate(parts, axis=1)


def _sorted_rows(t_total):
    return t_total * TOP_K + N_EXPERTS * _MOE_CAP + _MOE_TAIL


def _dispatch_kernel(src_ref, dst_ref, num_ref, pad_ref, route_ref, h_ref, xs_ref, stage_ref, sem, *, nt):
    i = pl.program_id(0)
    slot = i % 2
    cap = _MOE_CAP
    n_pairs = _ROUTE_TM * TOP_K
    chunk_rows = cap * _ROW_SPLIT

    def chunk_copy(step, j, s):
        src = pl.multiple_of(src_ref[step * _MOE_SLOTS + j] * _ROW_SPLIT, _ROW_SPLIT)
        dst = pl.multiple_of(dst_ref[step * _MOE_SLOTS + j] * _ROW_SPLIT, _ROW_SPLIT)
        return pltpu.make_async_copy(stage_ref.at[s, pl.ds(src, chunk_rows)], xs_ref.at[pl.ds(dst, chunk_rows)],
                                     sem.at[s])

    @pl.when(i == 0)
    def _():
        for s in range(2):
            stage_ref[s, n_pairs * _ROW_SPLIT:, :] = jnp.zeros((chunk_rows, LANES), F32)

    @pl.when(i < nt)
    def _():
        pos_t = route_ref[...].T[TOP_K:2 * TOP_K, :]
        hb = h_ref[...]
        blk = _ROUTE_TM
        for b in range(n_pairs // blk):
            jrow = (lax.broadcasted_iota(jnp.int32, (blk, _ROUTE_TM), 0) + b * blk).astype(F32)
            onehot = jnp.zeros((blk, _ROUTE_TM), F32)
            for kk in range(TOP_K):
                onehot = onehot + jnp.where(pos_t[kk:kk + 1, :] == jrow, 1.0, 0.0)
            _store_rows(stage_ref, slot, b * blk, _dot(onehot.astype(BF16), hb))

    @pl.when(i > 0)
    def _():
        def wait_prev(j, carry):
            chunk_copy(i - 1, j, 1 - slot).wait()
            return carry
        lax.fori_loop(0, num_ref[i - 1], wait_prev, 0)

    def start(j, carry):
        chunk_copy(i, j, slot).start()
        return carry

    lax.fori_loop(0, num_ref[i], start, 0)

    @pl.when(i == nt)
    def _():
        n_rows = xs_ref.shape[0] // _ROW_SPLIT
        zeros = stage_ref.at[slot, pl.ds(0, chunk_rows)]
        zeros[...] = jnp.zeros((chunk_rows, LANES), F32)
        fills = [pad_ref[e] for e in range(N_EXPERTS)]
        fills += [min(n_rows - _MOE_TAIL + j * cap, n_rows - cap) for j in range(-(-_MOE_TAIL // cap))]
        copies = [pltpu.make_async_copy(zeros, xs_ref.at[pl.ds(pl.multiple_of(r * _ROW_SPLIT, _ROW_SPLIT),
                                                               chunk_rows)], sem.at[slot]) for r in fills]
        for cp in copies:
            cp.start()
        for cp in copies:
            cp.wait()


def _dispatch(h2, route, lists):
    t_total = h2.shape[0]
    nt = t_total // _ROUTE_TM
    n_rows = _sorted_rows(t_total)
    tile = lambda i, *_: (jnp.minimum(i, nt - 1), 0)
    grid_spec = pltpu.PrefetchScalarGridSpec(
        num_scalar_prefetch=4,
        grid=(nt + 1,),
        in_specs=[pl.BlockSpec((_ROUTE_TM, LANES), tile),
                  pl.BlockSpec((_ROUTE_TM, D_MODEL), tile)],
        out_specs=pl.BlockSpec(memory_space=pl.ANY),
        scratch_shapes=[pltpu.VMEM((2, (_ROUTE_TM * TOP_K + _MOE_CAP) * _ROW_SPLIT, LANES), F32),
                        pltpu.SemaphoreType.DMA((2,))])
    return pl.pallas_call(
        functools.partial(_dispatch_kernel, nt=nt),
        grid_spec=grid_spec,
        out_shape=jax.ShapeDtypeStruct((n_rows * _ROW_SPLIT, LANES), F32),
        compiler_params=_cparams("arbitrary"),
        name="moe_dispatch",
    )(lists["stage_row"], lists["sorted_row"], lists["num"], lists["pad_row"], route, h2)


def _gmm_kernel(tile_ref, exp_ref, valid_ref, offs_ref, x_ref, wgu_ref, bgu_ref, wdn_ref, bdn_ref, y_ref,
                wgu_bf, wdn_bf):
    j = pl.program_id(0)
    e = exp_ref[j]
    tile = tile_ref[j]
    prev_j = jnp.maximum(j - 1, 0)
    new_expert = (j == 0) | (exp_ref[prev_j] != e)
    new_tile = (j == 0) | (tile_ref[prev_j] != tile)

    @pl.when(new_expert)
    def _():
        wgu_bf[...] = wgu_ref[0].astype(BF16)
        wdn_bf[...] = wdn_ref[0].astype(BF16)

    @pl.when(valid_ref[j] != 0)
    def _():
        half = D_MODEL // 2
        gu = bgu_ref[0]
        for kc in range(2):
            parts = [x_ref[0, pl.ds(q, _MOE_TM, stride=_ROW_SPLIT), :]
                     for q in range(kc * _ROW_SPLIT // 2, (kc + 1) * _ROW_SPLIT // 2)]
            xk = jnp.concatenate(parts, axis=1).astype(BF16)
            gu = gu + _dot(xk, wgu_bf[kc * half:(kc + 1) * half, :])
        gate = jnp.minimum(gu[:, :EXPERT_FF], SWIGLU_LIMIT)
        up = jnp.clip(gu[:, EXPERT_FF:], -SWIGLU_LIMIT, SWIGLU_LIMIT)
        act = (up + 1.0) * gate * _sigmoid(SWIGLU_ALPHA * gate)
        y = (_dot(act.astype(BF16), wdn_bf[...]) + bdn_ref[0]).astype(BF16)
        rows = tile * _MOE_TM + lax.broadcasted_iota(jnp.int32, (_MOE_TM, 1), 0)
        mine = (rows >= offs_ref[e]) & (rows < offs_ref[e + 1])

        @pl.when(new_tile)
        def _():
            y_ref[...] = jnp.where(mine, y, jnp.zeros_like(y))

        @pl.when(jnp.logical_not(new_tile))
        def _():
            y_ref[...] = jnp.where(mine, y, y_ref[...])


def _grouped_mlp(xs, offs, w_gate_up, b_gate_up, w_down, b_down):
    n_tiles = xs.shape[0] // (_MOE_TM * _ROW_SPLIT)
    n_visits = n_tiles + N_EXPERTS - 1
    first = offs[:-1] // _MOE_TM
    last = (offs[1:] - 1) // _MOE_TM
    per_expert = last - first + 1
    vstart = jnp.concatenate([jnp.zeros((1,), jnp.int32), jnp.cumsum(per_expert)]).astype(jnp.int32)
    total = vstart[-1]
    vis = jnp.minimum(jnp.arange(n_visits, dtype=jnp.int32), total - 1)
    exp_ids = (jnp.sum(vstart[None, :] <= vis[:, None], axis=1) - 1).astype(jnp.int32)
    mine = exp_ids[:, None] == jnp.arange(N_EXPERTS, dtype=jnp.int32)[None, :]
    pick = lambda a: jnp.sum(jnp.where(mine, a[None, :], 0), axis=1)
    tile_ids = (pick(first) + vis - pick(vstart[:-1])).astype(jnp.int32)
    valid = (jnp.arange(n_visits) < total).astype(jnp.int32)
    grid_spec = pltpu.PrefetchScalarGridSpec(
        num_scalar_prefetch=4,
        grid=(n_visits,),
        in_specs=[pl.BlockSpec((1, _MOE_TM * _ROW_SPLIT, LANES), lambda j, t, e, v, o: (t[j], 0, 0)),
                  pl.BlockSpec((1, D_MODEL, 2 * EXPERT_FF), lambda j, t, e, v, o: (e[j], 0, 0)),
                  pl.BlockSpec((1, 1, 2 * EXPERT_FF), lambda j, t, e, v, o: (e[j], 0, 0)),
                  pl.BlockSpec((1, EXPERT_FF, D_MODEL), lambda j, t, e, v, o: (e[j], 0, 0)),
                  pl.BlockSpec((1, 1, D_MODEL), lambda j, t, e, v, o: (e[j], 0, 0))],
        out_specs=pl.BlockSpec((_MOE_TM, D_MODEL), lambda j, t, e, v, o: (t[j], 0)),
        scratch_shapes=[pltpu.VMEM((D_MODEL, 2 * EXPERT_FF), BF16), pltpu.VMEM((EXPERT_FF, D_MODEL), BF16)])
    tiled = (n_tiles, _MOE_TM * _ROW_SPLIT, LANES)
    return pl.pallas_call(
        _gmm_kernel,
        grid_spec=grid_spec,
        out_shape=jax.ShapeDtypeStruct((n_tiles * _MOE_TM, D_MODEL), BF16),
        compiler_params=_cparams("arbitrary"),
        name="moe_mlp",
    )(tile_ids, exp_ids, valid, offs, xs.reshape(tiled), w_gate_up, b_gate_up.reshape(N_EXPERTS, 1, -1),
      w_down, b_down.reshape(N_EXPERTS, 1, -1))


def _combine_kernel(win_ref, num_ref, ys_ref, route_ref, base_ref, res_ref, mod_ref, g_ref,
                    o_ref, ybuf_ref, acc_ref, sem, *, nt):
    i = pl.program_id(0)
    slot = i % 2
    cap = _MOE_CAP
    win = _MOE_WIN
    tm = _ROUTE_TM
    kcols = _MOE_KBLOCK * win

    def window_copy(step, j, s):
        src = pl.multiple_of(win_ref[step * _MOE_SLOTS + j], _BF16_ROWS)
        return pltpu.make_async_copy(ys_ref.at[pl.ds(src, win)],
                                     ybuf_ref.at[s, pl.ds(pl.multiple_of(j * win, _BF16_ROWS), win)], sem.at[s])

    def start_all(step, s):
        def body(j, carry):
            window_copy(step, j, s).start()
            return carry
        lax.fori_loop(0, num_ref[step], body, 0)

    @pl.when(i == 0)
    def _():
        ybuf_ref[...] = jnp.zeros_like(ybuf_ref)
        start_all(0, 0)

    @pl.when(i + 1 < nt)
    def _():
        start_all(i + 1, 1 - slot)

    def wait_own(j, carry):
        window_copy(i, j, slot).wait()
        return carry

    lax.fori_loop(0, num_ref[i], wait_own, 0)

    route = route_ref[...]
    lane = lax.broadcasted_iota(jnp.int32, (tm, LANES), 1).astype(F32)
    base = base_ref[0, 0:1, :]
    cols, wts = [], []
    for kk in range(TOP_K):
        expert = route[:, kk:kk + 1]
        rank = route[:, 2 * TOP_K + kk:2 * TOP_K + kk + 1]
        chunk = jnp.floor(rank * (1.0 / cap))
        first = jnp.sum(jnp.where(lane == expert, base, 0.0), axis=-1, keepdims=True)
        cols.append(first + chunk * win + (rank - chunk * cap))
        wts.append(route[:, 3 * TOP_K + kk:3 * TOP_K + kk + 1])

    for b in range(_MOE_SLOTS // _MOE_KBLOCK):
        @pl.when(b * _MOE_KBLOCK < num_ref[i])
        def _():
            col = (lax.broadcasted_iota(jnp.int32, (tm, kcols), 1) + b * kcols).astype(F32)
            pw = jnp.zeros((tm, kcols), F32)
            for kk in range(TOP_K):
                pw = pw + jnp.where(col == cols[kk], wts[kk], 0.0)
            part = _dot(pw.astype(BF16), ybuf_ref[slot, b * kcols:(b + 1) * kcols, :])
            if b == 0:
                acc_ref[...] = part
            else:
                acc_ref[...] += part

    ffn = acc_ref[...] * (1.0 / (wts[0] + wts[1] + wts[2] + wts[3]))
    gate_f = mod_ref[0, 5:6, :]
    o_ref[...] = res_ref[...] + gate_f * (_rms(ffn) * g_ref[...])


def _combine(ys, route, lists, res, mod, g_post_ffn, *, seq):
    t_total = res.shape[0]
    nt = t_total // _ROUTE_TM
    tiles_per_seq = seq // _ROUTE_TM
    row = lambda i, *_: (i, 0)
    grid_spec = pltpu.PrefetchScalarGridSpec(
        num_scalar_prefetch=2,
        grid=(nt,),
        in_specs=[pl.BlockSpec(memory_space=pl.ANY),
                  pl.BlockSpec((_ROUTE_TM, LANES), row),
                  pl.BlockSpec((1, SUBLANES, LANES), lambda i, *_: (i, 0, 0)),
                  pl.BlockSpec((_ROUTE_TM, D_MODEL), row),
                  pl.BlockSpec((1, N_MOD, D_MODEL), lambda i, *_: (i // tiles_per_seq, 0, 0)),
                  pl.BlockSpec((1, D_MODEL), lambda i, *_: (0, 0))],
        out_specs=pl.BlockSpec((_ROUTE_TM, D_MODEL), row),
        scratch_shapes=[pltpu.VMEM((2, _MOE_SLOTS * _MOE_WIN, D_MODEL), BF16),
                        pltpu.VMEM((_ROUTE_TM, D_MODEL), F32),
                        pltpu.SemaphoreType.DMA((2,))])
    return pl.pallas_call(
        functools.partial(_combine_kernel, nt=nt),
        grid_spec=grid_spec,
        out_shape=jax.ShapeDtypeStruct((t_total, D_MODEL), F32),
        compiler_params=_cparams("arbitrary"),
        name="moe_combine",
    )(lists["window_row"], lists["num"], ys, route, lists["window_base"], res, mod, g_post_ffn.reshape(1, -1))


def _chunk_lists(n_te, offs):
    nt = n_te.shape[0]
    cap = _MOE_CAP
    zero_col = jnp.zeros((nt, 1), jnp.int32)
    zero_row = jnp.zeros((1, N_EXPERTS), jnp.int32)
    before = jnp.concatenate([zero_row, jnp.cumsum(n_te, axis=0)[:-1]], axis=0)
    seg = offs[None, :N_EXPERTS] + before
    first = jnp.concatenate([zero_col, jnp.cumsum(n_te, axis=1)[:, :-1]], axis=1)
    chunk_base = jnp.concatenate([zero_col, jnp.cumsum((n_te + cap - 1) // cap, axis=1)], axis=1)
    slots = jnp.arange(_MOE_SLOTS, dtype=jnp.int32)
    expert = jnp.minimum(jnp.sum(chunk_base[:, 1:, None] <= slots[None, None, :], axis=1), N_EXPERTS - 1)
    pick = expert[:, None, :] == jnp.arange(N_EXPERTS, dtype=jnp.int32)[None, :, None]
    take = lambda a: jnp.sum(jnp.where(pick, a[:, :, None], 0), axis=1)
    chunk = slots[None, :] - take(chunk_base[:, :N_EXPERTS])
    sorted_row = take(seg) + chunk * cap
    num = jnp.concatenate([chunk_base[:, -1], jnp.zeros((1,), jnp.int32)])
    window_base = chunk_base[:, :N_EXPERTS] * _MOE_WIN + seg % _BF16_ROWS
    window_base = jnp.zeros((nt, SUBLANES, LANES), F32).at[:, :, :N_EXPERTS].set(
        window_base.astype(F32)[:, None, :])
    as_i32 = lambda a: a.astype(jnp.int32).reshape(-1)
    return {"stage_row": as_i32(take(first) + chunk * cap),
            "sorted_row": as_i32(sorted_row),
            "window_row": as_i32(sorted_row // _BF16_ROWS * _BF16_ROWS),
            "num": as_i32(num),
            "pad_row": as_i32(offs[1:] - cap).at[N_EXPERTS - 1].add(-_MOE_TAIL),
            "window_base": window_base}


def _forward(x, c, w_ada, b_ada, g_pre_mix, g_post_mix, w_in, conv_w, conv_b, dt_bias, a_log, d_skip,
             g_ssd_norm, w_out, g_pre_ffn, g_post_ffn, w_router, b_router, w_gate_up, b_gate_up, w_down, b_down):
    bsz, seq, _ = x.shape
    x2 = x.astype(F32).reshape(bsz * seq, D_MODEL)
    mod = _ada_mod(c.astype(F32), w_ada, b_ada)
    *qkv, z, xc, dt = _in_proj(x2, mod, g_pre_mix, w_in, conv_w, conv_b, dt_bias, seq=seq)
    attn = []
    for gi, dil in enumerate(DILATIONS):
        q, k, v = (a.reshape(bsz * seq, ATTN_WIDTH) for a in qkv[3 * gi:3 * gi + 3])
        attn.append(_dilated_attention(q, k, v, dil, seq=seq))
    yf, yb = _ssd(xc, dt, a_log, seq=seq)
    res, h2, route, cnt = _mix_and_route(
        [o for o, _ in attn], [l for _, l in attn], yf, yb, xc, z, x2, mod, d_skip, g_ssd_norm, w_out,
        g_post_mix, g_pre_ffn, w_router, b_router, seq=seq, tm=_ROUTE_TM)
    n_te = cnt[:, 0, :N_EXPERTS]
    region = jnp.sum(n_te, axis=0) + _MOE_CAP
    region = region.at[N_EXPERTS - 1].add(_MOE_TAIL)
    offs = jnp.concatenate([jnp.zeros((1,), jnp.int32), jnp.cumsum(region)]).astype(jnp.int32)
    lists = _chunk_lists(n_te, offs)
    xs = _dispatch(h2, route, lists)
    ys = _grouped_mlp(xs, offs, w_gate_up, b_gate_up, w_down, b_down)
    out = _combine(ys, route, lists, res, mod, g_post_ffn, seq=seq)
    return out.reshape(bsz, seq, D_MODEL)


def kernel(x, c, w_ada, b_ada, g_pre_mix, g_post_mix, w_in, conv_w, conv_b, dt_bias, a_log, d_skip, g_ssd_norm, w_out, g_pre_ffn, g_post_ffn, w_router, b_router, w_gate_up, b_gate_up, w_down, b_down):
    layer = lambda t: t[0].astype(F32)
    out = _forward(x, c, layer(w_ada), layer(b_ada), layer(g_pre_mix), layer(g_post_mix), layer(w_in),
                   layer(conv_w), layer(conv_b), layer(dt_bias), layer(a_log), layer(d_skip),
                   layer(g_ssd_norm), layer(w_out), layer(g_pre_ffn), layer(g_post_ffn), layer(w_router),
                   layer(b_router), layer(w_gate_up), layer(b_gate_up), layer(w_down), layer(b_down))
    return out.astype(x.dtype)
```

```python
import functools

import numpy as np
import jax
import jax.numpy as jnp
from jax import lax
from jax.experimental import pallas as pl
from jax.experimental.pallas import tpu as pltpu

F32 = jnp.float32
BF16 = jnp.bfloat16

D_MODEL = 1024
ATTN_HEADS = 16
HEAD_DIM = 64
ATTN_WIDTH = ATTN_HEADS * HEAD_DIM
DILATIONS = (1, 4, 16)
ATTN_HALF = 64
SSD_HEADS = 16
SSD_HEAD_DIM = 64
SSD_WIDTH = SSD_HEADS * SSD_HEAD_DIM
SSD_GROUPS = 2
SSD_STATE = 128
SSD_CHUNK = 128
CONV_WIDTH = 5
XBC_WIDTH = SSD_WIDTH + 2 * SSD_GROUPS * SSD_STATE
N_EXPERTS = 32
TOP_K = 4
EXPERT_FF = 1024
SWIGLU_ALPHA = 1.702
SWIGLU_LIMIT = 7.0
RMS_EPS = 1e-6
N_MOD = 6
LANES = 128
SUBLANES = 8
VMEM_LIMIT = 56 * 1024 * 1024


def _cparams(*sem):
    return pltpu.CompilerParams(dimension_semantics=sem, vmem_limit_bytes=VMEM_LIMIT)


def _const_spec(shape):
    nd = len(shape)
    return pl.BlockSpec(shape, lambda *_: (0,) * nd)


def _split2(a):
    hi = a.astype(BF16)
    lo = (a - hi.astype(F32)).astype(BF16)
    return hi, lo


def _dot(a, b):
    return jnp.dot(a, b, preferred_element_type=F32)


def _dot_f32(a, b):
    ah, al = _split2(a)
    bh, bl = _split2(b)
    return _dot(ah, bh) + (_dot(ah, bl) + _dot(al, bh))


def _sigmoid(x):
    return 1.0 / (1.0 + jnp.exp(-x))


def _rms(x):
    return x * lax.rsqrt(jnp.mean(x * x, axis=-1, keepdims=True) + RMS_EPS)


def _ada_kernel(c_ref, w_ref, b_ref, o_ref):
    c = c_ref[...]
    o_ref[...] = _dot_f32(c * _sigmoid(c), w_ref[...]) + b_ref[...]


def _ada_mod(c, w_ada, b_ada):
    bsz = c.shape[0]
    c8 = jnp.zeros((SUBLANES, D_MODEL), F32).at[:bsz].set(c)
    out = pl.pallas_call(
        _ada_kernel,
        grid=(N_MOD,),
        in_specs=[_const_spec((SUBLANES, D_MODEL)),
                  pl.BlockSpec((D_MODEL, D_MODEL), lambda j: (0, j)),
                  pl.BlockSpec((1, D_MODEL), lambda j: (0, j))],
        out_specs=pl.BlockSpec((SUBLANES, D_MODEL), lambda j: (0, j)),
        out_shape=jax.ShapeDtypeStruct((SUBLANES, N_MOD * D_MODEL), F32),
        compiler_params=_cparams("parallel"),
        name="ada_mod",
    )(c8, w_ada, b_ada.reshape(1, -1))
    return out[:bsz].reshape(bsz, N_MOD, D_MODEL)


def _inproj_kernel(x_ref, xp_ref, xn_ref, mod_ref, g_ref, wqkv_ref, wz_ref, wxbc_ref, wdt_ref,
                   cw_ref, cb_ref, dtb_ref,
                   q1_ref, k1_ref, v1_ref, q4_ref, k4_ref, v4_ref, q16_ref, k16_ref, v16_ref,
                   z_ref, xc_ref, dt_ref, buf_ref, slab_ref, *, tm, tiles_per_seq):
    i = pl.program_id(0)
    shift = mod_ref[0, 0:1, :]
    scale = mod_ref[0, 1:2, :]
    g = g_ref[...]

    def norm_mod(x):
        return _rms(x) * g * (1.0 + scale) + shift

    h = norm_mod(x_ref[...])
    hb = h.astype(BF16)
    n_slab = ATTN_WIDTH // LANES
    outs = ((q1_ref, q4_ref, q16_ref), (k1_ref, k4_ref, k16_ref), (v1_ref, v4_ref, v16_ref))
    for c, (nat_ref, d4_ref, d16_ref) in enumerate(outs):
        r = _dot(hb, wqkv_ref[:, c * ATTN_WIDTH:(c + 1) * ATTN_WIDTH])
        if c == 0:
            r = r * HEAD_DIM ** -0.5
        nat_ref[...] = r.astype(BF16)
        for s in range(n_slab):
            slab_ref[0, s] = r[:, s * LANES:(s + 1) * LANES]
        step = DILATIONS[1]
        sub = tm // step
        for r4 in range(step):
            for s in range(n_slab):
                rows = slab_ref[0, s, pl.ds(r4, sub, stride=step), :]
                d4_ref[0, r4, :, s * LANES:(s + 1) * LANES] = rows.astype(BF16)
                slab_ref[1, s, r4 * sub:(r4 + 1) * sub, :] = rows
        for r4 in range(step):
            for s4 in range(step):
                for s in range(n_slab):
                    rows = slab_ref[1, s, pl.ds(r4 * sub + s4, sub // step, stride=step), :]
                    d16_ref[0, step * s4 + r4, :, s * LANES:(s + 1) * LANES] = rows.astype(BF16)
    z_ref[...] = _dot(hb, wz_ref[...]).astype(BF16)
    dt_ref[...] = jax.nn.softplus(_dot_f32(h, wdt_ref[...]) + dtb_ref[...])

    hh = norm_mod(jnp.concatenate([xp_ref[...], xn_ref[...]], axis=0)).astype(BF16)
    halo = _dot(hh, wxbc_ref[...])
    t_in_seq = i % tiles_per_seq
    prev_ok = (t_in_seq != 0).astype(F32)
    next_ok = (t_in_seq != tiles_per_seq - 1).astype(F32)
    buf_ref[0:SUBLANES, :] = halo[0:SUBLANES] * prev_ok
    buf_ref[SUBLANES + tm:, :] = halo[SUBLANES:] * next_ok
    buf_ref[SUBLANES:SUBLANES + tm, :] = _dot(hb, wxbc_ref[...])
    acc = cb_ref[...]
    for j in range(CONV_WIDTH):
        off = SUBLANES - CONV_WIDTH // 2 + j
        acc = acc + buf_ref[off:off + tm, :] * cw_ref[j:j + 1, :]
    xc_ref[...] = (acc * _sigmoid(acc)).astype(BF16)


def _in_proj(x2, mod, g_pre, w_in, conv_w, conv_b, dt_bias, *, seq, tm=512):
    t_total = x2.shape[0]
    nt = t_total // tm
    tiles_per_seq = seq // tm
    hb = tm // SUBLANES
    n_hblk = t_total // SUBLANES
    qkv_w = 3 * ATTN_WIDTH
    w_qkv = w_in[:, :qkv_w].astype(BF16)
    w_z = w_in[:, qkv_w:qkv_w + SSD_WIDTH].astype(BF16)
    w_xbc = w_in[:, qkv_w + SSD_WIDTH:qkv_w + SSD_WIDTH + XBC_WIDTH].astype(BF16)
    w_dt_raw = w_in[:, qkv_w + SSD_WIDTH + XBC_WIDTH:]
    w_dt = jnp.zeros((D_MODEL, 2 * LANES), F32)
    dtb = jnp.zeros((1, 2 * LANES), F32)
    for dr in range(2):
        w_dt = w_dt.at[:, dr * LANES:dr * LANES + SSD_HEADS].set(w_dt_raw[:, dr * SSD_HEADS:(dr + 1) * SSD_HEADS])
        dtb = dtb.at[0, dr * LANES:dr * LANES + SSD_HEADS].set(dt_bias[dr])
    row = lambda i: (i, 0)
    bsz = t_total // seq
    dil_shapes, dil_specs = [], []
    for dil in DILATIONS[1:]:
        dil_shapes += [jax.ShapeDtypeStruct((bsz, dil, seq // dil, ATTN_WIDTH), BF16)] * 3
        dil_specs += [pl.BlockSpec((1, dil, tm // dil, ATTN_WIDTH),
                                   lambda i: (i // tiles_per_seq, 0, i % tiles_per_seq, 0))] * 3
    out_shape = [jax.ShapeDtypeStruct((t_total, ATTN_WIDTH), BF16)] * 3 + dil_shapes + [
        jax.ShapeDtypeStruct((t_total, SSD_WIDTH), BF16),
        jax.ShapeDtypeStruct((t_total, XBC_WIDTH), BF16),
        jax.ShapeDtypeStruct((t_total, 2 * LANES), F32)]
    return pl.pallas_call(
        functools.partial(_inproj_kernel, tm=tm, tiles_per_seq=tiles_per_seq),
        grid=(nt,),
        in_specs=[pl.BlockSpec((tm, D_MODEL), row),
                  pl.BlockSpec((SUBLANES, D_MODEL), lambda i: (jnp.maximum(i * hb - 1, 0), 0)),
                  pl.BlockSpec((SUBLANES, D_MODEL), lambda i: (jnp.minimum((i + 1) * hb, n_hblk - 1), 0)),
                  pl.BlockSpec((1, N_MOD, D_MODEL), lambda i: (i // tiles_per_seq, 0, 0)),
                  _const_spec((1, D_MODEL)),
                  _const_spec((D_MODEL, qkv_w)),
                  _const_spec((D_MODEL, SSD_WIDTH)),
                  _const_spec((D_MODEL, XBC_WIDTH)),
                  _const_spec((D_MODEL, 2 * LANES)),
                  _const_spec((CONV_WIDTH, XBC_WIDTH)),
                  _const_spec((1, XBC_WIDTH)),
                  _const_spec((1, 2 * LANES))],
        out_specs=[pl.BlockSpec((tm, ATTN_WIDTH), row)] * 3 + dil_specs + [
            pl.BlockSpec((tm, SSD_WIDTH), row),
            pl.BlockSpec((tm, XBC_WIDTH), row),
            pl.BlockSpec((tm, 2 * LANES), row)],
        out_shape=out_shape,
        scratch_shapes=[pltpu.VMEM((tm + 2 * SUBLANES, XBC_WIDTH), F32),
                        pltpu.VMEM((2, ATTN_WIDTH // LANES, tm, LANES), F32)],
        compiler_params=_cparams("parallel"),
        name="in_proj",
    )(x2, x2, x2, mod, g_pre.reshape(1, -1), w_qkv, w_z, w_xbc, w_dt, conv_w, conv_b.reshape(1, -1), dtb)


_NEG = -1e30


def _ssd_direction(xc_ref, dt_ref, a_ref, e_ref, h_ref, y_ref, reverse):
    qn = SSD_CHUNK
    gw = SSD_WIDTH // SSD_GROUPS
    li = lax.broadcasted_iota(jnp.int32, (qn, qn), 0)
    si = lax.broadcasted_iota(jnp.int32, (qn, qn), 1)
    mask = (si >= li) if reverse else (si <= li)
    tri = jnp.where(mask, 1.0, 0.0).astype(BF16)
    dt = dt_ref[...]
    adt = dt * a_ref[...]
    p0 = adt.astype(BF16)
    r0 = adt - p0.astype(F32)
    p1 = r0.astype(BF16)
    p2 = (r0 - p1.astype(F32)).astype(BF16)
    acum = _dot(tri, p0) + (_dot(tri, p1) + _dot(tri, p2))
    last = 0 if reverse else qn - 1
    eo = jnp.exp(acum)
    ds = jnp.exp(acum[last:last + 1, :] - acum)
    dt_h, dt_l = _split2(dt)
    eo_h, eo_l = _split2(eo)
    ex = _dot(jnp.concatenate([dt_h, eo_h, ds.astype(BF16)], axis=0), e_ref[...])
    ex_l = _dot(jnp.concatenate([dt_l, eo_l], axis=0), e_ref[...])
    dt_x = ex[0:qn] + ex_l[0:qn]
    eo_x = ex[qn:2 * qn] + ex_l[qn:]
    ds_x = ex[2 * qn:]
    xd = xc_ref[:, 0:SSD_WIDTH].astype(F32) * dt_x
    xdb = xd.astype(BF16)
    xwb = (xd * ds_x).astype(BF16)
    acum_t = acum.T
    lane = lax.broadcasted_iota(jnp.int32, (qn, LANES), 1)
    for g in range(SSD_GROUPS):
        b0 = SSD_WIDTH + g * SSD_STATE
        c0 = SSD_WIDTH + SSD_GROUPS * SSD_STATE + g * SSD_STATE
        bm = xc_ref[:, b0:b0 + SSD_STATE]
        cm = xc_ref[:, c0:c0 + SSD_STATE]
        cb = lax.dot_general(cm, bm, (((1,), (1,)), ((), ())), preferred_element_type=F32)
        hg = h_ref[g]
        yoff = _dot(cm, hg.astype(BF16))
        st = lax.dot_general(bm, xwb[:, g * gw:(g + 1) * gw], (((0,), (0,)), ((), ())),
                             preferred_element_type=F32)
        h_ref[g] = hg * eo_x[last:last + 1, g * gw:(g + 1) * gw] + st
        for pr in range(gw // LANES):
            col = g * gw + pr * LANES
            xp = xdb[:, col:col + LANES]
            ms = []
            for e in (col // SSD_HEAD_DIM, col // SSD_HEAD_DIM + 1):
                seg = acum[:, e:e + 1] - acum_t[e:e + 1, :]
                ms.append((cb * jnp.exp(jnp.where(mask, seg, _NEG))).astype(BF16))
            zero = jnp.zeros_like(xp)
            xp2 = jnp.concatenate([jnp.where(lane < SSD_HEAD_DIM, xp, zero),
                                   jnp.where(lane < SSD_HEAD_DIM, zero, xp)], axis=0)
            yd = _dot(jnp.concatenate(ms, axis=1), xp2)
            y = yd + yoff[:, pr * LANES:(pr + 1) * LANES] * eo_x[:, col:col + LANES]
            y_ref[:, col:col + LANES] = y.astype(BF16)


_ATTN_QB = 2 * ATTN_HALF
_ATTN_WIN = _ATTN_QB + 2 * ATTN_HALF
_ATTN_LQ = 512


def _attn_kernel(*refs, lq, sub_len, with_ssd):
    if with_ssd:
        (q_ref, k_ref, kp_ref, kn_ref, v_ref, vp_ref, vn_ref, bias_ref, xcf_ref, xcb_ref, dtf_ref, dtb_ref,
         a_ref, e_ref, o_ref, lse_ref, yf_ref, yb_ref, kw_ref, vw_ref, hf_ref, hb_ref) = refs
    else:
        q_ref, k_ref, kp_ref, kn_ref, v_ref, vp_ref, vn_ref, bias_ref, o_ref, lse_ref, kw_ref, vw_ref = refs
    t = pl.program_id(2)
    if with_ssd:
        @pl.when(t == 0)
        def _():
            hf_ref[...] = jnp.zeros_like(hf_ref)
            hb_ref[...] = jnp.zeros_like(hb_ref)
    hb = ATTN_HALF
    for src, halo_p, halo_n, win in ((k_ref, kp_ref, kn_ref, kw_ref), (v_ref, vp_ref, vn_ref, vw_ref)):
        win[0:hb, :] = halo_p[...]
        win[hb:hb + lq, :] = src[...]
        win[hb + lq:, :] = halo_n[...]
    qn = _ATTN_QB
    lane = lax.broadcasted_iota(jnp.int32, (qn, LANES), 1)
    first_head = lane < HEAD_DIM
    kpos = lax.broadcasted_iota(jnp.int32, (1, _ATTN_WIN), 1)

    def body(qb, carry):
        r0 = pl.multiple_of(qb * qn, qn)
        kidx = t * lq + r0 - hb + kpos
        in_seq = (kidx >= 0) & (kidx < sub_len)
        lse_tile = jnp.zeros((qn, LANES), F32)
        for hp in range(ATTN_HEADS // 2):
            cs = slice(hp * LANES, (hp + 1) * LANES)
            q = q_ref[pl.ds(r0, qn), cs]
            zero = jnp.zeros_like(q)
            q2 = jnp.concatenate([jnp.where(first_head, q, zero), jnp.where(first_head, zero, q)], axis=0)
            s = lax.dot_general(q2, kw_ref[pl.ds(r0, _ATTN_WIN), cs], (((1,), (1,)), ((), ())),
                                preferred_element_type=F32) + bias_ref[hp]
            s = jnp.where(in_seq, s, _NEG)
            m = jnp.max(s, axis=-1, keepdims=True)
            p = jnp.exp(s - m)
            den = jnp.sum(p, axis=-1, keepdims=True)
            pv = _dot(p.astype(BF16), vw_ref[pl.ds(r0, _ATTN_WIN), cs]) * (1.0 / den)
            o_ref[pl.ds(r0, qn), cs] = jnp.where(first_head, pv[0:qn], pv[qn:]).astype(BF16)
            lse = m + jnp.log(den)
            lse_tile = jnp.where(lane == 2 * hp, lse[0:qn], jnp.where(lane == 2 * hp + 1, lse[qn:], lse_tile))
        lse_ref[pl.ds(r0, qn), :] = lse_tile
        if with_ssd:
            b0 = pl.multiple_of((lq // qn - 1 - qb) * qn, qn)
            _ssd_direction(xcf_ref.at[pl.ds(r0, qn)], dtf_ref.at[pl.ds(r0, qn)], a_ref.at[0], e_ref, hf_ref,
                           yf_ref.at[pl.ds(r0, qn)], False)
            _ssd_direction(xcb_ref.at[pl.ds(b0, qn)], dtb_ref.at[pl.ds(b0, qn)], a_ref.at[1], e_ref, hb_ref,
                           yb_ref.at[pl.ds(b0, qn)], True)
        return carry

    lax.fori_loop(0, lq // qn, body, 0, unroll=2)


def _attn_bias(dilation):
    slopes = jnp.exp2(-8.0 * jnp.arange(1, ATTN_HEADS + 1, dtype=F32) / ATTN_HEADS)
    rel = np.abs(np.arange(_ATTN_WIN)[None, :] - ATTN_HALF - np.arange(_ATTN_QB)[:, None])
    dist = jnp.asarray((rel * dilation).astype(np.float32))
    b = jnp.where(jnp.asarray(rel <= ATTN_HALF), -slopes[:, None, None] * dist, _NEG)
    return b.reshape(ATTN_HEADS // 2, 2 * _ATTN_QB, _ATTN_WIN)


def _dilated_attention(q, k, v, dilation, *, seq, ssd=None):
    t_total = q.shape[0]
    bsz = t_total // seq
    sub_len = seq // dilation
    lq = min(_ATTN_LQ, sub_len)
    nt = sub_len // lq
    hpb = lq // ATTN_HALF
    n_hblk = t_total // ATTN_HALF
    tile = lambda b, r, t: (b * dilation + r) * nt + t
    main = lambda b, r, t: (tile(b, r, t), 0)
    prev = lambda b, r, t: (jnp.maximum(tile(b, r, t) * hpb - 1, 0), 0)
    nxt = lambda b, r, t: (jnp.minimum((tile(b, r, t) + 1) * hpb, n_hblk - 1), 0)
    blk = pl.BlockSpec((lq, ATTN_WIDTH), main)
    hblk_p = pl.BlockSpec((ATTN_HALF, ATTN_WIDTH), prev)
    hblk_n = pl.BlockSpec((ATTN_HALF, ATTN_WIDTH), nxt)
    window = pltpu.VMEM((lq + 2 * ATTN_HALF, ATTN_WIDTH), BF16)
    args = [q, k, k, k, v, v, v, _attn_bias(dilation)]
    in_specs = [blk, blk, hblk_p, hblk_n, blk, hblk_p, hblk_n,
                _const_spec((ATTN_HEADS // 2, 2 * _ATTN_QB, _ATTN_WIN))]
    out_specs = [blk, pl.BlockSpec((lq, LANES), main)]
    out_shape = [jax.ShapeDtypeStruct((t_total, ATTN_WIDTH), BF16), jax.ShapeDtypeStruct((t_total, LANES), F32)]
    scratch = [window, window]
    semantics = ("parallel", "parallel", "parallel")
    if ssd is not None:
        assert dilation == 1 and _ATTN_QB == SSD_CHUNK
        xc, dt, a_log = ssd
        a_rows = jnp.zeros((2, 1, LANES), F32).at[:, 0, :SSD_HEADS].set(-jnp.exp(a_log))
        expand = (jnp.arange(LANES)[:, None] == jnp.arange(SSD_WIDTH)[None, :] // SSD_HEAD_DIM).astype(BF16)
        back = lambda b, r, t: (b * nt + nt - 1 - t, 0)
        args += [xc, xc, dt, dt, a_rows, expand]
        in_specs += [pl.BlockSpec((lq, XBC_WIDTH), main), pl.BlockSpec((lq, XBC_WIDTH), back),
                     pl.BlockSpec((lq, LANES), main),
                     pl.BlockSpec((lq, LANES), lambda b, r, t: (b * nt + nt - 1 - t, 1)),
                     _const_spec((2, 1, LANES)), _const_spec((LANES, SSD_WIDTH))]
        out_specs += [pl.BlockSpec((lq, SSD_WIDTH), main), pl.BlockSpec((lq, SSD_WIDTH), back)]
        out_shape += [jax.ShapeDtypeStruct((t_total, SSD_WIDTH), BF16)] * 2
        state = pltpu.VMEM((SSD_GROUPS, SSD_STATE, SSD_WIDTH // SSD_GROUPS), F32)
        scratch += [state, state]
        semantics = ("parallel", "arbitrary", "arbitrary")
    return pl.pallas_call(
        functools.partial(_attn_kernel, lq=lq, sub_len=sub_len, with_ssd=ssd is not None),
        grid=(bsz, dilation, nt),
        in_specs=in_specs,
        out_specs=out_specs,
        out_shape=out_shape,
        scratch_shapes=scratch,
        compiler_params=_cparams(*semantics),
        name=f"attn_d{dilation}" + ("_ssd" if ssd is not None else ""),
    )(*args)


def _mix_kernel(o0_ref, o1_ref, o2_ref, l0_ref, l1_ref, l2_ref, yf_ref, yb_ref, xs_ref, z_ref, x_ref,
                mod_ref, e_ref, dskip_ref, gssd_ref, wout_ref, gpost_ref, gpre_ref, wr_ref, br_ref,
                res_ref, h2_ref, route_ref, cnt_ref, lnat_ref, onat_ref, *, tm):
    n_slab = ATTN_WIDTH // LANES

    dilated = ((DILATIONS[1], l1_ref, o1_ref), (DILATIONS[2], l2_ref, o2_ref))
    for gi, (dil, l_ref, _) in enumerate(dilated):
        for rr in range(dil):
            lnat_ref[gi, pl.ds(rr, tm // dil, stride=dil), :] = l_ref[0, rr]

    l0, l1, l2 = l0_ref[...], lnat_ref[0], lnat_ref[1]
    m = jnp.maximum(jnp.maximum(l0, l1), l2)
    es = [jnp.exp(l - m) for l in (l0, l1, l2)]
    inv = 1.0 / (es[0] + es[1] + es[2])
    expand = e_ref[...]

    def widen(w):
        wh, wl = _split2(w)
        return _dot(wh, expand) + _dot(wl, expand)

    attn = widen(es[0] * inv) * o0_ref[...].astype(F32)
    for gi, (dil, _, o_ref) in enumerate(dilated):
        for rr in range(dil):
            for s in range(n_slab):
                onat_ref[s, pl.ds(rr, tm // dil, stride=dil), :] = (
                    o_ref[0, rr, :, s * LANES:(s + 1) * LANES].astype(F32))
        o_nat = jnp.concatenate([onat_ref[s] for s in range(n_slab)], axis=1)
        attn = attn + widen(es[gi + 1] * inv) * o_nat

    xs = xs_ref[...].astype(F32)
    z = z_ref[...].astype(F32)
    y = yf_ref[...].astype(F32) + yb_ref[...].astype(F32) + dskip_ref[...] * xs
    y = _rms(y * (z * _sigmoid(z))) * gssd_ref[...]
    mix = _dot(attn.astype(BF16), wout_ref[0:ATTN_WIDTH, :]) + _dot(y.astype(BF16), wout_ref[ATTN_WIDTH:, :])
    gate_m = mod_ref[0, 2:3, :]
    shift_f = mod_ref[0, 3:4, :]
    scale_f = mod_ref[0, 4:5, :]
    res = x_ref[...] + gate_m * (_rms(mix) * gpost_ref[...])
    res_ref[...] = res
    h2 = _rms(res) * gpre_ref[...] * (1.0 + scale_f) + shift_f
    h2_ref[...] = h2.astype(BF16)

    vals = _dot_f32(h2, wr_ref[...]) + br_ref[...]
    lane = lax.broadcasted_iota(jnp.int32, (tm, LANES), 1)
    sels, tops = [], []
    for _ in range(TOP_K):
        mx = jnp.max(vals, axis=-1, keepdims=True)
        idx = jnp.min(jnp.where(vals == mx, lane, LANES), axis=-1, keepdims=True)
        sel = lane == idx
        sels.append(sel)
        tops.append((mx, idx))
        vals = jnp.where(sel, -jnp.inf, vals)
    ex = [jnp.exp(tv - tops[0][0]) for tv, _ in tops]
    hit = jnp.zeros((tm, LANES), F32)
    for sel in sels:
        hit = jnp.where(sel, 1.0, hit)
    ri = lax.broadcasted_iota(jnp.int32, (tm, tm), 0)
    ci = lax.broadcasted_iota(jnp.int32, (tm, tm), 1)
    before = jnp.where(ci < ri, 1.0, 0.0).astype(BF16)
    rank = _dot(before, hit.astype(BF16))
    cnt = jnp.sum(hit, axis=0, keepdims=True)
    ei = lax.broadcasted_iota(jnp.int32, (LANES, LANES), 0)
    ej = lax.broadcasted_iota(jnp.int32, (LANES, LANES), 1)
    lower = jnp.where(ei < ej, 1.0, 0.0).astype(BF16)
    cnt_h, cnt_l = _split2(jnp.broadcast_to(cnt, (SUBLANES, LANES)))
    first_row = (_dot(cnt_h, lower) + _dot(cnt_l, lower))[0:1, :]
    route = jnp.zeros((tm, LANES), F32)
    for kk, sel in enumerate(sels):
        rk = jnp.sum(jnp.where(sel, rank, 0.0), axis=-1, keepdims=True)
        dest = jnp.sum(jnp.where(sel, rank + first_row, 0.0), axis=-1, keepdims=True)
        for base, val in ((0, tops[kk][1].astype(F32)), (TOP_K, dest), (2 * TOP_K, rk), (3 * TOP_K, ex[kk])):
            route = jnp.where(lane == base + kk, val, route)
    route_ref[...] = route
    cnt_ref[0] = jnp.broadcast_to(cnt, (SUBLANES, LANES)).astype(jnp.int32)


def _mix_and_route(outs, lses, yf, yb, xc, z, x2, mod, d_skip, g_ssd, w_out, g_post, g_pre_ffn,
                   w_router, b_router, *, seq, tm=256):
    t_total = x2.shape[0]
    nt = t_total // tm
    tiles_per_seq = seq // tm
    expand = (jnp.arange(LANES)[:, None] == jnp.arange(ATTN_WIDTH)[None, :] // HEAD_DIM).astype(BF16)
    wr = jnp.zeros((D_MODEL, LANES), F32).at[:, :N_EXPERTS].set(w_router)
    br = jnp.full((1, LANES), _NEG, F32).at[0, :N_EXPERTS].set(b_router)
    row = lambda i: (i, 0)
    wide = pl.BlockSpec((tm, D_MODEL), row)
    narrow = pl.BlockSpec((tm, LANES), row)
    vec = _const_spec((1, D_MODEL))
    bsz = t_total // seq

    def dilated(a, dil):
        a = a.reshape(bsz, dil, seq // dil, a.shape[-1])
        spec = pl.BlockSpec((1, dil, tm // dil, a.shape[-1]),
                            lambda i: (i // tiles_per_seq, 0, i % tiles_per_seq, 0))
        return a, spec

    o_args, o_specs = [outs[0]], [wide]
    l_args, l_specs = [lses[0]], [narrow]
    for dil, o, l in zip(DILATIONS[1:], outs[1:], lses[1:]):
        a, spec = dilated(o, dil)
        o_args.append(a)
        o_specs.append(spec)
        a, spec = dilated(l, dil)
        l_args.append(a)
        l_specs.append(spec)
    outs, lses = o_args, l_args
    return pl.pallas_call(
        functools.partial(_mix_kernel, tm=tm),
        grid=(nt,),
        in_specs=o_specs + l_specs + [wide] * 5 + [
            pl.BlockSpec((1, N_MOD, D_MODEL), lambda i: (i // tiles_per_seq, 0, 0)),
            _const_spec((LANES, ATTN_WIDTH)), vec, vec,
            _const_spec((ATTN_WIDTH + SSD_WIDTH, D_MODEL)), vec, vec,
            _const_spec((D_MODEL, LANES)), _const_spec((1, LANES))],
        out_specs=[wide, wide, narrow, pl.BlockSpec((1, SUBLANES, LANES), lambda i: (i, 0, 0))],
        out_shape=[jax.ShapeDtypeStruct((t_total, D_MODEL), F32),
                   jax.ShapeDtypeStruct((t_total, D_MODEL), BF16),
                   jax.ShapeDtypeStruct((t_total, LANES), F32),
                   jax.ShapeDtypeStruct((nt, SUBLANES, LANES), jnp.int32)],
        scratch_shapes=[pltpu.VMEM((len(DILATIONS) - 1, tm, LANES), F32),
                        pltpu.VMEM((ATTN_WIDTH // LANES, tm, LANES), F32)],
        compiler_params=_cparams("parallel"),
        name="mix_route",
    )(*outs, *lses, yf, yb, xc, z, x2, mod, expand, jnp.repeat(d_skip, SSD_HEAD_DIM).reshape(1, -1),
      g_ssd.reshape(1, -1), w_out.astype(BF16), g_post.reshape(1, -1), g_pre_ffn.reshape(1, -1), wr, br)


_MOE_TM = 256
_ROUTE_TM = 256
_MOE_CAP = 32
_BF16_ROWS = 2 * SUBLANES
_MOE_WIN = _MOE_CAP + _BF16_ROWS
_MOE_TAIL = _MOE_TM
_MOE_SLOTS = N_EXPERTS + _ROUTE_TM * TOP_K // _MOE_CAP
_MOE_KBLOCK = 16
_ROW_SPLIT = D_MODEL // LANES


def _store_rows(ref, lead, row0, vals):
    n = vals.shape[0]
    for q in range(_ROW_SPLIT):
        ref[lead, pl.ds(row0 * _ROW_SPLIT + q, n, stride=_ROW_SPLIT), :] = vals[:, q * LANES:(q + 1) * LANES]


def _load_rows(ref, lead, row0, n):
    parts = [ref[lead, pl.ds(row0 * _ROW_SPLIT + q, n, stride=_ROW_SPLIT), :] for q in range(_ROW_SPLIT)]
    return jnp.concatenate(parts, axis=1)


def _sorted_rows(t_total):
    return t_total * TOP_K + N_EXPERTS * _MOE_CAP + _MOE_TAIL


def _dispatch_kernel(src_ref, dst_ref, num_ref, pad_ref, route_ref, h_ref, xs_ref, stage_ref, sem, *, nt):
    i = pl.program_id(0)
    slot = i % 2
    cap = _MOE_CAP
    n_pairs = _ROUTE_TM * TOP_K
    chunk_rows = cap * _ROW_SPLIT

    def chunk_copy(step, j, s):
        src = pl.multiple_of(src_ref[step * _MOE_SLOTS + j] * _ROW_SPLIT, _ROW_SPLIT)
        dst = pl.multiple_of(dst_ref[step * _MOE_SLOTS + j] * _ROW_SPLIT, _ROW_SPLIT)
        return pltpu.make_async_copy(stage_ref.at[s, pl.ds(src, chunk_rows)], xs_ref.at[pl.ds(dst, chunk_rows)],
                                     sem.at[s])

    @pl.when(i == 0)
    def _():
        for s in range(2):
            stage_ref[s, n_pairs * _ROW_SPLIT:, :] = jnp.zeros((chunk_rows, LANES), F32)

    @pl.when(i < nt)
    def _():
        pos_t = route_ref[...].T[TOP_K:2 * TOP_K, :]
        hb = h_ref[...]
        blk = _ROUTE_TM
        for b in range(n_pairs // blk):
            jrow = (lax.broadcasted_iota(jnp.int32, (blk, _ROUTE_TM), 0) + b * blk).astype(F32)
            onehot = jnp.zeros((blk, _ROUTE_TM), F32)
            for kk in range(TOP_K):
                onehot = onehot + jnp.where(pos_t[kk:kk + 1, :] == jrow, 1.0, 0.0)
            _store_rows(stage_ref, slot, b * blk, _dot(onehot.astype(BF16), hb))

    @pl.when(i > 0)
    def _():
        def wait_prev(j, carry):
            chunk_copy(i - 1, j, 1 - slot).wait()
            return carry
        lax.fori_loop(0, num_ref[i - 1], wait_prev, 0)

    def start(j, carry):
        chunk_copy(i, j, slot).start()
        return carry

    lax.fori_loop(0, num_ref[i], start, 0)

    @pl.when(i == nt)
    def _():
        n_rows = xs_ref.shape[0] // _ROW_SPLIT
        zeros = stage_ref.at[slot, pl.ds(0, chunk_rows)]
        zeros[...] = jnp.zeros((chunk_rows, LANES), F32)
        fills = [pad_ref[e] for e in range(N_EXPERTS)]
        fills += [min(n_rows - _MOE_TAIL + j * cap, n_rows - cap) for j in range(-(-_MOE_TAIL // cap))]
        copies = [pltpu.make_async_copy(zeros, xs_ref.at[pl.ds(pl.multiple_of(r * _ROW_SPLIT, _ROW_SPLIT),
                                                               chunk_rows)], sem.at[slot]) for r in fills]
        for cp in copies:
            cp.start()
        for cp in copies:
            cp.wait()


def _dispatch(h2, route, lists):
    t_total = h2.shape[0]
    nt = t_total // _ROUTE_TM
    n_rows = _sorted_rows(t_total)
    tile = lambda i, *_: (jnp.minimum(i, nt - 1), 0)
    grid_spec = pltpu.PrefetchScalarGridSpec(
        num_scalar_prefetch=4,
        grid=(nt + 1,),
        in_specs=[pl.BlockSpec((_ROUTE_TM, LANES), tile),
                  pl.BlockSpec((_ROUTE_TM, D_MODEL), tile)],
        out_specs=pl.BlockSpec(memory_space=pl.ANY),
        scratch_shapes=[pltpu.VMEM((2, (_ROUTE_TM * TOP_K + _MOE_CAP) * _ROW_SPLIT, LANES), F32),
                        pltpu.SemaphoreType.DMA((2,))])
    return pl.pallas_call(
        functools.partial(_dispatch_kernel, nt=nt),
        grid_spec=grid_spec,
        out_shape=jax.ShapeDtypeStruct((n_rows * _ROW_SPLIT, LANES), F32),
        compiler_params=_cparams("arbitrary"),
        name="moe_dispatch",
    )(lists["stage_row"], lists["sorted_row"], lists["num"], lists["pad_row"], route, h2)


def _gmm_kernel(tile_ref, exp_ref, valid_ref, offs_ref, x_ref, wgu_ref, bgu_ref, wdn_ref, bdn_ref, y_ref,
                wgu_bf, wdn_bf):
    j = pl.program_id(0)
    e = exp_ref[j]
    tile = tile_ref[j]
    prev_j = jnp.maximum(j - 1, 0)
    new_expert = (j == 0) | (exp_ref[prev_j] != e)
    new_tile = (j == 0) | (tile_ref[prev_j] != tile)

    @pl.when(new_expert)
    def _():
        wgu_bf[...] = wgu_ref[0].astype(BF16)
        wdn_bf[...] = wdn_ref[0].astype(BF16)

    @pl.when(valid_ref[j] != 0)
    def _():
        half = D_MODEL // 2
        gu = bgu_ref[0]
        for kc in range(2):
            parts = [x_ref[0, pl.ds(q, _MOE_TM, stride=_ROW_SPLIT), :]
                     for q in range(kc * _ROW_SPLIT // 2, (kc + 1) * _ROW_SPLIT // 2)]
            xk = jnp.concatenate(parts, axis=1).astype(BF16)
            gu = gu + _dot(xk, wgu_bf[kc * half:(kc + 1) * half, :])
        gate = jnp.minimum(gu[:, :EXPERT_FF], SWIGLU_LIMIT)
        up = jnp.clip(gu[:, EXPERT_FF:], -SWIGLU_LIMIT, SWIGLU_LIMIT)
        act = (up + 1.0) * gate * _sigmoid(SWIGLU_ALPHA * gate)
        y = (_dot(act.astype(BF16), wdn_bf[...]) + bdn_ref[0]).astype(BF16)
        rows = tile * _MOE_TM + lax.broadcasted_iota(jnp.int32, (_MOE_TM, 1), 0)
        mine = (rows >= offs_ref[e]) & (rows < offs_ref[e + 1])

        @pl.when(new_tile)
        def _():
            y_ref[...] = jnp.where(mine, y, jnp.zeros_like(y))

        @pl.when(jnp.logical_not(new_tile))
        def _():
            y_ref[...] = jnp.where(mine, y, y_ref[...])


def _grouped_mlp(xs, offs, w_gate_up, b_gate_up, w_down, b_down):
    n_tiles = xs.shape[0] // (_MOE_TM * _ROW_SPLIT)
    n_visits = n_tiles + N_EXPERTS - 1
    first = offs[:-1] // _MOE_TM
    last = (offs[1:] - 1) // _MOE_TM
    per_expert = last - first + 1
    vstart = jnp.concatenate([jnp.zeros((1,), jnp.int32), jnp.cumsum(per_expert)]).astype(jnp.int32)
    total = vstart[-1]
    vis = jnp.minimum(jnp.arange(n_visits, dtype=jnp.int32), total - 1)
    exp_ids = (jnp.sum(vstart[None, :] <= vis[:, None], axis=1) - 1).astype(jnp.int32)
    mine = exp_ids[:, None] == jnp.arange(N_EXPERTS, dtype=jnp.int32)[None, :]
    pick = lambda a: jnp.sum(jnp.where(mine, a[None, :], 0), axis=1)
    tile_ids = (pick(first) + vis - pick(vstart[:-1])).astype(jnp.int32)
    valid = (jnp.arange(n_visits) < total).astype(jnp.int32)
    grid_spec = pltpu.PrefetchScalarGridSpec(
        num_scalar_prefetch=4,
        grid=(n_visits,),
        in_specs=[pl.BlockSpec((1, _MOE_TM * _ROW_SPLIT, LANES), lambda j, t, e, v, o: (t[j], 0, 0)),
                  pl.BlockSpec((1, D_MODEL, 2 * EXPERT_FF), lambda j, t, e, v, o: (e[j], 0, 0)),
                  pl.BlockSpec((1, 1, 2 * EXPERT_FF), lambda j, t, e, v, o: (e[j], 0, 0)),
                  pl.BlockSpec((1, EXPERT_FF, D_MODEL), lambda j, t, e, v, o: (e[j], 0, 0)),
                  pl.BlockSpec((1, 1, D_MODEL), lambda j, t, e, v, o: (e[j], 0, 0))],
        out_specs=pl.BlockSpec((_MOE_TM, D_MODEL), lambda j, t, e, v, o: (t[j], 0)),
        scratch_shapes=[pltpu.VMEM((D_MODEL, 2 * EXPERT_FF), BF16), pltpu.VMEM((EXPERT_FF, D_MODEL), BF16)])
    tiled = (n_tiles, _MOE_TM * _ROW_SPLIT, LANES)
    return pl.pallas_call(
        _gmm_kernel,
        grid_spec=grid_spec,
        out_shape=jax.ShapeDtypeStruct((n_tiles * _MOE_TM, D_MODEL), BF16),
        compiler_params=_cparams("arbitrary"),
        name="moe_mlp",
    )(tile_ids, exp_ids, valid, offs, xs.reshape(tiled), w_gate_up, b_gate_up.reshape(N_EXPERTS, 1, -1),
      w_down, b_down.reshape(N_EXPERTS, 1, -1))


def _combine_kernel(win_ref, num_ref, ys_ref, route_ref, base_ref, res_ref, mod_ref, g_ref,
                    o_ref, ybuf_ref, acc_ref, sem, *, nt):
    i = pl.program_id(0)
    slot = i % 2
    cap = _MOE_CAP
    win = _MOE_WIN
    tm = _ROUTE_TM
    kcols = _MOE_KBLOCK * win

    def window_copy(step, j, s):
        src = pl.multiple_of(win_ref[step * _MOE_SLOTS + j], _BF16_ROWS)
        return pltpu.make_async_copy(ys_ref.at[pl.ds(src, win)],
                                     ybuf_ref.at[s, pl.ds(pl.multiple_of(j * win, _BF16_ROWS), win)], sem.at[s])

    def start_all(step, s):
        def body(j, carry):
            window_copy(step, j, s).start()
            return carry
        lax.fori_loop(0, num_ref[step], body, 0)

    @pl.when(i == 0)
    def _():
        ybuf_ref[...] = jnp.zeros_like(ybuf_ref)
        start_all(0, 0)

    @pl.when(i + 1 < nt)
    def _():
        start_all(i + 1, 1 - slot)

    def wait_own(j, carry):
        window_copy(i, j, slot).wait()
        return carry

    lax.fori_loop(0, num_ref[i], wait_own, 0)

    route = route_ref[...]
    lane = lax.broadcasted_iota(jnp.int32, (tm, LANES), 1).astype(F32)
    base = base_ref[0, 0:1, :]
    cols, wts = [], []
    for kk in range(TOP_K):
        expert = route[:, kk:kk + 1]
        rank = route[:, 2 * TOP_K + kk:2 * TOP_K + kk + 1]
        chunk = jnp.floor(rank * (1.0 / cap))
        first = jnp.sum(jnp.where(lane == expert, base, 0.0), axis=-1, keepdims=True)
        cols.append(first + chunk * win + (rank - chunk * cap))
        wts.append(route[:, 3 * TOP_K + kk:3 * TOP_K + kk + 1])

    for b in range(_MOE_SLOTS // _MOE_KBLOCK):
        @pl.when(b * _MOE_KBLOCK < num_ref[i])
        def _():
            col = (lax.broadcasted_iota(jnp.int32, (tm, kcols), 1) + b * kcols).astype(F32)
            pw = jnp.zeros((tm, kcols), F32)
            for kk in range(TOP_K):
                pw = pw + jnp.where(col == cols[kk], wts[kk], 0.0)
            part = _dot(pw.astype(BF16), ybuf_ref[slot, b * kcols:(b + 1) * kcols, :])
            if b == 0:
                acc_ref[...] = part
            else:
                acc_ref[...] += part

    ffn = acc_ref[...] * (1.0 / (wts[0] + wts[1] + wts[2] + wts[3]))
    gate_f = mod_ref[0, 5:6, :]
    o_ref[...] = res_ref[...] + gate_f * (_rms(ffn) * g_ref[...])


def _combine(ys, route, lists, res, mod, g_post_ffn, *, seq):
    t_total = res.shape[0]
    nt = t_total // _ROUTE_TM
    tiles_per_seq = seq // _ROUTE_TM
    row = lambda i, *_: (i, 0)
    grid_spec = pltpu.PrefetchScalarGridSpec(
        num_scalar_prefetch=2,
        grid=(nt,),
        in_specs=[pl.BlockSpec(memory_space=pl.ANY),
                  pl.BlockSpec((_ROUTE_TM, LANES), row),
                  pl.BlockSpec((1, SUBLANES, LANES), lambda i, *_: (i, 0, 0)),
                  pl.BlockSpec((_ROUTE_TM, D_MODEL), row),
                  pl.BlockSpec((1, N_MOD, D_MODEL), lambda i, *_: (i // tiles_per_seq, 0, 0)),
                  pl.BlockSpec((1, D_MODEL), lambda i, *_: (0, 0))],
        out_specs=pl.BlockSpec((_ROUTE_TM, D_MODEL), row),
        scratch_shapes=[pltpu.VMEM((2, _MOE_SLOTS * _MOE_WIN, D_MODEL), BF16),
                        pltpu.VMEM((_ROUTE_TM, D_MODEL), F32),
                        pltpu.SemaphoreType.DMA((2,))])
    return pl.pallas_call(
        functools.partial(_combine_kernel, nt=nt),
        grid_spec=grid_spec,
        out_shape=jax.ShapeDtypeStruct((t_total, D_MODEL), F32),
        compiler_params=_cparams("arbitrary"),
        name="moe_combine",
    )(lists["window_row"], lists["num"], ys, route, lists["window_base"], res, mod, g_post_ffn.reshape(1, -1))


def _chunk_lists(n_te, offs):
    nt = n_te.shape[0]
    cap = _MOE_CAP
    zero_col = jnp.zeros((nt, 1), jnp.int32)
    zero_row = jnp.zeros((1, N_EXPERTS), jnp.int32)
    before = jnp.concatenate([zero_row, jnp.cumsum(n_te, axis=0)[:-1]], axis=0)
    seg = offs[None, :N_EXPERTS] + before
    first = jnp.concatenate([zero_col, jnp.cumsum(n_te, axis=1)[:, :-1]], axis=1)
    chunk_base = jnp.concatenate([zero_col, jnp.cumsum((n_te + cap - 1) // cap, axis=1)], axis=1)
    slots = jnp.arange(_MOE_SLOTS, dtype=jnp.int32)
    expert = jnp.minimum(jnp.sum(chunk_base[:, 1:, None] <= slots[None, None, :], axis=1), N_EXPERTS - 1)
    pick = expert[:, None, :] == jnp.arange(N_EXPERTS, dtype=jnp.int32)[None, :, None]
    take = lambda a: jnp.sum(jnp.where(pick, a[:, :, None], 0), axis=1)
    chunk = slots[None, :] - take(chunk_base[:, :N_EXPERTS])
    sorted_row = take(seg) + chunk * cap
    num = jnp.concatenate([chunk_base[:, -1], jnp.zeros((1,), jnp.int32)])
    window_base = chunk_base[:, :N_EXPERTS] * _MOE_WIN + seg % _BF16_ROWS
    window_base = jnp.zeros((nt, SUBLANES, LANES), F32).at[:, :, :N_EXPERTS].set(
        window_base.astype(F32)[:, None, :])
    as_i32 = lambda a: a.astype(jnp.int32).reshape(-1)
    return {"stage_row": as_i32(take(first) + chunk * cap),
            "sorted_row": as_i32(sorted_row),
            "window_row": as_i32(sorted_row // _BF16_ROWS * _BF16_ROWS),
            "num": as_i32(num),
            "pad_row": as_i32(offs[1:] - cap).at[N_EXPERTS - 1].add(-_MOE_TAIL),
            "window_base": window_base}


def _forward(x, c, w_ada, b_ada, g_pre_mix, g_post_mix, w_in, conv_w, conv_b, dt_bias, a_log, d_skip,
             g_ssd_norm, w_out, g_pre_ffn, g_post_ffn, w_router, b_router, w_gate_up, b_gate_up, w_down, b_down):
    bsz, seq, _ = x.shape
    x2 = x.astype(F32).reshape(bsz * seq, D_MODEL)
    mod = _ada_mod(c.astype(F32), w_ada, b_ada)
    *qkv, z, xc, dt = _in_proj(x2, mod, g_pre_mix, w_in, conv_w, conv_b, dt_bias, seq=seq)
    attn = []
    for gi, dil in enumerate(DILATIONS):
        q, k, v = (a.reshape(bsz * seq, ATTN_WIDTH) for a in qkv[3 * gi:3 * gi + 3])
        if gi == 0:
            o, lse, yf, yb = _dilated_attention(q, k, v, dil, seq=seq, ssd=(xc, dt, a_log))
            attn.append((o, lse))
        else:
            attn.append(_dilated_attention(q, k, v, dil, seq=seq))
    res, h2, route, cnt = _mix_and_route(
        [o for o, _ in attn], [l for _, l in attn], yf, yb, xc, z, x2, mod, d_skip, g_ssd_norm, w_out,
        g_post_mix, g_pre_ffn, w_router, b_router, seq=seq, tm=_ROUTE_TM)
    n_te = cnt[:, 0, :N_EXPERTS]
    region = jnp.sum(n_te, axis=0) + _MOE_CAP
    region = region.at[N_EXPERTS - 1].add(_MOE_TAIL)
    offs = jnp.concatenate([jnp.zeros((1,), jnp.int32), jnp.cumsum(region)]).astype(jnp.int32)
    lists = _chunk_lists(n_te, offs)
    xs = _dispatch(h2, route, lists)
    ys = _grouped_mlp(xs, offs, w_gate_up, b_gate_up, w_down, b_down)
    out = _combine(ys, route, lists, res, mod, g_post_ffn, seq=seq)
    return out.reshape(bsz, seq, D_MODEL)


def kernel(x, c, w_ada, b_ada, g_pre_mix, g_post_mix, w_in, conv_w, conv_b, dt_bias, a_log, d_skip, g_ssd_norm, w_out, g_pre_ffn, g_post_ffn, w_router, b_router, w_gate_up, b_gate_up, w_down, b_down):
    layer = lambda t: t[0].astype(F32)
    out = _forward(x, c, layer(w_ada), layer(b_ada), layer(g_pre_mix), layer(g_post_mix), layer(w_in),
                   layer(conv_w), layer(conv_b), layer(dt_bias), layer(a_log), layer(d_skip),
                   layer(g_ssd_norm), layer(w_out), layer(g_pre_ffn), layer(g_post_ffn), layer(w_router),
                   layer(b_router), layer(w_gate_up), layer(b_gate_up), layer(w_down), layer(b_down))
    return out.astype(x.dtype)
```

```python
import functools

import numpy as np
import jax
import jax.numpy as jnp
from jax import lax
from jax.experimental import pallas as pl
from jax.experimental.pallas import tpu as pltpu

F32 = jnp.float32
BF16 = jnp.bfloat16

D_MODEL = 1024
ATTN_HEADS = 16
HEAD_DIM = 64
ATTN_WIDTH = ATTN_HEADS * HEAD_DIM
DILATIONS = (1, 4, 16)
ATTN_HALF = 64
SSD_HEADS = 16
SSD_HEAD_DIM = 64
SSD_WIDTH = SSD_HEADS * SSD_HEAD_DIM
SSD_GROUPS = 2
SSD_STATE = 128
SSD_CHUNK = 128
CONV_WIDTH = 5
XBC_WIDTH = SSD_WIDTH + 2 * SSD_GROUPS * SSD_STATE
N_EXPERTS = 32
TOP_K = 4
EXPERT_FF = 1024
SWIGLU_ALPHA = 1.702
SWIGLU_LIMIT = 7.0
RMS_EPS = 1e-6
N_MOD = 6
LANES = 128
SUBLANES = 8
VMEM_LIMIT = 56 * 1024 * 1024


def _cparams(*sem):
    return pltpu.CompilerParams(dimension_semantics=sem, vmem_limit_bytes=VMEM_LIMIT)


def _const_spec(shape):
    nd = len(shape)
    return pl.BlockSpec(shape, lambda *_: (0,) * nd)


def _split2(a):
    hi = a.astype(BF16)
    lo = (a - hi.astype(F32)).astype(BF16)
    return hi, lo


def _dot(a, b):
    return jnp.dot(a, b, preferred_element_type=F32)


def _dot_f32(a, b):
    ah, al = _split2(a)
    bh, bl = _split2(b)
    return _dot(ah, bh) + (_dot(ah, bl) + _dot(al, bh))


def _sigmoid(x):
    return 1.0 / (1.0 + jnp.exp(-x))


def _rms(x):
    return x * lax.rsqrt(jnp.mean(x * x, axis=-1, keepdims=True) + RMS_EPS)


def _ada_kernel(c_ref, w_ref, b_ref, o_ref):
    c = c_ref[...]
    o_ref[...] = _dot_f32(c * _sigmoid(c), w_ref[...]) + b_ref[...]


def _ada_mod(c, w_ada, b_ada):
    bsz = c.shape[0]
    c8 = jnp.zeros((SUBLANES, D_MODEL), F32).at[:bsz].set(c)
    out = pl.pallas_call(
        _ada_kernel,
        grid=(N_MOD,),
        in_specs=[_const_spec((SUBLANES, D_MODEL)),
                  pl.BlockSpec((D_MODEL, D_MODEL), lambda j: (0, j)),
                  pl.BlockSpec((1, D_MODEL), lambda j: (0, j))],
        out_specs=pl.BlockSpec((SUBLANES, D_MODEL), lambda j: (0, j)),
        out_shape=jax.ShapeDtypeStruct((SUBLANES, N_MOD * D_MODEL), F32),
        compiler_params=_cparams("parallel"),
        name="ada_mod",
    )(c8, w_ada, b_ada.reshape(1, -1))
    return out[:bsz].reshape(bsz, N_MOD, D_MODEL)


def _inproj_kernel(x_ref, xp_ref, xn_ref, mod_ref, g_ref, wqkv_ref, wz_ref, wxbc_ref, wdt_ref,
                   cw_ref, cb_ref, dtb_ref,
                   q1_ref, k1_ref, v1_ref, q4_ref, k4_ref, v4_ref, q16_ref, k16_ref, v16_ref,
                   z_ref, xc_ref, dt_ref, buf_ref, slab_ref, *, tm, tiles_per_seq):
    i = pl.program_id(0)
    shift = mod_ref[0, 0:1, :]
    scale = mod_ref[0, 1:2, :]
    g = g_ref[...]

    def norm_mod(x):
        return _rms(x) * g * (1.0 + scale) + shift

    h = norm_mod(x_ref[...])
    hb = h.astype(BF16)
    hh = norm_mod(jnp.concatenate([xp_ref[...], xn_ref[...]], axis=0)).astype(BF16)
    halo = _dot(hh, wxbc_ref[...])
    t_in_seq = i % tiles_per_seq
    prev_ok = (t_in_seq != 0).astype(F32)
    next_ok = (t_in_seq != tiles_per_seq - 1).astype(F32)
    buf_ref[0:SUBLANES, :] = halo[0:SUBLANES] * prev_ok
    buf_ref[SUBLANES + tm:, :] = halo[SUBLANES:] * next_ok
    buf_ref[SUBLANES:SUBLANES + tm, :] = _dot(hb, wxbc_ref[...])

    n_slab = ATTN_WIDTH // LANES
    outs = ((q1_ref, q4_ref, q16_ref), (k1_ref, k4_ref, k16_ref), (v1_ref, v4_ref, v16_ref))
    conv_cols = XBC_WIDTH // len(outs)
    for c, (nat_ref, d4_ref, d16_ref) in enumerate(outs):
        r = _dot(hb, wqkv_ref[:, c * ATTN_WIDTH:(c + 1) * ATTN_WIDTH])
        if c == 0:
            r = r * HEAD_DIM ** -0.5
        nat_ref[...] = r.astype(BF16)
        for s in range(n_slab):
            slab_ref[0, s] = r[:, s * LANES:(s + 1) * LANES]
        step = DILATIONS[1]
        sub = tm // step
        for r4 in range(step):
            for s in range(n_slab):
                rows = slab_ref[0, s, pl.ds(r4, sub, stride=step), :]
                d4_ref[0, r4, :, s * LANES:(s + 1) * LANES] = rows.astype(BF16)
                slab_ref[1, s, r4 * sub:(r4 + 1) * sub, :] = rows
        for r4 in range(step):
            for s4 in range(step):
                for s in range(n_slab):
                    rows = slab_ref[1, s, pl.ds(r4 * sub + s4, sub // step, stride=step), :]
                    d16_ref[0, step * s4 + r4, :, s * LANES:(s + 1) * LANES] = rows.astype(BF16)
        cols = slice(c * conv_cols, (c + 1) * conv_cols)
        acc = cb_ref[:, cols]
        for j in range(CONV_WIDTH):
            off = SUBLANES - CONV_WIDTH // 2 + j
            acc = acc + buf_ref[off:off + tm, cols] * cw_ref[j:j + 1, cols]
        xc_ref[:, cols] = (acc * _sigmoid(acc)).astype(BF16)
    z_ref[...] = _dot(hb, wz_ref[...]).astype(BF16)
    dt_ref[...] = jax.nn.softplus(_dot_f32(h, wdt_ref[...]) + dtb_ref[...])


def _in_proj(x2, mod, g_pre, w_in, conv_w, conv_b, dt_bias, *, seq, tm=512):
    t_total = x2.shape[0]
    nt = t_total // tm
    tiles_per_seq = seq // tm
    hb = tm // SUBLANES
    n_hblk = t_total // SUBLANES
    qkv_w = 3 * ATTN_WIDTH
    w_qkv = w_in[:, :qkv_w].astype(BF16)
    w_z = w_in[:, qkv_w:qkv_w + SSD_WIDTH].astype(BF16)
    w_xbc = w_in[:, qkv_w + SSD_WIDTH:qkv_w + SSD_WIDTH + XBC_WIDTH].astype(BF16)
    w_dt_raw = w_in[:, qkv_w + SSD_WIDTH + XBC_WIDTH:]
    w_dt = jnp.zeros((D_MODEL, 2 * LANES), F32)
    dtb = jnp.zeros((1, 2 * LANES), F32)
    for dr in range(2):
        w_dt = w_dt.at[:, dr * LANES:dr * LANES + SSD_HEADS].set(w_dt_raw[:, dr * SSD_HEADS:(dr + 1) * SSD_HEADS])
        dtb = dtb.at[0, dr * LANES:dr * LANES + SSD_HEADS].set(dt_bias[dr])
    row = lambda i: (i, 0)
    bsz = t_total // seq
    dil_shapes, dil_specs = [], []
    for dil in DILATIONS[1:]:
        dil_shapes += [jax.ShapeDtypeStruct((bsz, dil, seq // dil, ATTN_WIDTH), BF16)] * 3
        dil_specs += [pl.BlockSpec((1, dil, tm // dil, ATTN_WIDTH),
                                   lambda i: (i // tiles_per_seq, 0, i % tiles_per_seq, 0))] * 3
    out_shape = [jax.ShapeDtypeStruct((t_total, ATTN_WIDTH), BF16)] * 3 + dil_shapes + [
        jax.ShapeDtypeStruct((t_total, SSD_WIDTH), BF16),
        jax.ShapeDtypeStruct((t_total, XBC_WIDTH), BF16),
        jax.ShapeDtypeStruct((t_total, 2 * LANES), F32)]
    return pl.pallas_call(
        functools.partial(_inproj_kernel, tm=tm, tiles_per_seq=tiles_per_seq),
        grid=(nt,),
        in_specs=[pl.BlockSpec((tm, D_MODEL), row),
                  pl.BlockSpec((SUBLANES, D_MODEL), lambda i: (jnp.maximum(i * hb - 1, 0), 0)),
                  pl.BlockSpec((SUBLANES, D_MODEL), lambda i: (jnp.minimum((i + 1) * hb, n_hblk - 1), 0)),
                  pl.BlockSpec((1, N_MOD, D_MODEL), lambda i: (i // tiles_per_seq, 0, 0)),
                  _const_spec((1, D_MODEL)),
                  _const_spec((D_MODEL, qkv_w)),
                  _const_spec((D_MODEL, SSD_WIDTH)),
                  _const_spec((D_MODEL, XBC_WIDTH)),
                  _const_spec((D_MODEL, 2 * LANES)),
                  _const_spec((CONV_WIDTH, XBC_WIDTH)),
                  _const_spec((1, XBC_WIDTH)),
                  _const_spec((1, 2 * LANES))],
        out_specs=[pl.BlockSpec((tm, ATTN_WIDTH), row)] * 3 + dil_specs + [
            pl.BlockSpec((tm, SSD_WIDTH), row),
            pl.BlockSpec((tm, XBC_WIDTH), row),
            pl.BlockSpec((tm, 2 * LANES), row)],
        out_shape=out_shape,
        scratch_shapes=[pltpu.VMEM((tm + 2 * SUBLANES, XBC_WIDTH), F32),
                        pltpu.VMEM((2, ATTN_WIDTH // LANES, tm, LANES), F32)],
        compiler_params=_cparams("parallel"),
        name="in_proj",
    )(x2, x2, x2, mod, g_pre.reshape(1, -1), w_qkv, w_z, w_xbc, w_dt, conv_w, conv_b.reshape(1, -1), dtb)


_NEG = -1e30


def _ssd_direction(xc_ref, dt_ref, a_ref, e_ref, h_ref, y_ref, reverse):
    qn = SSD_CHUNK
    gw = SSD_WIDTH // SSD_GROUPS
    li = lax.broadcasted_iota(jnp.int32, (qn, qn), 0)
    si = lax.broadcasted_iota(jnp.int32, (qn, qn), 1)
    mask = (si >= li) if reverse else (si <= li)
    tri = jnp.where(mask, 1.0, 0.0).astype(BF16)
    dt = dt_ref[...]
    adt = dt * a_ref[...]
    p0 = adt.astype(BF16)
    r0 = adt - p0.astype(F32)
    p1 = r0.astype(BF16)
    p2 = (r0 - p1.astype(F32)).astype(BF16)
    acum = _dot(tri, p0) + (_dot(tri, p1) + _dot(tri, p2))
    last = 0 if reverse else qn - 1
    eo = jnp.exp(acum)
    ds = jnp.exp(acum[last:last + 1, :] - acum)
    dt_h, dt_l = _split2(dt)
    eo_h, eo_l = _split2(eo)
    ex = _dot(jnp.concatenate([dt_h, eo_h, ds.astype(BF16)], axis=0), e_ref[...])
    ex_l = _dot(jnp.concatenate([dt_l, eo_l], axis=0), e_ref[...])
    dt_x = ex[0:qn] + ex_l[0:qn]
    eo_x = ex[qn:2 * qn] + ex_l[qn:]
    ds_x = ex[2 * qn:]
    xd = xc_ref[:, 0:SSD_WIDTH].astype(F32) * dt_x
    xdb = xd.astype(BF16)
    xwb = (xd * ds_x).astype(BF16)
    acum_t = acum.T
    lane = lax.broadcasted_iota(jnp.int32, (qn, LANES), 1)
    for g in range(SSD_GROUPS):
        b0 = SSD_WIDTH + g * SSD_STATE
        c0 = SSD_WIDTH + SSD_GROUPS * SSD_STATE + g * SSD_STATE
        bm = xc_ref[:, b0:b0 + SSD_STATE]
        cm = xc_ref[:, c0:c0 + SSD_STATE]
        cb = lax.dot_general(cm, bm, (((1,), (1,)), ((), ())), preferred_element_type=F32)
        hg = h_ref[g]
        yoff = _dot(cm, hg.astype(BF16))
        st = lax.dot_general(bm, xwb[:, g * gw:(g + 1) * gw], (((0,), (0,)), ((), ())),
                             preferred_element_type=F32)
        h_ref[g] = hg * eo_x[last:last + 1, g * gw:(g + 1) * gw] + st
        for pr in range(gw // LANES):
            col = g * gw + pr * LANES
            xp = xdb[:, col:col + LANES]
            ms = []
            for e in (col // SSD_HEAD_DIM, col // SSD_HEAD_DIM + 1):
                seg = acum[:, e:e + 1] - acum_t[e:e + 1, :]
                ms.append((cb * jnp.exp(jnp.where(mask, seg, _NEG))).astype(BF16))
            zero = jnp.zeros_like(xp)
            xp2 = jnp.concatenate([jnp.where(lane < SSD_HEAD_DIM, xp, zero),
                                   jnp.where(lane < SSD_HEAD_DIM, zero, xp)], axis=0)
            yd = _dot(jnp.concatenate(ms, axis=1), xp2)
            y = yd + yoff[:, pr * LANES:(pr + 1) * LANES] * eo_x[:, col:col + LANES]
            y_ref[:, col:col + LANES] = y.astype(BF16)


_ATTN_QB = 2 * ATTN_HALF
_ATTN_WIN = _ATTN_QB + 2 * ATTN_HALF
_ATTN_LQ = 512


def _attn_kernel(*refs, lq, sub_len, with_ssd):
    if with_ssd:
        (q_ref, k_ref, kp_ref, kn_ref, v_ref, vp_ref, vn_ref, bias_ref, xcf_ref, xcb_ref, dtf_ref, dtb_ref,
         a_ref, e_ref, o_ref, lse_ref, yf_ref, yb_ref, kw_ref, vw_ref, hf_ref, hb_ref) = refs
    else:
        q_ref, k_ref, kp_ref, kn_ref, v_ref, vp_ref, vn_ref, bias_ref, o_ref, lse_ref, kw_ref, vw_ref = refs
    t = pl.program_id(2)
    if with_ssd:
        @pl.when(t == 0)
        def _():
            hf_ref[...] = jnp.zeros_like(hf_ref)
            hb_ref[...] = jnp.zeros_like(hb_ref)
    hb = ATTN_HALF
    for src, halo_p, halo_n, win in ((k_ref, kp_ref, kn_ref, kw_ref), (v_ref, vp_ref, vn_ref, vw_ref)):
        win[0:hb, :] = halo_p[...]
        win[hb:hb + lq, :] = src[...]
        win[hb + lq:, :] = halo_n[...]
    qn = _ATTN_QB
    lane = lax.broadcasted_iota(jnp.int32, (qn, LANES), 1)
    first_head = lane < HEAD_DIM
    kpos = lax.broadcasted_iota(jnp.int32, (1, _ATTN_WIN), 1)

    def body(qb, carry):
        r0 = pl.multiple_of(qb * qn, qn)
        kidx = t * lq + r0 - hb + kpos
        in_seq = (kidx >= 0) & (kidx < sub_len)
        lse_tile = jnp.zeros((qn, LANES), F32)
        for hp in range(ATTN_HEADS // 2):
            cs = slice(hp * LANES, (hp + 1) * LANES)
            q = q_ref[pl.ds(r0, qn), cs]
            zero = jnp.zeros_like(q)
            q2 = jnp.concatenate([jnp.where(first_head, q, zero), jnp.where(first_head, zero, q)], axis=0)
            s = lax.dot_general(q2, kw_ref[pl.ds(r0, _ATTN_WIN), cs], (((1,), (1,)), ((), ())),
                                preferred_element_type=F32) + bias_ref[hp]
            s = jnp.where(in_seq, s, _NEG)
            m = jnp.max(s, axis=-1, keepdims=True)
            p = jnp.exp(s - m)
            den = jnp.sum(p, axis=-1, keepdims=True)
            pv = _dot(p.astype(BF16), vw_ref[pl.ds(r0, _ATTN_WIN), cs]) * (1.0 / den)
            o_ref[pl.ds(r0, qn), cs] = jnp.where(first_head, pv[0:qn], pv[qn:]).astype(BF16)
            lse = m + jnp.log(den)
            lse_tile = jnp.where(lane == 2 * hp, lse[0:qn], jnp.where(lane == 2 * hp + 1, lse[qn:], lse_tile))
        lse_ref[pl.ds(r0, qn), :] = lse_tile
        if with_ssd:
            b0 = pl.multiple_of((lq // qn - 1 - qb) * qn, qn)
            _ssd_direction(xcf_ref.at[pl.ds(r0, qn)], dtf_ref.at[pl.ds(r0, qn)], a_ref.at[0], e_ref, hf_ref,
                           yf_ref.at[pl.ds(r0, qn)], False)
            _ssd_direction(xcb_ref.at[pl.ds(b0, qn)], dtb_ref.at[pl.ds(b0, qn)], a_ref.at[1], e_ref, hb_ref,
                           yb_ref.at[pl.ds(b0, qn)], True)
        return carry

    lax.fori_loop(0, lq // qn, body, 0, unroll=2)


def _attn_bias(dilation):
    slopes = jnp.exp2(-8.0 * jnp.arange(1, ATTN_HEADS + 1, dtype=F32) / ATTN_HEADS)
    rel = np.abs(np.arange(_ATTN_WIN)[None, :] - ATTN_HALF - np.arange(_ATTN_QB)[:, None])
    dist = jnp.asarray((rel * dilation).astype(np.float32))
    b = jnp.where(jnp.asarray(rel <= ATTN_HALF), -slopes[:, None, None] * dist, _NEG)
    return b.reshape(ATTN_HEADS // 2, 2 * _ATTN_QB, _ATTN_WIN)


def _dilated_attention(q, k, v, dilation, *, seq, ssd=None):
    t_total = q.shape[0]
    bsz = t_total // seq
    sub_len = seq // dilation
    lq = min(_ATTN_LQ, sub_len)
    nt = sub_len // lq
    hpb = lq // ATTN_HALF
    n_hblk = t_total // ATTN_HALF
    tile = lambda b, r, t: (b * dilation + r) * nt + t
    main = lambda b, r, t: (tile(b, r, t), 0)
    prev = lambda b, r, t: (jnp.maximum(tile(b, r, t) * hpb - 1, 0), 0)
    nxt = lambda b, r, t: (jnp.minimum((tile(b, r, t) + 1) * hpb, n_hblk - 1), 0)
    blk = pl.BlockSpec((lq, ATTN_WIDTH), main)
    hblk_p = pl.BlockSpec((ATTN_HALF, ATTN_WIDTH), prev)
    hblk_n = pl.BlockSpec((ATTN_HALF, ATTN_WIDTH), nxt)
    window = pltpu.VMEM((lq + 2 * ATTN_HALF, ATTN_WIDTH), BF16)
    args = [q, k, k, k, v, v, v, _attn_bias(dilation)]
    in_specs = [blk, blk, hblk_p, hblk_n, blk, hblk_p, hblk_n,
                _const_spec((ATTN_HEADS // 2, 2 * _ATTN_QB, _ATTN_WIN))]
    out_specs = [blk, pl.BlockSpec((lq, LANES), main)]
    out_shape = [jax.ShapeDtypeStruct((t_total, ATTN_WIDTH), BF16), jax.ShapeDtypeStruct((t_total, LANES), F32)]
    scratch = [window, window]
    semantics = ("parallel", "parallel", "parallel")
    if ssd is not None:
        assert dilation == 1 and _ATTN_QB == SSD_CHUNK
        xc, dt, a_log = ssd
        a_rows = jnp.zeros((2, 1, LANES), F32).at[:, 0, :SSD_HEADS].set(-jnp.exp(a_log))
        expand = (jnp.arange(LANES)[:, None] == jnp.arange(SSD_WIDTH)[None, :] // SSD_HEAD_DIM).astype(BF16)
        back = lambda b, r, t: (b * nt + nt - 1 - t, 0)
        args += [xc, xc, dt, dt, a_rows, expand]
        in_specs += [pl.BlockSpec((lq, XBC_WIDTH), main), pl.BlockSpec((lq, XBC_WIDTH), back),
                     pl.BlockSpec((lq, LANES), main),
                     pl.BlockSpec((lq, LANES), lambda b, r, t: (b * nt + nt - 1 - t, 1)),
                     _const_spec((2, 1, LANES)), _const_spec((LANES, SSD_WIDTH))]
        out_specs += [pl.BlockSpec((lq, SSD_WIDTH), main), pl.BlockSpec((lq, SSD_WIDTH), back)]
        out_shape += [jax.ShapeDtypeStruct((t_total, SSD_WIDTH), BF16)] * 2
        state = pltpu.VMEM((SSD_GROUPS, SSD_STATE, SSD_WIDTH // SSD_GROUPS), F32)
        scratch += [state, state]
        semantics = ("parallel", "arbitrary", "arbitrary")
    return pl.pallas_call(
        functools.partial(_attn_kernel, lq=lq, sub_len=sub_len, with_ssd=ssd is not None),
        grid=(bsz, dilation, nt),
        in_specs=in_specs,
        out_specs=out_specs,
        out_shape=out_shape,
        scratch_shapes=scratch,
        compiler_params=_cparams(*semantics),
        name=f"attn_d{dilation}" + ("_ssd" if ssd is not None else ""),
    )(*args)


def _mix_kernel(o0_ref, o1_ref, o2_ref, l0_ref, l1_ref, l2_ref, yf_ref, yb_ref, xs_ref, z_ref, x_ref,
                mod_ref, e_ref, dskip_ref, gssd_ref, wout_ref, gpost_ref, gpre_ref, wr_ref, br_ref,
                res_ref, h2_ref, route_ref, cnt_ref, lnat_ref, onat_ref, *, tm):
    n_slab = ATTN_WIDTH // LANES

    dilated = ((DILATIONS[1], l1_ref, o1_ref), (DILATIONS[2], l2_ref, o2_ref))
    for gi, (dil, l_ref, _) in enumerate(dilated):
        for rr in range(dil):
            lnat_ref[gi, pl.ds(rr, tm // dil, stride=dil), :] = l_ref[0, rr]

    l0, l1, l2 = l0_ref[...], lnat_ref[0], lnat_ref[1]
    m = jnp.maximum(jnp.maximum(l0, l1), l2)
    es = [jnp.exp(l - m) for l in (l0, l1, l2)]
    inv = 1.0 / (es[0] + es[1] + es[2])
    expand = e_ref[...]

    def widen(w):
        wh, wl = _split2(w)
        return _dot(wh, expand) + _dot(wl, expand)

    attn = widen(es[0] * inv) * o0_ref[...].astype(F32)
    for gi, (dil, _, o_ref) in enumerate(dilated):
        for rr in range(dil):
            for s in range(n_slab):
                onat_ref[s, pl.ds(rr, tm // dil, stride=dil), :] = (
                    o_ref[0, rr, :, s * LANES:(s + 1) * LANES].astype(F32))
        o_nat = jnp.concatenate([onat_ref[s] for s in range(n_slab)], axis=1)
        attn = attn + widen(es[gi + 1] * inv) * o_nat

    xs = xs_ref[...].astype(F32)
    z = z_ref[...].astype(F32)
    y = yf_ref[...].astype(F32) + yb_ref[...].astype(F32) + dskip_ref[...] * xs
    y = _rms(y * (z * _sigmoid(z))) * gssd_ref[...]
    mix = _dot(attn.astype(BF16), wout_ref[0:ATTN_WIDTH, :]) + _dot(y.astype(BF16), wout_ref[ATTN_WIDTH:, :])
    gate_m = mod_ref[0, 2:3, :]
    shift_f = mod_ref[0, 3:4, :]
    scale_f = mod_ref[0, 4:5, :]
    res = x_ref[...] + gate_m * (_rms(mix) * gpost_ref[...])
    res_ref[...] = res
    h2 = _rms(res) * gpre_ref[...] * (1.0 + scale_f) + shift_f
    h2_ref[...] = h2.astype(BF16)

    vals = _dot_f32(h2, wr_ref[...]) + br_ref[...]
    lane = lax.broadcasted_iota(jnp.int32, (tm, LANES), 1)
    sels, tops = [], []
    for _ in range(TOP_K):
        mx = jnp.max(vals, axis=-1, keepdims=True)
        idx = jnp.min(jnp.where(vals == mx, lane, LANES), axis=-1, keepdims=True)
        sel = lane == idx
        sels.append(sel)
        tops.append((mx, idx))
        vals = jnp.where(sel, -jnp.inf, vals)
    ex = [jnp.exp(tv - tops[0][0]) for tv, _ in tops]
    hit = jnp.zeros((tm, LANES), F32)
    for sel in sels:
        hit = jnp.where(sel, 1.0, hit)
    ri = lax.broadcasted_iota(jnp.int32, (tm, tm), 0)
    ci = lax.broadcasted_iota(jnp.int32, (tm, tm), 1)
    before = jnp.where(ci < ri, 1.0, 0.0).astype(BF16)
    rank = _dot(before, hit.astype(BF16))
    cnt = jnp.sum(hit, axis=0, keepdims=True)
    ei = lax.broadcasted_iota(jnp.int32, (LANES, LANES), 0)
    ej = lax.broadcasted_iota(jnp.int32, (LANES, LANES), 1)
    lower = jnp.where(ei < ej, 1.0, 0.0).astype(BF16)
    cnt_h, cnt_l = _split2(jnp.broadcast_to(cnt, (SUBLANES, LANES)))
    first_row = (_dot(cnt_h, lower) + _dot(cnt_l, lower))[0:1, :]
    route = jnp.zeros((tm, LANES), F32)
    for kk, sel in enumerate(sels):
        rk = jnp.sum(jnp.where(sel, rank, 0.0), axis=-1, keepdims=True)
        dest = jnp.sum(jnp.where(sel, rank + first_row, 0.0), axis=-1, keepdims=True)
        for base, val in ((0, tops[kk][1].astype(F32)), (TOP_K, dest), (2 * TOP_K, rk), (3 * TOP_K, ex[kk])):
            route = jnp.where(lane == base + kk, val, route)
    route_ref[...] = route
    cnt_ref[0] = jnp.broadcast_to(cnt, (SUBLANES, LANES)).astype(jnp.int32)


def _mix_and_route(outs, lses, yf, yb, xc, z, x2, mod, d_skip, g_ssd, w_out, g_post, g_pre_ffn,
                   w_router, b_router, *, seq, tm=256):
    t_total = x2.shape[0]
    nt = t_total // tm
    tiles_per_seq = seq // tm
    expand = (jnp.arange(LANES)[:, None] == jnp.arange(ATTN_WIDTH)[None, :] // HEAD_DIM).astype(BF16)
    wr = jnp.zeros((D_MODEL, LANES), F32).at[:, :N_EXPERTS].set(w_router)
    br = jnp.full((1, LANES), _NEG, F32).at[0, :N_EXPERTS].set(b_router)
    row = lambda i: (i, 0)
    wide = pl.BlockSpec((tm, D_MODEL), row)
    narrow = pl.BlockSpec((tm, LANES), row)
    vec = _const_spec((1, D_MODEL))
    bsz = t_total // seq

    def dilated(a, dil):
        a = a.reshape(bsz, dil, seq // dil, a.shape[-1])
        spec = pl.BlockSpec((1, dil, tm // dil, a.shape[-1]),
                            lambda i: (i // tiles_per_seq, 0, i % tiles_per_seq, 0))
        return a, spec

    o_args, o_specs = [outs[0]], [wide]
    l_args, l_specs = [lses[0]], [narrow]
    for dil, o, l in zip(DILATIONS[1:], outs[1:], lses[1:]):
        a, spec = dilated(o, dil)
        o_args.append(a)
        o_specs.append(spec)
        a, spec = dilated(l, dil)
        l_args.append(a)
        l_specs.append(spec)
    outs, lses = o_args, l_args
    return pl.pallas_call(
        functools.partial(_mix_kernel, tm=tm),
        grid=(nt,),
        in_specs=o_specs + l_specs + [wide] * 5 + [
            pl.BlockSpec((1, N_MOD, D_MODEL), lambda i: (i // tiles_per_seq, 0, 0)),
            _const_spec((LANES, ATTN_WIDTH)), vec, vec,
            _const_spec((ATTN_WIDTH + SSD_WIDTH, D_MODEL)), vec, vec,
            _const_spec((D_MODEL, LANES)), _const_spec((1, LANES))],
        out_specs=[wide, wide, narrow, pl.BlockSpec((1, SUBLANES, LANES), lambda i: (i, 0, 0))],
        out_shape=[jax.ShapeDtypeStruct((t_total, D_MODEL), F32),
                   jax.ShapeDtypeStruct((t_total, D_MODEL), BF16),
                   jax.ShapeDtypeStruct((t_total, LANES), F32),
                   jax.ShapeDtypeStruct((nt, SUBLANES, LANES), jnp.int32)],
        scratch_shapes=[pltpu.VMEM((len(DILATIONS) - 1, tm, LANES), F32),
                        pltpu.VMEM((ATTN_WIDTH // LANES, tm, LANES), F32)],
        compiler_params=_cparams("parallel"),
        name="mix_route",
    )(*outs, *lses, yf, yb, xc, z, x2, mod, expand, jnp.repeat(d_skip, SSD_HEAD_DIM).reshape(1, -1),
      g_ssd.reshape(1, -1), w_out.astype(BF16), g_post.reshape(1, -1), g_pre_ffn.reshape(1, -1), wr, br)


_MOE_TM = 256
_ROUTE_TM = 256
_MOE_CAP = 32
_BF16_ROWS = 2 * SUBLANES
_MOE_WIN = _MOE_CAP + _BF16_ROWS
_MOE_TAIL = _MOE_TM
_MOE_SLOTS = N_EXPERTS + _ROUTE_TM * TOP_K // _MOE_CAP
_MOE_KBLOCK = 16
_ROW_SPLIT = D_MODEL // LANES


def _store_rows(ref, lead, row0, vals):
    n = vals.shape[0]
    for q in range(_ROW_SPLIT):
        ref[lead, pl.ds(row0 * _ROW_SPLIT + q, n, stride=_ROW_SPLIT), :] = vals[:, q * LANES:(q + 1) * LANES]


def _load_rows(ref, lead, row0, n):
    parts = [ref[lead, pl.ds(row0 * _ROW_SPLIT + q, n, stride=_ROW_SPLIT), :] for q in range(_ROW_SPLIT)]
    return jnp.concatenate(parts, axis=1)


def _sorted_rows(t_total):
    return t_total * TOP_K + N_EXPERTS * _MOE_CAP + _MOE_TAIL


def _dispatch_kernel(src_ref, dst_ref, num_ref, nhalf_ref, pad_ref, route_ref, h_ref, xs_ref, stage_ref, sem,
                     *, nt):
    i = pl.program_id(0)
    slot = i % 2
    cap = _MOE_CAP
    n_pairs = _ROUTE_TM * TOP_K
    chunk_rows = cap * _ROW_SPLIT
    per_tile = _MOE_SLOTS + N_EXPERTS

    def chunk_copy(step, j, s, rows):
        src = pl.multiple_of(src_ref[step * per_tile + j] * _ROW_SPLIT, _ROW_SPLIT)
        dst = pl.multiple_of(dst_ref[step * per_tile + j] * _ROW_SPLIT, _ROW_SPLIT)
        return pltpu.make_async_copy(stage_ref.at[s, pl.ds(src, rows * _ROW_SPLIT)],
                                     xs_ref.at[pl.ds(dst, rows * _ROW_SPLIT)], sem.at[s])

    def for_chunks(step, s, fn):
        def full(j, carry):
            fn(chunk_copy(step, j, s, cap))
            return carry

        def half(j, carry):
            fn(chunk_copy(step, _MOE_SLOTS + j, s, cap // 2))
            return carry

        lax.fori_loop(0, num_ref[step], full, 0)
        lax.fori_loop(0, nhalf_ref[step], half, 0)

    @pl.when(i == 0)
    def _():
        for s in range(2):
            stage_ref[s, n_pairs * _ROW_SPLIT:, :] = jnp.zeros((chunk_rows, LANES), F32)

    @pl.when(i < nt)
    def _():
        pos_t = route_ref[...].T[TOP_K:2 * TOP_K, :]
        hb = h_ref[...]
        blk = _ROUTE_TM
        for b in range(n_pairs // blk):
            jrow = (lax.broadcasted_iota(jnp.int32, (blk, _ROUTE_TM), 0) + b * blk).astype(F32)
            onehot = jnp.zeros((blk, _ROUTE_TM), F32)
            for kk in range(TOP_K):
                onehot = onehot + jnp.where(pos_t[kk:kk + 1, :] == jrow, 1.0, 0.0)
            _store_rows(stage_ref, slot, b * blk, _dot(onehot.astype(BF16), hb))

    @pl.when(i > 0)
    def _():
        for_chunks(i - 1, 1 - slot, lambda cp: cp.wait())

    for_chunks(i, slot, lambda cp: cp.start())

    @pl.when(i == nt)
    def _():
        n_rows = xs_ref.shape[0] // _ROW_SPLIT
        zeros = stage_ref.at[slot, pl.ds(0, chunk_rows)]
        zeros[...] = jnp.zeros((chunk_rows, LANES), F32)
        fills = [pad_ref[e] for e in range(N_EXPERTS)]
        fills += [min(n_rows - _MOE_TAIL + j * cap, n_rows - cap) for j in range(-(-_MOE_TAIL // cap))]
        copies = [pltpu.make_async_copy(zeros, xs_ref.at[pl.ds(pl.multiple_of(r * _ROW_SPLIT, _ROW_SPLIT),
                                                               chunk_rows)], sem.at[slot]) for r in fills]
        for cp in copies:
            cp.start()
        for cp in copies:
            cp.wait()


def _dispatch(h2, route, lists):
    t_total = h2.shape[0]
    nt = t_total // _ROUTE_TM
    n_rows = _sorted_rows(t_total)
    tile = lambda i, *_: (jnp.minimum(i, nt - 1), 0)
    grid_spec = pltpu.PrefetchScalarGridSpec(
        num_scalar_prefetch=5,
        grid=(nt + 1,),
        in_specs=[pl.BlockSpec((_ROUTE_TM, LANES), tile),
                  pl.BlockSpec((_ROUTE_TM, D_MODEL), tile)],
        out_specs=pl.BlockSpec(memory_space=pl.ANY),
        scratch_shapes=[pltpu.VMEM((2, (_ROUTE_TM * TOP_K + _MOE_CAP) * _ROW_SPLIT, LANES), F32),
                        pltpu.SemaphoreType.DMA((2,))])
    return pl.pallas_call(
        functools.partial(_dispatch_kernel, nt=nt),
        grid_spec=grid_spec,
        out_shape=jax.ShapeDtypeStruct((n_rows * _ROW_SPLIT, LANES), F32),
        compiler_params=_cparams("arbitrary"),
        name="moe_dispatch",
    )(lists["stage_row"], lists["sorted_row"], lists["num_full"], lists["num_half"], lists["pad_row"], route, h2)


def _gmm_kernel(tile_ref, exp_ref, valid_ref, offs_ref, x_ref, wgu_hbm, bgu_ref, wdn_hbm, bdn_ref, y_ref,
                wgu_bf, wdn_bf, wgu_f32, wdn_f32, sem):
    j = pl.program_id(0)
    e = exp_ref[j]
    tile = tile_ref[j]
    prev_j = jnp.maximum(j - 1, 0)
    new_expert = (j == 0) | (exp_ref[prev_j] != e)
    new_tile = (j == 0) | (tile_ref[prev_j] != tile)

    def fetch(expert, s):
        return (pltpu.make_async_copy(wgu_hbm.at[expert], wgu_f32.at[s], sem.at[0, s]),
                pltpu.make_async_copy(wdn_hbm.at[expert], wdn_f32.at[s], sem.at[1, s]))

    @pl.when(j == 0)
    def _():
        for cp in fetch(e, e % 2):
            cp.start()

    @pl.when(new_expert)
    def _():
        s = e % 2
        for cp in fetch(e, s):
            cp.wait()
        wgu_bf[...] = wgu_f32[s].astype(BF16)
        wdn_bf[...] = wdn_f32[s].astype(BF16)

        @pl.when(e + 1 < N_EXPERTS)
        def _():
            for cp in fetch(e + 1, 1 - s):
                cp.start()

    @pl.when(valid_ref[j] != 0)
    def _():
        half = D_MODEL // 2
        gu = bgu_ref[0]
        for kc in range(2):
            parts = [x_ref[0, pl.ds(q, _MOE_TM, stride=_ROW_SPLIT), :]
                     for q in range(kc * _ROW_SPLIT // 2, (kc + 1) * _ROW_SPLIT // 2)]
            xk = jnp.concatenate(parts, axis=1).astype(BF16)
            gu = gu + _dot(xk, wgu_bf[kc * half:(kc + 1) * half, :])
        gate = jnp.minimum(gu[:, :EXPERT_FF], SWIGLU_LIMIT)
        up = jnp.clip(gu[:, EXPERT_FF:], -SWIGLU_LIMIT, SWIGLU_LIMIT)
        act = (up + 1.0) * gate * _sigmoid(SWIGLU_ALPHA * gate)
        y = (_dot(act.astype(BF16), wdn_bf[...]) + bdn_ref[0]).astype(BF16)
        rows = tile * _MOE_TM + lax.broadcasted_iota(jnp.int32, (_MOE_TM, 1), 0)
        mine = (rows >= offs_ref[e]) & (rows < offs_ref[e + 1])

        @pl.when(new_tile)
        def _():
            y_ref[...] = jnp.where(mine, y, jnp.zeros_like(y))

        @pl.when(jnp.logical_not(new_tile))
        def _():
            y_ref[...] = jnp.where(mine, y, y_ref[...])


def _grouped_mlp(xs, offs, w_gate_up, b_gate_up, w_down, b_down):
    n_tiles = xs.shape[0] // (_MOE_TM * _ROW_SPLIT)
    n_visits = n_tiles + N_EXPERTS - 1
    first = offs[:-1] // _MOE_TM
    last = (offs[1:] - 1) // _MOE_TM
    per_expert = last - first + 1
    vstart = jnp.concatenate([jnp.zeros((1,), jnp.int32), jnp.cumsum(per_expert)]).astype(jnp.int32)
    total = vstart[-1]
    vis = jnp.minimum(jnp.arange(n_visits, dtype=jnp.int32), total - 1)
    exp_ids = (jnp.sum(vstart[None, :] <= vis[:, None], axis=1) - 1).astype(jnp.int32)
    mine = exp_ids[:, None] == jnp.arange(N_EXPERTS, dtype=jnp.int32)[None, :]
    pick = lambda a: jnp.sum(jnp.where(mine, a[None, :], 0), axis=1)
    tile_ids = (pick(first) + vis - pick(vstart[:-1])).astype(jnp.int32)
    valid = (jnp.arange(n_visits) < total).astype(jnp.int32)
    grid_spec = pltpu.PrefetchScalarGridSpec(
        num_scalar_prefetch=4,
        grid=(n_visits,),
        in_specs=[pl.BlockSpec((1, _MOE_TM * _ROW_SPLIT, LANES), lambda j, t, e, v, o: (t[j], 0, 0)),
                  pl.BlockSpec(memory_space=pl.ANY),
                  pl.BlockSpec((1, 1, 2 * EXPERT_FF), lambda j, t, e, v, o: (e[j], 0, 0)),
                  pl.BlockSpec(memory_space=pl.ANY),
                  pl.BlockSpec((1, 1, D_MODEL), lambda j, t, e, v, o: (e[j], 0, 0))],
        out_specs=pl.BlockSpec((_MOE_TM, D_MODEL), lambda j, t, e, v, o: (t[j], 0)),
        scratch_shapes=[pltpu.VMEM((D_MODEL, 2 * EXPERT_FF), BF16), pltpu.VMEM((EXPERT_FF, D_MODEL), BF16),
                        pltpu.VMEM((2, D_MODEL, 2 * EXPERT_FF), F32), pltpu.VMEM((2, EXPERT_FF, D_MODEL), F32),
                        pltpu.SemaphoreType.DMA((2, 2))])
    tiled = (n_tiles, _MOE_TM * _ROW_SPLIT, LANES)
    return pl.pallas_call(
        _gmm_kernel,
        grid_spec=grid_spec,
        out_shape=jax.ShapeDtypeStruct((n_tiles * _MOE_TM, D_MODEL), BF16),
        compiler_params=_cparams("arbitrary"),
        name="moe_mlp",
    )(tile_ids, exp_ids, valid, offs, xs.reshape(tiled), w_gate_up, b_gate_up.reshape(N_EXPERTS, 1, -1),
      w_down, b_down.reshape(N_EXPERTS, 1, -1))


def _combine_kernel(win_ref, num_ref, ys_ref, route_ref, base_ref, res_ref, mod_ref, g_ref,
                    o_ref, ybuf_ref, acc_ref, sem, *, nt):
    i = pl.program_id(0)
    slot = i % 2
    cap = _MOE_CAP
    win = _MOE_WIN
    tm = _ROUTE_TM
    kcols = _MOE_KBLOCK * win

    def window_copy(step, j, s):
        src = pl.multiple_of(win_ref[step * _MOE_SLOTS + j], _BF16_ROWS)
        return pltpu.make_async_copy(ys_ref.at[pl.ds(src, win)],
                                     ybuf_ref.at[s, pl.ds(pl.multiple_of(j * win, _BF16_ROWS), win)], sem.at[s])

    def start_all(step, s):
        def body(j, carry):
            window_copy(step, j, s).start()
            return carry
        lax.fori_loop(0, num_ref[step], body, 0)

    @pl.when(i == 0)
    def _():
        ybuf_ref[...] = jnp.zeros_like(ybuf_ref)
        start_all(0, 0)

    @pl.when(i + 1 < nt)
    def _():
        start_all(i + 1, 1 - slot)

    def wait_own(j, carry):
        window_copy(i, j, slot).wait()
        return carry

    lax.fori_loop(0, num_ref[i], wait_own, 0)

    route = route_ref[...]
    lane = lax.broadcasted_iota(jnp.int32, (tm, LANES), 1).astype(F32)
    base = base_ref[0, 0:1, :]
    cols, wts = [], []
    for kk in range(TOP_K):
        expert = route[:, kk:kk + 1]
        rank = route[:, 2 * TOP_K + kk:2 * TOP_K + kk + 1]
        chunk = jnp.floor(rank * (1.0 / cap))
        first = jnp.sum(jnp.where(lane == expert, base, 0.0), axis=-1, keepdims=True)
        cols.append(first + chunk * win + (rank - chunk * cap))
        wts.append(route[:, 3 * TOP_K + kk:3 * TOP_K + kk + 1])

    for b in range(_MOE_SLOTS // _MOE_KBLOCK):
        @pl.when(b * _MOE_KBLOCK < num_ref[i])
        def _():
            col = (lax.broadcasted_iota(jnp.int32, (tm, kcols), 1) + b * kcols).astype(F32)
            pw = jnp.zeros((tm, kcols), F32)
            for kk in range(TOP_K):
                pw = pw + jnp.where(col == cols[kk], wts[kk], 0.0)
            part = _dot(pw.astype(BF16), ybuf_ref[slot, b * kcols:(b + 1) * kcols, :])
            if b == 0:
                acc_ref[...] = part
            else:
                acc_ref[...] += part

    ffn = acc_ref[...] * (1.0 / (wts[0] + wts[1] + wts[2] + wts[3]))
    gate_f = mod_ref[0, 5:6, :]
    o_ref[...] = res_ref[...] + gate_f * (_rms(ffn) * g_ref[...])


def _combine(ys, route, lists, res, mod, g_post_ffn, *, seq):
    t_total = res.shape[0]
    nt = t_total // _ROUTE_TM
    tiles_per_seq = seq // _ROUTE_TM
    row = lambda i, *_: (i, 0)
    grid_spec = pltpu.PrefetchScalarGridSpec(
        num_scalar_prefetch=2,
        grid=(nt,),
        in_specs=[pl.BlockSpec(memory_space=pl.ANY),
                  pl.BlockSpec((_ROUTE_TM, LANES), row),
                  pl.BlockSpec((1, SUBLANES, LANES), lambda i, *_: (i, 0, 0)),
                  pl.BlockSpec((_ROUTE_TM, D_MODEL), row),
                  pl.BlockSpec((1, N_MOD, D_MODEL), lambda i, *_: (i // tiles_per_seq, 0, 0)),
                  pl.BlockSpec((1, D_MODEL), lambda i, *_: (0, 0))],
        out_specs=pl.BlockSpec((_ROUTE_TM, D_MODEL), row),
        scratch_shapes=[pltpu.VMEM((2, _MOE_SLOTS * _MOE_WIN, D_MODEL), BF16),
                        pltpu.VMEM((_ROUTE_TM, D_MODEL), F32),
                        pltpu.SemaphoreType.DMA((2,))])
    return pl.pallas_call(
        functools.partial(_combine_kernel, nt=nt),
        grid_spec=grid_spec,
        out_shape=jax.ShapeDtypeStruct((t_total, D_MODEL), F32),
        compiler_params=_cparams("arbitrary"),
        name="moe_combine",
    )(lists["window_row"], lists["num"], ys, route, lists["window_base"], res, mod, g_post_ffn.reshape(1, -1))


def _chunk_lists(n_te, offs):
    nt = n_te.shape[0]
    cap = _MOE_CAP
    zero_col = jnp.zeros((nt, 1), jnp.int32)
    zero_row = jnp.zeros((1, N_EXPERTS), jnp.int32)
    before = jnp.concatenate([zero_row, jnp.cumsum(n_te, axis=0)[:-1]], axis=0)
    seg = offs[None, :N_EXPERTS] + before
    first = jnp.concatenate([zero_col, jnp.cumsum(n_te, axis=1)[:, :-1]], axis=1)
    experts = jnp.arange(N_EXPERTS, dtype=jnp.int32)

    def slot_lists(count, n_slots):
        base = jnp.concatenate([zero_col, jnp.cumsum(count, axis=1)], axis=1)
        slots = jnp.arange(n_slots, dtype=jnp.int32)
        owner = jnp.minimum(jnp.sum(base[:, 1:, None] <= slots[None, None, :], axis=1), N_EXPERTS - 1)
        pick = owner[:, None, :] == experts[None, :, None]
        take = lambda a: jnp.sum(jnp.where(pick, a[:, :, None], 0), axis=1)
        return base, take, slots[None, :] - take(base[:, :N_EXPERTS])

    as_i32 = lambda a: a.astype(jnp.int32).reshape(-1)
    with_end = lambda a: jnp.concatenate([a, jnp.zeros((1,), jnp.int32)])
    chunk_base, take, chunk = slot_lists((n_te + cap - 1) // cap, _MOE_SLOTS)
    window_row = (take(seg) + chunk * cap) // _BF16_ROWS * _BF16_ROWS
    window_base = chunk_base[:, :N_EXPERTS] * _MOE_WIN + seg % _BF16_ROWS
    window_base = jnp.zeros((nt, SUBLANES, LANES), F32).at[:, :, :N_EXPERTS].set(
        window_base.astype(F32)[:, None, :])
    rem = n_te % cap
    has_half = (rem > 0) & (rem <= cap // 2)
    n_full = n_te // cap + (rem > cap // 2)
    _, take_f, chunk_f = slot_lists(n_full, _MOE_SLOTS)
    half_base, take_h, _ = slot_lists(has_half.astype(jnp.int32), N_EXPERTS)
    stage_row = jnp.concatenate([take_f(first) + chunk_f * cap, take_h(first + n_full * cap)], axis=1)
    sorted_row = jnp.concatenate([take_f(seg) + chunk_f * cap, take_h(seg + n_full * cap)], axis=1)
    return {"stage_row": as_i32(stage_row),
            "sorted_row": as_i32(sorted_row),
            "num_full": as_i32(with_end(jnp.sum(n_full, axis=1))),
            "num_half": as_i32(with_end(half_base[:, -1])),
            "window_row": as_i32(window_row),
            "num": as_i32(with_end(chunk_base[:, -1])),
            "pad_row": as_i32(offs[1:] - cap).at[N_EXPERTS - 1].add(-_MOE_TAIL),
            "window_base": window_base}


def _forward(x, c, w_ada, b_ada, g_pre_mix, g_post_mix, w_in, conv_w, conv_b, dt_bias, a_log, d_skip,
             g_ssd_norm, w_out, g_pre_ffn, g_post_ffn, w_router, b_router, w_gate_up, b_gate_up, w_down, b_down):
    bsz, seq, _ = x.shape
    x2 = x.astype(F32).reshape(bsz * seq, D_MODEL)
    mod = _ada_mod(c.astype(F32), w_ada, b_ada)
    *qkv, z, xc, dt = _in_proj(x2, mod, g_pre_mix, w_in, conv_w, conv_b, dt_bias, seq=seq)
    attn = []
    for gi, dil in enumerate(DILATIONS):
        q, k, v = (a.reshape(bsz * seq, ATTN_WIDTH) for a in qkv[3 * gi:3 * gi + 3])
        if gi == 0:
            o, lse, yf, yb = _dilated_attention(q, k, v, dil, seq=seq, ssd=(xc, dt, a_log))
            attn.append((o, lse))
        else:
            attn.append(_dilated_attention(q, k, v, dil, seq=seq))
    res, h2, route, cnt = _mix_and_route(
        [o for o, _ in attn], [l for _, l in attn], yf, yb, xc, z, x2, mod, d_skip, g_ssd_norm, w_out,
        g_post_mix, g_pre_ffn, w_router, b_router, seq=seq, tm=_ROUTE_TM)
    n_te = cnt[:, 0, :N_EXPERTS]
    region = jnp.sum(n_te, axis=0) + _MOE_CAP
    region = region.at[N_EXPERTS - 1].add(_MOE_TAIL)
    offs = jnp.concatenate([jnp.zeros((1,), jnp.int32), jnp.cumsum(region)]).astype(jnp.int32)
    lists = _chunk_lists(n_te, offs)
    xs = _dispatch(h2, route, lists)
    ys = _grouped_mlp(xs, offs, w_gate_up, b_gate_up, w_down, b_down)
    out = _combine(ys, route, lists, res, mod, g_post_ffn, seq=seq)
    return out.reshape(bsz, seq, D_MODEL)


def kernel(x, c, w_ada, b_ada, g_pre_mix, g_post_mix, w_in, conv_w, conv_b, dt_bias, a_log, d_skip, g_ssd_norm, w_out, g_pre_ffn, g_post_ffn, w_router, b_router, w_gate_up, b_gate_up, w_down, b_down):
    layer = lambda t: t[0].astype(F32)
    out = _forward(x, c, layer(w_ada), layer(b_ada), layer(g_pre_mix), layer(g_post_mix), layer(w_in),
                   layer(conv_w), layer(conv_b), layer(dt_bias), layer(a_log), layer(d_skip),
                   layer(g_ssd_norm), layer(w_out), layer(g_pre_ffn), layer(g_post_ffn), layer(w_router),
                   layer(b_router), layer(w_gate_up), layer(b_gate_up), layer(w_down), layer(b_down))
    return out.astype(x.dtype)
```

```python
import functools

import numpy as np
import jax
import jax.numpy as jnp
from jax import lax
from jax.experimental import pallas as pl
from jax.experimental.pallas import tpu as pltpu

F32 = jnp.float32
BF16 = jnp.bfloat16

D_MODEL = 1024
ATTN_HEADS = 16
HEAD_DIM = 64
ATTN_WIDTH = ATTN_HEADS * HEAD_DIM
DILATIONS = (1, 4, 16)
ATTN_HALF = 64
SSD_HEADS = 16
SSD_HEAD_DIM = 64
SSD_WIDTH = SSD_HEADS * SSD_HEAD_DIM
SSD_GROUPS = 2
SSD_STATE = 128
SSD_CHUNK = 128
CONV_WIDTH = 5
XBC_WIDTH = SSD_WIDTH + 2 * SSD_GROUPS * SSD_STATE
N_EXPERTS = 32
TOP_K = 4
EXPERT_FF = 1024
SWIGLU_ALPHA = 1.702
SWIGLU_LIMIT = 7.0
RMS_EPS = 1e-6
N_MOD = 6
LANES = 128
SUBLANES = 8
VMEM_LIMIT = 56 * 1024 * 1024


def _cparams(*sem):
    return pltpu.CompilerParams(dimension_semantics=sem, vmem_limit_bytes=VMEM_LIMIT)


def _const_spec(shape):
    nd = len(shape)
    return pl.BlockSpec(shape, lambda *_: (0,) * nd)


def _split2(a):
    hi = a.astype(BF16)
    lo = (a - hi.astype(F32)).astype(BF16)
    return hi, lo


def _dot(a, b):
    return jnp.dot(a, b, preferred_element_type=F32)


def _dot_f32(a, b):
    ah, al = _split2(a)
    bh, bl = _split2(b)
    return _dot(ah, bh) + (_dot(ah, bl) + _dot(al, bh))


def _sigmoid(x):
    return 1.0 / (1.0 + jnp.exp(-x))


def _rms(x):
    return x * lax.rsqrt(jnp.mean(x * x, axis=-1, keepdims=True) + RMS_EPS)


def _ada_kernel(c_ref, w_ref, b_ref, o_ref):
    c = c_ref[...]
    o_ref[...] = _dot_f32(c * _sigmoid(c), w_ref[...]) + b_ref[...]


def _ada_mod(c, w_ada, b_ada):
    bsz = c.shape[0]
    c8 = jnp.zeros((SUBLANES, D_MODEL), F32).at[:bsz].set(c)
    out = pl.pallas_call(
        _ada_kernel,
        grid=(N_MOD,),
        in_specs=[_const_spec((SUBLANES, D_MODEL)),
                  pl.BlockSpec((D_MODEL, D_MODEL), lambda j: (0, j)),
                  pl.BlockSpec((1, D_MODEL), lambda j: (0, j))],
        out_specs=pl.BlockSpec((SUBLANES, D_MODEL), lambda j: (0, j)),
        out_shape=jax.ShapeDtypeStruct((SUBLANES, N_MOD * D_MODEL), F32),
        compiler_params=_cparams("parallel"),
        name="ada_mod",
    )(c8, w_ada, b_ada.reshape(1, -1))
    return out[:bsz].reshape(bsz, N_MOD, D_MODEL)


def _inproj_kernel(x_ref, xp_ref, xn_ref, mod_ref, g_ref, wqkv_ref, wz_ref, wxbc_ref, wdt_ref,
                   cw_ref, cb_ref, dtb_ref,
                   q1_ref, k1_ref, v1_ref, q4_ref, k4_ref, v4_ref, q16_ref, k16_ref, v16_ref,
                   z_ref, xc_ref, dt_ref, buf_ref, slab_ref, *, tm, tiles_per_seq):
    i = pl.program_id(0)
    shift = mod_ref[0, 0:1, :]
    scale = mod_ref[0, 1:2, :]
    g = g_ref[...]

    def norm_mod(x):
        return _rms(x) * g * (1.0 + scale) + shift

    h = norm_mod(x_ref[...])
    hb = h.astype(BF16)
    hh = norm_mod(jnp.concatenate([xp_ref[...], xn_ref[...]], axis=0)).astype(BF16)
    halo = _dot(hh, wxbc_ref[...])
    t_in_seq = i % tiles_per_seq
    prev_ok = (t_in_seq != 0).astype(F32)
    next_ok = (t_in_seq != tiles_per_seq - 1).astype(F32)
    buf_ref[0:SUBLANES, :] = halo[0:SUBLANES] * prev_ok
    buf_ref[SUBLANES + tm:, :] = halo[SUBLANES:] * next_ok
    buf_ref[SUBLANES:SUBLANES + tm, :] = _dot(hb, wxbc_ref[...])

    n_slab = ATTN_WIDTH // LANES
    outs = ((q1_ref, q4_ref, q16_ref), (k1_ref, k4_ref, k16_ref), (v1_ref, v4_ref, v16_ref))
    conv_cols = XBC_WIDTH // len(outs)
    for c, (nat_ref, d4_ref, d16_ref) in enumerate(outs):
        r = _dot(hb, wqkv_ref[:, c * ATTN_WIDTH:(c + 1) * ATTN_WIDTH])
        if c == 0:
            r = r * HEAD_DIM ** -0.5
        nat_ref[...] = r.astype(BF16)
        for s in range(n_slab):
            slab_ref[0, s] = r[:, s * LANES:(s + 1) * LANES]
        step = DILATIONS[1]
        sub = tm // step
        for r4 in range(step):
            for s in range(n_slab):
                rows = slab_ref[0, s, pl.ds(r4, sub, stride=step), :]
                d4_ref[0, r4, :, s * LANES:(s + 1) * LANES] = rows.astype(BF16)
                slab_ref[1, s, r4 * sub:(r4 + 1) * sub, :] = rows
        for r4 in range(step):
            for s4 in range(step):
                for s in range(n_slab):
                    rows = slab_ref[1, s, pl.ds(r4 * sub + s4, sub // step, stride=step), :]
                    d16_ref[0, step * s4 + r4, :, s * LANES:(s + 1) * LANES] = rows.astype(BF16)
        cols = slice(c * conv_cols, (c + 1) * conv_cols)
        acc = cb_ref[:, cols]
        for j in range(CONV_WIDTH):
            off = SUBLANES - CONV_WIDTH // 2 + j
            acc = acc + buf_ref[off:off + tm, cols] * cw_ref[j:j + 1, cols]
        xc_ref[:, cols] = (acc * _sigmoid(acc)).astype(BF16)
    z_ref[...] = _dot(hb, wz_ref[...]).astype(BF16)
    dt_ref[...] = jax.nn.softplus(_dot_f32(h, wdt_ref[...]) + dtb_ref[...])


def _in_proj(x2, mod, g_pre, w_in, conv_w, conv_b, dt_bias, *, seq, tm=512):
    t_total = x2.shape[0]
    nt = t_total // tm
    tiles_per_seq = seq // tm
    hb = tm // SUBLANES
    n_hblk = t_total // SUBLANES
    qkv_w = 3 * ATTN_WIDTH
    w_qkv = w_in[:, :qkv_w].astype(BF16)
    w_z = w_in[:, qkv_w:qkv_w + SSD_WIDTH].astype(BF16)
    w_xbc = w_in[:, qkv_w + SSD_WIDTH:qkv_w + SSD_WIDTH + XBC_WIDTH].astype(BF16)
    w_dt_raw = w_in[:, qkv_w + SSD_WIDTH + XBC_WIDTH:]
    w_dt = jnp.zeros((D_MODEL, 2 * LANES), F32)
    dtb = jnp.zeros((1, 2 * LANES), F32)
    for dr in range(2):
        w_dt = w_dt.at[:, dr * LANES:dr * LANES + SSD_HEADS].set(w_dt_raw[:, dr * SSD_HEADS:(dr + 1) * SSD_HEADS])
        dtb = dtb.at[0, dr * LANES:dr * LANES + SSD_HEADS].set(dt_bias[dr])
    row = lambda i: (i, 0)
    bsz = t_total // seq
    dil_shapes, dil_specs = [], []
    for dil in DILATIONS[1:]:
        dil_shapes += [jax.ShapeDtypeStruct((bsz, dil, seq // dil, ATTN_WIDTH), BF16)] * 3
        dil_specs += [pl.BlockSpec((1, dil, tm // dil, ATTN_WIDTH),
                                   lambda i: (i // tiles_per_seq, 0, i % tiles_per_seq, 0))] * 3
    out_shape = [jax.ShapeDtypeStruct((t_total, ATTN_WIDTH), BF16)] * 3 + dil_shapes + [
        jax.ShapeDtypeStruct((t_total, SSD_WIDTH), BF16),
        jax.ShapeDtypeStruct((t_total, XBC_WIDTH), BF16),
        jax.ShapeDtypeStruct((t_total, 2 * LANES), F32)]
    return pl.pallas_call(
        functools.partial(_inproj_kernel, tm=tm, tiles_per_seq=tiles_per_seq),
        grid=(nt,),
        in_specs=[pl.BlockSpec((tm, D_MODEL), row),
                  pl.BlockSpec((SUBLANES, D_MODEL), lambda i: (jnp.maximum(i * hb - 1, 0), 0)),
                  pl.BlockSpec((SUBLANES, D_MODEL), lambda i: (jnp.minimum((i + 1) * hb, n_hblk - 1), 0)),
                  pl.BlockSpec((1, N_MOD, D_MODEL), lambda i: (i // tiles_per_seq, 0, 0)),
                  _const_spec((1, D_MODEL)),
                  _const_spec((D_MODEL, qkv_w)),
                  _const_spec((D_MODEL, SSD_WIDTH)),
                  _const_spec((D_MODEL, XBC_WIDTH)),
                  _const_spec((D_MODEL, 2 * LANES)),
                  _const_spec((CONV_WIDTH, XBC_WIDTH)),
                  _const_spec((1, XBC_WIDTH)),
                  _const_spec((1, 2 * LANES))],
        out_specs=[pl.BlockSpec((tm, ATTN_WIDTH), row)] * 3 + dil_specs + [
            pl.BlockSpec((tm, SSD_WIDTH), row),
            pl.BlockSpec((tm, XBC_WIDTH), row),
            pl.BlockSpec((tm, 2 * LANES), row)],
        out_shape=out_shape,
        scratch_shapes=[pltpu.VMEM((tm + 2 * SUBLANES, XBC_WIDTH), F32),
                        pltpu.VMEM((2, ATTN_WIDTH // LANES, tm, LANES), F32)],
        compiler_params=_cparams("parallel"),
        name="in_proj",
    )(x2, x2, x2, mod, g_pre.reshape(1, -1), w_qkv, w_z, w_xbc, w_dt, conv_w, conv_b.reshape(1, -1), dtb)


_NEG = -1e30


def _ssd_direction(xc_ref, dt_ref, a_ref, e_ref, h_ref, y_ref, reverse):
    qn = SSD_CHUNK
    gw = SSD_WIDTH // SSD_GROUPS
    li = lax.broadcasted_iota(jnp.int32, (qn, qn), 0)
    si = lax.broadcasted_iota(jnp.int32, (qn, qn), 1)
    mask = (si >= li) if reverse else (si <= li)
    tri = jnp.where(mask, 1.0, 0.0).astype(BF16)
    dt = dt_ref[...]
    adt = dt * a_ref[...]
    p0 = adt.astype(BF16)
    r0 = adt - p0.astype(F32)
    p1 = r0.astype(BF16)
    p2 = (r0 - p1.astype(F32)).astype(BF16)
    acum = _dot(tri, p0) + (_dot(tri, p1) + _dot(tri, p2))
    last = 0 if reverse else qn - 1
    eo = jnp.exp(acum)
    ds = jnp.exp(acum[last:last + 1, :] - acum)
    dt_h, dt_l = _split2(dt)
    eo_h, eo_l = _split2(eo)
    ex = _dot(jnp.concatenate([dt_h, eo_h, ds.astype(BF16)], axis=0), e_ref[...])
    ex_l = _dot(jnp.concatenate([dt_l, eo_l], axis=0), e_ref[...])
    dt_x = ex[0:qn] + ex_l[0:qn]
    eo_x = ex[qn:2 * qn] + ex_l[qn:]
    ds_x = ex[2 * qn:]
    xd = xc_ref[:, 0:SSD_WIDTH].astype(F32) * dt_x
    xdb = xd.astype(BF16)
    xwb = (xd * ds_x).astype(BF16)
    acum_t = acum.T
    lane = lax.broadcasted_iota(jnp.int32, (qn, LANES), 1)
    for g in range(SSD_GROUPS):
        b0 = SSD_WIDTH + g * SSD_STATE
        c0 = SSD_WIDTH + SSD_GROUPS * SSD_STATE + g * SSD_STATE
        bm = xc_ref[:, b0:b0 + SSD_STATE]
        cm = xc_ref[:, c0:c0 + SSD_STATE]
        cb = lax.dot_general(cm, bm, (((1,), (1,)), ((), ())), preferred_element_type=F32)
        hg = h_ref[g]
        yoff = _dot(cm, hg.astype(BF16))
        st = lax.dot_general(bm, xwb[:, g * gw:(g + 1) * gw], (((0,), (0,)), ((), ())),
                             preferred_element_type=F32)
        h_ref[g] = hg * eo_x[last:last + 1, g * gw:(g + 1) * gw] + st
        for pr in range(gw // LANES):
            col = g * gw + pr * LANES
            xp = xdb[:, col:col + LANES]
            ms = []
            for e in (col // SSD_HEAD_DIM, col // SSD_HEAD_DIM + 1):
                seg = acum[:, e:e + 1] - acum_t[e:e + 1, :]
                ms.append((cb * jnp.exp(jnp.where(mask, seg, _NEG))).astype(BF16))
            zero = jnp.zeros_like(xp)
            xp2 = jnp.concatenate([jnp.where(lane < SSD_HEAD_DIM, xp, zero),
                                   jnp.where(lane < SSD_HEAD_DIM, zero, xp)], axis=0)
            yd = _dot(jnp.concatenate(ms, axis=1), xp2)
            y = yd + yoff[:, pr * LANES:(pr + 1) * LANES] * eo_x[:, col:col + LANES]
            y_ref[:, col:col + LANES] = y.astype(BF16)


_ATTN_QB = 2 * ATTN_HALF
_ATTN_WIN = _ATTN_QB + 2 * ATTN_HALF
_ATTN_LQ = 512
_ATTN_UNROLL = 4
_ATTN_SSD_UNROLL = 2


def _attn_kernel(*refs, lq, sub_len, with_ssd):
    if with_ssd:
        (q_ref, k_ref, kp_ref, kn_ref, v_ref, vp_ref, vn_ref, bias_ref, xcf_ref, xcb_ref, dtf_ref, dtb_ref,
         a_ref, e_ref, o_ref, lse_ref, yf_ref, yb_ref, kw_ref, vw_ref, hf_ref, hb_ref) = refs
    else:
        q_ref, k_ref, kp_ref, kn_ref, v_ref, vp_ref, vn_ref, bias_ref, o_ref, lse_ref, kw_ref, vw_ref = refs
    t = pl.program_id(2)
    if with_ssd:
        @pl.when(t == 0)
        def _():
            hf_ref[...] = jnp.zeros_like(hf_ref)
            hb_ref[...] = jnp.zeros_like(hb_ref)
    hb = ATTN_HALF
    for src, halo_p, halo_n, win in ((k_ref, kp_ref, kn_ref, kw_ref), (v_ref, vp_ref, vn_ref, vw_ref)):
        win[0:hb, :] = halo_p[...]
        win[hb:hb + lq, :] = src[...]
        win[hb + lq:, :] = halo_n[...]
    qn = _ATTN_QB
    lane = lax.broadcasted_iota(jnp.int32, (qn, LANES), 1)
    first_head = lane < HEAD_DIM
    kpos = lax.broadcasted_iota(jnp.int32, (1, _ATTN_WIN), 1)

    def body(qb, carry):
        r0 = pl.multiple_of(qb * qn, qn)
        kidx = t * lq + r0 - hb + kpos
        in_seq = (kidx >= 0) & (kidx < sub_len)
        lse_tile = jnp.zeros((qn, LANES), F32)
        for hp in range(ATTN_HEADS // 2):
            cs = slice(hp * LANES, (hp + 1) * LANES)
            q = q_ref[pl.ds(r0, qn), cs]
            zero = jnp.zeros_like(q)
            q2 = jnp.concatenate([jnp.where(first_head, q, zero), jnp.where(first_head, zero, q)], axis=0)
            s = lax.dot_general(q2, kw_ref[pl.ds(r0, _ATTN_WIN), cs], (((1,), (1,)), ((), ())),
                                preferred_element_type=F32) + bias_ref[hp]
            s = jnp.where(in_seq, s, _NEG)
            m = jnp.max(s, axis=-1, keepdims=True)
            p = jnp.exp(s - m)
            den = jnp.sum(p, axis=-1, keepdims=True)
            pv = _dot(p.astype(BF16), vw_ref[pl.ds(r0, _ATTN_WIN), cs]) * (1.0 / den)
            o_ref[pl.ds(r0, qn), cs] = jnp.where(first_head, pv[0:qn], pv[qn:]).astype(BF16)
            lse = m + jnp.log(den)
            lse_tile = jnp.where(lane == 2 * hp, lse[0:qn], jnp.where(lane == 2 * hp + 1, lse[qn:], lse_tile))
        lse_ref[pl.ds(r0, qn), :] = lse_tile
        if with_ssd:
            b0 = pl.multiple_of((lq // qn - 1 - qb) * qn, qn)
            _ssd_direction(xcf_ref.at[pl.ds(r0, qn)], dtf_ref.at[pl.ds(r0, qn)], a_ref.at[0], e_ref, hf_ref,
                           yf_ref.at[pl.ds(r0, qn)], False)
            _ssd_direction(xcb_ref.at[pl.ds(b0, qn)], dtb_ref.at[pl.ds(b0, qn)], a_ref.at[1], e_ref, hb_ref,
                           yb_ref.at[pl.ds(b0, qn)], True)
        return carry

    lax.fori_loop(0, lq // qn, body, 0, unroll=_ATTN_SSD_UNROLL if with_ssd else _ATTN_UNROLL)


def _attn_bias(dilation):
    slopes = jnp.exp2(-8.0 * jnp.arange(1, ATTN_HEADS + 1, dtype=F32) / ATTN_HEADS)
    rel = np.abs(np.arange(_ATTN_WIN)[None, :] - ATTN_HALF - np.arange(_ATTN_QB)[:, None])
    dist = jnp.asarray((rel * dilation).astype(np.float32))
    b = jnp.where(jnp.asarray(rel <= ATTN_HALF), -slopes[:, None, None] * dist, _NEG)
    return b.reshape(ATTN_HEADS // 2, 2 * _ATTN_QB, _ATTN_WIN)


def _dilated_attention(q, k, v, dilation, *, seq, ssd=None):
    t_total = q.shape[0]
    bsz = t_total // seq
    sub_len = seq // dilation
    lq = min(_ATTN_LQ, sub_len)
    nt = sub_len // lq
    hpb = lq // ATTN_HALF
    n_hblk = t_total // ATTN_HALF
    tile = lambda b, r, t: (b * dilation + r) * nt + t
    main = lambda b, r, t: (tile(b, r, t), 0)
    prev = lambda b, r, t: (jnp.maximum(tile(b, r, t) * hpb - 1, 0), 0)
    nxt = lambda b, r, t: (jnp.minimum((tile(b, r, t) + 1) * hpb, n_hblk - 1), 0)
    blk = pl.BlockSpec((lq, ATTN_WIDTH), main)
    hblk_p = pl.BlockSpec((ATTN_HALF, ATTN_WIDTH), prev)
    hblk_n = pl.BlockSpec((ATTN_HALF, ATTN_WIDTH), nxt)
    window = pltpu.VMEM((lq + 2 * ATTN_HALF, ATTN_WIDTH), BF16)
    args = [q, k, k, k, v, v, v, _attn_bias(dilation)]
    in_specs = [blk, blk, hblk_p, hblk_n, blk, hblk_p, hblk_n,
                _const_spec((ATTN_HEADS // 2, 2 * _ATTN_QB, _ATTN_WIN))]
    out_specs = [blk, pl.BlockSpec((lq, LANES), main)]
    out_shape = [jax.ShapeDtypeStruct((t_total, ATTN_WIDTH), BF16), jax.ShapeDtypeStruct((t_total, LANES), F32)]
    scratch = [window, window]
    semantics = ("parallel", "parallel", "parallel")
    if ssd is not None:
        assert dilation == 1 and _ATTN_QB == SSD_CHUNK
        xc, dt, a_log = ssd
        a_rows = jnp.zeros((2, 1, LANES), F32).at[:, 0, :SSD_HEADS].set(-jnp.exp(a_log))
        expand = (jnp.arange(LANES)[:, None] == jnp.arange(SSD_WIDTH)[None, :] // SSD_HEAD_DIM).astype(BF16)
        back = lambda b, r, t: (b * nt + nt - 1 - t, 0)
        args += [xc, xc, dt, dt, a_rows, expand]
        in_specs += [pl.BlockSpec((lq, XBC_WIDTH), main), pl.BlockSpec((lq, XBC_WIDTH), back),
                     pl.BlockSpec((lq, LANES), main),
                     pl.BlockSpec((lq, LANES), lambda b, r, t: (b * nt + nt - 1 - t, 1)),
                     _const_spec((2, 1, LANES)), _const_spec((LANES, SSD_WIDTH))]
        out_specs += [pl.BlockSpec((lq, SSD_WIDTH), main), pl.BlockSpec((lq, SSD_WIDTH), back)]
        out_shape += [jax.ShapeDtypeStruct((t_total, SSD_WIDTH), BF16)] * 2
        state = pltpu.VMEM((SSD_GROUPS, SSD_STATE, SSD_WIDTH // SSD_GROUPS), F32)
        scratch += [state, state]
        semantics = ("parallel", "arbitrary", "arbitrary")
    return pl.pallas_call(
        functools.partial(_attn_kernel, lq=lq, sub_len=sub_len, with_ssd=ssd is not None),
        grid=(bsz, dilation, nt),
        in_specs=in_specs,
        out_specs=out_specs,
        out_shape=out_shape,
        scratch_shapes=scratch,
        compiler_params=_cparams(*semantics),
        name=f"attn_d{dilation}" + ("_ssd" if ssd is not None else ""),
    )(*args)


def _mix_kernel(o0_ref, o1_ref, o2_ref, l0_ref, l1_ref, l2_ref, yf_ref, yb_ref, xs_ref, z_ref, x_ref,
                mod_ref, e_ref, dskip_ref, gssd_ref, wout_ref, gpost_ref, gpre_ref, wr_ref, br_ref,
                res_ref, h2_ref, route_ref, cnt_ref, lnat_ref, onat_ref, *, tm):
    n_slab = ATTN_WIDTH // LANES

    dilated = ((DILATIONS[1], l1_ref, o1_ref), (DILATIONS[2], l2_ref, o2_ref))
    for gi, (dil, l_ref, _) in enumerate(dilated):
        for rr in range(dil):
            lnat_ref[gi, pl.ds(rr, tm // dil, stride=dil), :] = l_ref[0, rr]

    l0, l1, l2 = l0_ref[...], lnat_ref[0], lnat_ref[1]
    m = jnp.maximum(jnp.maximum(l0, l1), l2)
    es = [jnp.exp(l - m) for l in (l0, l1, l2)]
    inv = 1.0 / (es[0] + es[1] + es[2])
    expand = e_ref[...]

    def widen(w):
        wh, wl = _split2(w)
        return _dot(wh, expand) + _dot(wl, expand)

    attn = widen(es[0] * inv) * o0_ref[...].astype(F32)
    for gi, (dil, _, o_ref) in enumerate(dilated):
        for rr in range(dil):
            for s in range(n_slab):
                onat_ref[s, pl.ds(rr, tm // dil, stride=dil), :] = (
                    o_ref[0, rr, :, s * LANES:(s + 1) * LANES].astype(F32))
        o_nat = jnp.concatenate([onat_ref[s] for s in range(n_slab)], axis=1)
        attn = attn + widen(es[gi + 1] * inv) * o_nat

    xs = xs_ref[...].astype(F32)
    z = z_ref[...].astype(F32)
    y = yf_ref[...].astype(F32) + yb_ref[...].astype(F32) + dskip_ref[...] * xs
    y = _rms(y * (z * _sigmoid(z))) * gssd_ref[...]
    mix = _dot(attn.astype(BF16), wout_ref[0:ATTN_WIDTH, :]) + _dot(y.astype(BF16), wout_ref[ATTN_WIDTH:, :])
    gate_m = mod_ref[0, 2:3, :]
    shift_f = mod_ref[0, 3:4, :]
    scale_f = mod_ref[0, 4:5, :]
    res = x_ref[...] + gate_m * (_rms(mix) * gpost_ref[...])
    res_ref[...] = res
    h2 = _rms(res) * gpre_ref[...] * (1.0 + scale_f) + shift_f
    h2_ref[...] = h2.astype(BF16)

    vals = _dot_f32(h2, wr_ref[...]) + br_ref[...]
    lane = lax.broadcasted_iota(jnp.int32, (tm, LANES), 1)
    sels, tops = [], []
    for _ in range(TOP_K):
        mx = jnp.max(vals, axis=-1, keepdims=True)
        idx = jnp.min(jnp.where(vals == mx, lane, LANES), axis=-1, keepdims=True)
        sel = lane == idx
        sels.append(sel)
        tops.append((mx, idx))
        vals = jnp.where(sel, -jnp.inf, vals)
    ex = [jnp.exp(tv - tops[0][0]) for tv, _ in tops]
    hit = jnp.zeros((tm, LANES), F32)
    for sel in sels:
        hit = jnp.where(sel, 1.0, hit)
    ri = lax.broadcasted_iota(jnp.int32, (tm, tm), 0)
    ci = lax.broadcasted_iota(jnp.int32, (tm, tm), 1)
    before = jnp.where(ci < ri, 1.0, 0.0).astype(BF16)
    rank = _dot(before, hit.astype(BF16))
    cnt = jnp.sum(hit, axis=0, keepdims=True)
    ei = lax.broadcasted_iota(jnp.int32, (LANES, LANES), 0)
    ej = lax.broadcasted_iota(jnp.int32, (LANES, LANES), 1)
    lower = jnp.where(ei < ej, 1.0, 0.0).astype(BF16)
    cnt_h, cnt_l = _split2(jnp.broadcast_to(cnt, (SUBLANES, LANES)))
    first_row = (_dot(cnt_h, lower) + _dot(cnt_l, lower))[0:1, :]
    route = jnp.zeros((tm, LANES), F32)
    for kk, sel in enumerate(sels):
        rk = jnp.sum(jnp.where(sel, rank, 0.0), axis=-1, keepdims=True)
        dest = jnp.sum(jnp.where(sel, rank + first_row, 0.0), axis=-1, keepdims=True)
        for base, val in ((0, tops[kk][1].astype(F32)), (TOP_K, dest), (2 * TOP_K, rk), (3 * TOP_K, ex[kk])):
            route = jnp.where(lane == base + kk, val, route)
    route_ref[...] = route
    cnt_ref[0] = jnp.broadcast_to(cnt, (SUBLANES, LANES)).astype(jnp.int32)


def _mix_and_route(outs, lses, yf, yb, xc, z, x2, mod, d_skip, g_ssd, w_out, g_post, g_pre_ffn,
                   w_router, b_router, *, seq, tm=256):
    t_total = x2.shape[0]
    nt = t_total // tm
    tiles_per_seq = seq // tm
    expand = (jnp.arange(LANES)[:, None] == jnp.arange(ATTN_WIDTH)[None, :] // HEAD_DIM).astype(BF16)
    wr = jnp.zeros((D_MODEL, LANES), F32).at[:, :N_EXPERTS].set(w_router)
    br = jnp.full((1, LANES), _NEG, F32).at[0, :N_EXPERTS].set(b_router)
    row = lambda i: (i, 0)
    wide = pl.BlockSpec((tm, D_MODEL), row)
    narrow = pl.BlockSpec((tm, LANES), row)
    vec = _const_spec((1, D_MODEL))
    bsz = t_total // seq

    def dilated(a, dil):
        a = a.reshape(bsz, dil, seq // dil, a.shape[-1])
        spec = pl.BlockSpec((1, dil, tm // dil, a.shape[-1]),
                            lambda i: (i // tiles_per_seq, 0, i % tiles_per_seq, 0))
        return a, spec

    o_args, o_specs = [outs[0]], [wide]
    l_args, l_specs = [lses[0]], [narrow]
    for dil, o, l in zip(DILATIONS[1:], outs[1:], lses[1:]):
        a, spec = dilated(o, dil)
        o_args.append(a)
        o_specs.append(spec)
        a, spec = dilated(l, dil)
        l_args.append(a)
        l_specs.append(spec)
    outs, lses = o_args, l_args
    return pl.pallas_call(
        functools.partial(_mix_kernel, tm=tm),
        grid=(nt,),
        in_specs=o_specs + l_specs + [wide] * 5 + [
            pl.BlockSpec((1, N_MOD, D_MODEL), lambda i: (i // tiles_per_seq, 0, 0)),
            _const_spec((LANES, ATTN_WIDTH)), vec, vec,
            _const_spec((ATTN_WIDTH + SSD_WIDTH, D_MODEL)), vec, vec,
            _const_spec((D_MODEL, LANES)), _const_spec((1, LANES))],
        out_specs=[wide, wide, narrow, pl.BlockSpec((1, SUBLANES, LANES), lambda i: (i, 0, 0))],
        out_shape=[jax.ShapeDtypeStruct((t_total, D_MODEL), F32),
                   jax.ShapeDtypeStruct((t_total, D_MODEL), BF16),
                   jax.ShapeDtypeStruct((t_total, LANES), F32),
                   jax.ShapeDtypeStruct((nt, SUBLANES, LANES), jnp.int32)],
        scratch_shapes=[pltpu.VMEM((len(DILATIONS) - 1, tm, LANES), F32),
                        pltpu.VMEM((ATTN_WIDTH // LANES, tm, LANES), F32)],
        compiler_params=_cparams("parallel"),
        name="mix_route",
    )(*outs, *lses, yf, yb, xc, z, x2, mod, expand, jnp.repeat(d_skip, SSD_HEAD_DIM).reshape(1, -1),
      g_ssd.reshape(1, -1), w_out.astype(BF16), g_post.reshape(1, -1), g_pre_ffn.reshape(1, -1), wr, br)


_MOE_TM = 256
_ROUTE_TM = 256
_MOE_CAP = 32
_BF16_ROWS = 2 * SUBLANES
_MOE_WIN = _MOE_CAP + _BF16_ROWS
_MOE_TAIL = _MOE_TM
_MOE_SLOTS = N_EXPERTS + _ROUTE_TM * TOP_K // _MOE_CAP
_MOE_KBLOCK = 16
_ROW_SPLIT = D_MODEL // LANES


def _store_rows(ref, lead, row0, vals):
    n = vals.shape[0]
    for q in range(_ROW_SPLIT):
        ref[lead, pl.ds(row0 * _ROW_SPLIT + q, n, stride=_ROW_SPLIT), :] = vals[:, q * LANES:(q + 1) * LANES]


def _load_rows(ref, lead, row0, n):
    parts = [ref[lead, pl.ds(row0 * _ROW_SPLIT + q, n, stride=_ROW_SPLIT), :] for q in range(_ROW_SPLIT)]
    return jnp.concatenate(parts, axis=1)


def _sorted_rows(t_total):
    return t_total * TOP_K + N_EXPERTS * _MOE_CAP + _MOE_TAIL


def _dispatch_kernel(src_ref, dst_ref, num_ref, nhalf_ref, pad_ref, route_ref, h_ref, xs_ref, stage_ref, sem,
                     *, nt):
    i = pl.program_id(0)
    slot = i % 2
    cap = _MOE_CAP
    n_pairs = _ROUTE_TM * TOP_K
    chunk_rows = cap * _ROW_SPLIT
    per_tile = _MOE_SLOTS + N_EXPERTS

    def chunk_copy(step, j, s, rows):
        src = pl.multiple_of(src_ref[step * per_tile + j] * _ROW_SPLIT, _ROW_SPLIT)
        dst = pl.multiple_of(dst_ref[step * per_tile + j] * _ROW_SPLIT, _ROW_SPLIT)
        return pltpu.make_async_copy(stage_ref.at[s, pl.ds(src, rows * _ROW_SPLIT)],
                                     xs_ref.at[pl.ds(dst, rows * _ROW_SPLIT)], sem.at[s])

    def for_chunks(step, s, fn):
        def full(j, carry):
            fn(chunk_copy(step, j, s, cap))
            return carry

        def half(j, carry):
            fn(chunk_copy(step, _MOE_SLOTS + j, s, cap // 2))
            return carry

        lax.fori_loop(0, num_ref[step], full, 0)
        lax.fori_loop(0, nhalf_ref[step], half, 0)

    @pl.when(i == 0)
    def _():
        for s in range(2):
            stage_ref[s, n_pairs * _ROW_SPLIT:, :] = jnp.zeros((chunk_rows, LANES), F32)

    @pl.when(i < nt)
    def _():
        pos_t = route_ref[...].T[TOP_K:2 * TOP_K, :]
        hb = h_ref[...]
        blk = _ROUTE_TM
        for b in range(n_pairs // blk):
            jrow = (lax.broadcasted_iota(jnp.int32, (blk, _ROUTE_TM), 0) + b * blk).astype(F32)
            onehot = jnp.zeros((blk, _ROUTE_TM), F32)
            for kk in range(TOP_K):
                onehot = onehot + jnp.where(pos_t[kk:kk + 1, :] == jrow, 1.0, 0.0)
            _store_rows(stage_ref, slot, b * blk, _dot(onehot.astype(BF16), hb))

    @pl.when(i > 0)
    def _():
        for_chunks(i - 1, 1 - slot, lambda cp: cp.wait())

    for_chunks(i, slot, lambda cp: cp.start())

    @pl.when(i == nt)
    def _():
        n_rows = xs_ref.shape[0] // _ROW_SPLIT
        zeros = stage_ref.at[slot, pl.ds(0, chunk_rows)]
        zeros[...] = jnp.zeros((chunk_rows, LANES), F32)
        fills = [pad_ref[e] for e in range(N_EXPERTS)]
        fills += [min(n_rows - _MOE_TAIL + j * cap, n_rows - cap) for j in range(-(-_MOE_TAIL // cap))]
        copies = [pltpu.make_async_copy(zeros, xs_ref.at[pl.ds(pl.multiple_of(r * _ROW_SPLIT, _ROW_SPLIT),
                                                               chunk_rows)], sem.at[slot]) for r in fills]
        for cp in copies:
            cp.start()
        for cp in copies:
            cp.wait()


def _dispatch(h2, route, lists):
    t_total = h2.shape[0]
    nt = t_total // _ROUTE_TM
    n_rows = _sorted_rows(t_total)
    tile = lambda i, *_: (jnp.minimum(i, nt - 1), 0)
    grid_spec = pltpu.PrefetchScalarGridSpec(
        num_scalar_prefetch=5,
        grid=(nt + 1,),
        in_specs=[pl.BlockSpec((_ROUTE_TM, LANES), tile),
                  pl.BlockSpec((_ROUTE_TM, D_MODEL), tile)],
        out_specs=pl.BlockSpec(memory_space=pl.ANY),
        scratch_shapes=[pltpu.VMEM((2, (_ROUTE_TM * TOP_K + _MOE_CAP) * _ROW_SPLIT, LANES), F32),
                        pltpu.SemaphoreType.DMA((2,))])
    return pl.pallas_call(
        functools.partial(_dispatch_kernel, nt=nt),
        grid_spec=grid_spec,
        out_shape=jax.ShapeDtypeStruct((n_rows * _ROW_SPLIT, LANES), F32),
        compiler_params=_cparams("arbitrary"),
        name="moe_dispatch",
    )(lists["stage_row"], lists["sorted_row"], lists["num_full"], lists["num_half"], lists["pad_row"], route, h2)


def _gmm_kernel(tile_ref, exp_ref, valid_ref, offs_ref, x_ref, wgu_hbm, bgu_ref, wdn_hbm, bdn_ref, y_ref,
                wgu_bf, wdn_bf, wgu_f32, wdn_f32, sem):
    j = pl.program_id(0)
    e = exp_ref[j]
    tile = tile_ref[j]
    prev_j = jnp.maximum(j - 1, 0)
    new_expert = (j == 0) | (exp_ref[prev_j] != e)
    new_tile = (j == 0) | (tile_ref[prev_j] != tile)

    def fetch(expert, s):
        return (pltpu.make_async_copy(wgu_hbm.at[expert], wgu_f32.at[s], sem.at[0, s]),
                pltpu.make_async_copy(wdn_hbm.at[expert], wdn_f32.at[s], sem.at[1, s]))

    @pl.when(j == 0)
    def _():
        for cp in fetch(e, e % 2):
            cp.start()

    @pl.when(new_expert)
    def _():
        s = e % 2
        for cp in fetch(e, s):
            cp.wait()
        wgu_bf[...] = wgu_f32[s].astype(BF16)
        wdn_bf[...] = wdn_f32[s].astype(BF16)

        @pl.when(e + 1 < N_EXPERTS)
        def _():
            for cp in fetch(e + 1, 1 - s):
                cp.start()

    @pl.when(valid_ref[j] != 0)
    def _():
        half = D_MODEL // 2
        gu = bgu_ref[0]
        for kc in range(2):
            parts = [x_ref[0, pl.ds(q, _MOE_TM, stride=_ROW_SPLIT), :]
                     for q in range(kc * _ROW_SPLIT // 2, (kc + 1) * _ROW_SPLIT // 2)]
            xk = jnp.concatenate(parts, axis=1).astype(BF16)
            gu = gu + _dot(xk, wgu_bf[kc * half:(kc + 1) * half, :])
        gate = jnp.minimum(gu[:, :EXPERT_FF], SWIGLU_LIMIT)
        up = jnp.clip(gu[:, EXPERT_FF:], -SWIGLU_LIMIT, SWIGLU_LIMIT)
        act = (up + 1.0) * gate * _sigmoid(SWIGLU_ALPHA * gate)
        y = (_dot(act.astype(BF16), wdn_bf[...]) + bdn_ref[0]).astype(BF16)
        rows = tile * _MOE_TM + lax.broadcasted_iota(jnp.int32, (_MOE_TM, 1), 0)
        mine = (rows >= offs_ref[e]) & (rows < offs_ref[e + 1])

        @pl.when(new_tile)
        def _():
            y_ref[...] = jnp.where(mine, y, jnp.zeros_like(y))

        @pl.when(jnp.logical_not(new_tile))
        def _():
            y_ref[...] = jnp.where(mine, y, y_ref[...])


def _grouped_mlp(xs, offs, w_gate_up, b_gate_up, w_down, b_down):
    n_tiles = xs.shape[0] // (_MOE_TM * _ROW_SPLIT)
    n_visits = n_tiles + N_EXPERTS - 1
    first = offs[:-1] // _MOE_TM
    last = (offs[1:] - 1) // _MOE_TM
    per_expert = last - first + 1
    vstart = jnp.concatenate([jnp.zeros((1,), jnp.int32), jnp.cumsum(per_expert)]).astype(jnp.int32)
    total = vstart[-1]
    vis = jnp.minimum(jnp.arange(n_visits, dtype=jnp.int32), total - 1)
    exp_ids = (jnp.sum(vstart[None, :] <= vis[:, None], axis=1) - 1).astype(jnp.int32)
    mine = exp_ids[:, None] == jnp.arange(N_EXPERTS, dtype=jnp.int32)[None, :]
    pick = lambda a: jnp.sum(jnp.where(mine, a[None, :], 0), axis=1)
    tile_ids = (pick(first) + vis - pick(vstart[:-1])).astype(jnp.int32)
    valid = (jnp.arange(n_visits) < total).astype(jnp.int32)
    grid_spec = pltpu.PrefetchScalarGridSpec(
        num_scalar_prefetch=4,
        grid=(n_visits,),
        in_specs=[pl.BlockSpec((1, _MOE_TM * _ROW_SPLIT, LANES), lambda j, t, e, v, o: (t[j], 0, 0)),
                  pl.BlockSpec(memory_space=pl.ANY),
                  pl.BlockSpec((1, 1, 2 * EXPERT_FF), lambda j, t, e, v, o: (e[j], 0, 0)),
                  pl.BlockSpec(memory_space=pl.ANY),
                  pl.BlockSpec((1, 1, D_MODEL), lambda j, t, e, v, o: (e[j], 0, 0))],
        out_specs=pl.BlockSpec((_MOE_TM, D_MODEL), lambda j, t, e, v, o: (t[j], 0)),
        scratch_shapes=[pltpu.VMEM((D_MODEL, 2 * EXPERT_FF), BF16), pltpu.VMEM((EXPERT_FF, D_MODEL), BF16),
                        pltpu.VMEM((2, D_MODEL, 2 * EXPERT_FF), F32), pltpu.VMEM((2, EXPERT_FF, D_MODEL), F32),
                        pltpu.SemaphoreType.DMA((2, 2))])
    tiled = (n_tiles, _MOE_TM * _ROW_SPLIT, LANES)
    return pl.pallas_call(
        _gmm_kernel,
        grid_spec=grid_spec,
        out_shape=jax.ShapeDtypeStruct((n_tiles * _MOE_TM, D_MODEL), BF16),
        compiler_params=_cparams("arbitrary"),
        name="moe_mlp",
    )(tile_ids, exp_ids, valid, offs, xs.reshape(tiled), w_gate_up, b_gate_up.reshape(N_EXPERTS, 1, -1),
      w_down, b_down.reshape(N_EXPERTS, 1, -1))


def _combine_kernel(win_ref, num_ref, ys_ref, route_ref, base_ref, res_ref, mod_ref, g_ref,
                    o_ref, ybuf_ref, acc_ref, sem, *, nt):
    i = pl.program_id(0)
    slot = i % 2
    cap = _MOE_CAP
    win = _MOE_WIN
    tm = _ROUTE_TM
    kcols = _MOE_KBLOCK * win

    def window_copy(step, j, s):
        src = pl.multiple_of(win_ref[step * _MOE_SLOTS + j], _BF16_ROWS)
        return pltpu.make_async_copy(ys_ref.at[pl.ds(src, win)],
                                     ybuf_ref.at[s, pl.ds(pl.multiple_of(j * win, _BF16_ROWS), win)], sem.at[s])

    def start_all(step, s):
        def body(j, carry):
            window_copy(step, j, s).start()
            return carry
        lax.fori_loop(0, num_ref[step], body, 0)

    @pl.when(i == 0)
    def _():
        ybuf_ref[...] = jnp.zeros_like(ybuf_ref)
        start_all(0, 0)

    @pl.when(i + 1 < nt)
    def _():
        start_all(i + 1, 1 - slot)

    def wait_own(j, carry):
        window_copy(i, j, slot).wait()
        return carry

    lax.fori_loop(0, num_ref[i], wait_own, 0)

    route = route_ref[...]
    lane = lax.broadcasted_iota(jnp.int32, (tm, LANES), 1).astype(F32)
    base = base_ref[0, 0:1, :]
    cols, wts = [], []
    for kk in range(TOP_K):
        expert = route[:, kk:kk + 1]
        rank = route[:, 2 * TOP_K + kk:2 * TOP_K + kk + 1]
        chunk = jnp.floor(rank * (1.0 / cap))
        first = jnp.sum(jnp.where(lane == expert, base, 0.0), axis=-1, keepdims=True)
        cols.append(first + chunk * win + (rank - chunk * cap))
        wts.append(route[:, 3 * TOP_K + kk:3 * TOP_K + kk + 1])

    for b in range(_MOE_SLOTS // _MOE_KBLOCK):
        @pl.when(b * _MOE_KBLOCK < num_ref[i])
        def _():
            col = (lax.broadcasted_iota(jnp.int32, (tm, kcols), 1) + b * kcols).astype(F32)
            pw = jnp.zeros((tm, kcols), F32)
            for kk in range(TOP_K):
                pw = pw + jnp.where(col == cols[kk], wts[kk], 0.0)
            part = _dot(pw.astype(BF16), ybuf_ref[slot, b * kcols:(b + 1) * kcols, :])
            if b == 0:
                acc_ref[...] = part
            else:
                acc_ref[...] += part

    ffn = acc_ref[...] * (1.0 / (wts[0] + wts[1] + wts[2] + wts[3]))
    gate_f = mod_ref[0, 5:6, :]
    o_ref[...] = res_ref[...] + gate_f * (_rms(ffn) * g_ref[...])


def _combine(ys, route, lists, res, mod, g_post_ffn, *, seq):
    t_total = res.shape[0]
    nt = t_total // _ROUTE_TM
    tiles_per_seq = seq // _ROUTE_TM
    row = lambda i, *_: (i, 0)
    grid_spec = pltpu.PrefetchScalarGridSpec(
        num_scalar_prefetch=2,
        grid=(nt,),
        in_specs=[pl.BlockSpec(memory_space=pl.ANY),
                  pl.BlockSpec((_ROUTE_TM, LANES), row),
                  pl.BlockSpec((1, SUBLANES, LANES), lambda i, *_: (i, 0, 0)),
                  pl.BlockSpec((_ROUTE_TM, D_MODEL), row),
                  pl.BlockSpec((1, N_MOD, D_MODEL), lambda i, *_: (i // tiles_per_seq, 0, 0)),
                  pl.BlockSpec((1, D_MODEL), lambda i, *_: (0, 0))],
        out_specs=pl.BlockSpec((_ROUTE_TM, D_MODEL), row),
        scratch_shapes=[pltpu.VMEM((2, _MOE_SLOTS * _MOE_WIN, D_MODEL), BF16),
                        pltpu.VMEM((_ROUTE_TM, D_MODEL), F32),
                        pltpu.SemaphoreType.DMA((2,))])
    return pl.pallas_call(
        functools.partial(_combine_kernel, nt=nt),
        grid_spec=grid_spec,
        out_shape=jax.ShapeDtypeStruct((t_total, D_MODEL), F32),
        compiler_params=_cparams("arbitrary"),
        name="moe_combine",
    )(lists["window_row"], lists["num"], ys, route, lists["window_base"], res, mod, g_post_ffn.reshape(1, -1))


def _chunk_lists(n_te, offs):
    nt = n_te.shape[0]
    cap = _MOE_CAP
    zero_col = jnp.zeros((nt, 1), jnp.int32)
    zero_row = jnp.zeros((1, N_EXPERTS), jnp.int32)
    before = jnp.concatenate([zero_row, jnp.cumsum(n_te, axis=0)[:-1]], axis=0)
    seg = offs[None, :N_EXPERTS] + before
    first = jnp.concatenate([zero_col, jnp.cumsum(n_te, axis=1)[:, :-1]], axis=1)
    experts = jnp.arange(N_EXPERTS, dtype=jnp.int32)

    def slot_lists(count, n_slots):
        base = jnp.concatenate([zero_col, jnp.cumsum(count, axis=1)], axis=1)
        slots = jnp.arange(n_slots, dtype=jnp.int32)
        owner = jnp.minimum(jnp.sum(base[:, 1:, None] <= slots[None, None, :], axis=1), N_EXPERTS - 1)
        pick = owner[:, None, :] == experts[None, :, None]
        take = lambda a: jnp.sum(jnp.where(pick, a[:, :, None], 0), axis=1)
        return base, take, slots[None, :] - take(base[:, :N_EXPERTS])

    as_i32 = lambda a: a.astype(jnp.int32).reshape(-1)
    with_end = lambda a: jnp.concatenate([a, jnp.zeros((1,), jnp.int32)])
    chunk_base, take, chunk = slot_lists((n_te + cap - 1) // cap, _MOE_SLOTS)
    window_row = (take(seg) + chunk * cap) // _BF16_ROWS * _BF16_ROWS
    window_base = chunk_base[:, :N_EXPERTS] * _MOE_WIN + seg % _BF16_ROWS
    window_base = jnp.zeros((nt, SUBLANES, LANES), F32).at[:, :, :N_EXPERTS].set(
        window_base.astype(F32)[:, None, :])
    rem = n_te % cap
    has_half = (rem > 0) & (rem <= cap // 2)
    n_full = n_te // cap + (rem > cap // 2)
    _, take_f, chunk_f = slot_lists(n_full, _MOE_SLOTS)
    half_base, take_h, _ = slot_lists(has_half.astype(jnp.int32), N_EXPERTS)
    stage_row = jnp.concatenate([take_f(first) + chunk_f * cap, take_h(first + n_full * cap)], axis=1)
    sorted_row = jnp.concatenate([take_f(seg) + chunk_f * cap, take_h(seg + n_full * cap)], axis=1)
    return {"stage_row": as_i32(stage_row),
            "sorted_row": as_i32(sorted_row),
            "num_full": as_i32(with_end(jnp.sum(n_full, axis=1))),
            "num_half": as_i32(with_end(half_base[:, -1])),
            "window_row": as_i32(window_row),
            "num": as_i32(with_end(chunk_base[:, -1])),
            "pad_row": as_i32(offs[1:] - cap).at[N_EXPERTS - 1].add(-_MOE_TAIL),
            "window_base": window_base}


def _forward(x, c, w_ada, b_ada, g_pre_mix, g_post_mix, w_in, conv_w, conv_b, dt_bias, a_log, d_skip,
             g_ssd_norm, w_out, g_pre_ffn, g_post_ffn, w_router, b_router, w_gate_up, b_gate_up, w_down, b_down):
    bsz, seq, _ = x.shape
    x2 = x.astype(F32).reshape(bsz * seq, D_MODEL)
    mod = _ada_mod(c.astype(F32), w_ada, b_ada)
    *qkv, z, xc, dt = _in_proj(x2, mod, g_pre_mix, w_in, conv_w, conv_b, dt_bias, seq=seq)
    attn = []
    for gi, dil in enumerate(DILATIONS):
        q, k, v = (a.reshape(bsz * seq, ATTN_WIDTH) for a in qkv[3 * gi:3 * gi + 3])
        if gi == 0:
            o, lse, yf, yb = _dilated_attention(q, k, v, dil, seq=seq, ssd=(xc, dt, a_log))
            attn.append((o, lse))
        else:
            attn.append(_dilated_attention(q, k, v, dil, seq=seq))
    res, h2, route, cnt = _mix_and_route(
        [o for o, _ in attn], [l for _, l in attn], yf, yb, xc, z, x2, mod, d_skip, g_ssd_norm, w_out,
        g_post_mix, g_pre_ffn, w_router, b_router, seq=seq, tm=_ROUTE_TM)
    n_te = cnt[:, 0, :N_EXPERTS]
    region = jnp.sum(n_te, axis=0) + _MOE_CAP
    region = region.at[N_EXPERTS - 1].add(_MOE_TAIL)
    offs = jnp.concatenate([jnp.zeros((1,), jnp.int32), jnp.cumsum(region)]).astype(jnp.int32)
    lists = _chunk_lists(n_te, offs)
    xs = _dispatch(h2, route, lists)
    ys = _grouped_mlp(xs, offs, w_gate_up, b_gate_up, w_down, b_down)
    out = _combine(ys, route, lists, res, mod, g_post_ffn, seq=seq)
    return out.reshape(bsz, seq, D_MODEL)


def kernel(x, c, w_ada, b_ada, g_pre_mix, g_post_mix, w_in, conv_w, conv_b, dt_bias, a_log, d_skip, g_ssd_norm, w_out, g_pre_ffn, g_post_ffn, w_router, b_router, w_gate_up, b_gate_up, w_down, b_down):
    layer = lambda t: t[0].astype(F32)
    out = _forward(x, c, layer(w_ada), layer(b_ada), layer(g_pre_mix), layer(g_post_mix), layer(w_in),
                   layer(conv_w), layer(conv_b), layer(dt_bias), layer(a_log), layer(d_skip),
                   layer(g_ssd_norm), layer(w_out), layer(g_pre_ffn), layer(g_post_ffn), layer(w_router),
                   layer(b_router), layer(w_gate_up), layer(b_gate_up), layer(w_down), layer(b_down))
    return out.astype(x.dtype)
```

```python
import functools

import numpy as np
import jax
import jax.numpy as jnp
from jax import lax
from jax.experimental import pallas as pl
from jax.experimental.pallas import tpu as pltpu

F32 = jnp.float32
BF16 = jnp.bfloat16

D_MODEL = 1024
ATTN_HEADS = 16
HEAD_DIM = 64
ATTN_WIDTH = ATTN_HEADS * HEAD_DIM
DILATIONS = (1, 4, 16)
ATTN_HALF = 64
SSD_HEADS = 16
SSD_HEAD_DIM = 64
SSD_WIDTH = SSD_HEADS * SSD_HEAD_DIM
SSD_GROUPS = 2
SSD_STATE = 128
SSD_CHUNK = 128
CONV_WIDTH = 5
XBC_WIDTH = SSD_WIDTH + 2 * SSD_GROUPS * SSD_STATE
N_EXPERTS = 32
TOP_K = 4
EXPERT_FF = 1024
SWIGLU_ALPHA = 1.702
SWIGLU_LIMIT = 7.0
RMS_EPS = 1e-6
N_MOD = 6
LANES = 128
SUBLANES = 8
VMEM_LIMIT = 56 * 1024 * 1024


def _cparams(*sem):
    return pltpu.CompilerParams(dimension_semantics=sem, vmem_limit_bytes=VMEM_LIMIT)


def _const_spec(shape):
    nd = len(shape)
    return pl.BlockSpec(shape, lambda *_: (0,) * nd)


def _split2(a):
    hi = a.astype(BF16)
    lo = (a - hi.astype(F32)).astype(BF16)
    return hi, lo


def _dot(a, b):
    return jnp.dot(a, b, preferred_element_type=F32)


def _dot_f32(a, b):
    ah, al = _split2(a)
    bh, bl = _split2(b)
    return _dot(ah, bh) + (_dot(ah, bl) + _dot(al, bh))


def _sigmoid(x):
    return 1.0 / (1.0 + jnp.exp(-x))


def _rms(x):
    return x * lax.rsqrt(jnp.mean(x * x, axis=-1, keepdims=True) + RMS_EPS)


def _ada_kernel(c_ref, w_ref, b_ref, o_ref):
    c = c_ref[...]
    o_ref[...] = _dot_f32(c * _sigmoid(c), w_ref[...]) + b_ref[...]


def _ada_mod(c, w_ada, b_ada):
    bsz = c.shape[0]
    c8 = jnp.zeros((SUBLANES, D_MODEL), F32).at[:bsz].set(c)
    out = pl.pallas_call(
        _ada_kernel,
        grid=(N_MOD,),
        in_specs=[_const_spec((SUBLANES, D_MODEL)),
                  pl.BlockSpec((D_MODEL, D_MODEL), lambda j: (0, j)),
                  pl.BlockSpec((1, D_MODEL), lambda j: (0, j))],
        out_specs=pl.BlockSpec((SUBLANES, D_MODEL), lambda j: (0, j)),
        out_shape=jax.ShapeDtypeStruct((SUBLANES, N_MOD * D_MODEL), F32),
        compiler_params=_cparams("parallel"),
        name="ada_mod",
    )(c8, w_ada, b_ada.reshape(1, -1))
    return out[:bsz].reshape(bsz, N_MOD, D_MODEL)


def _inproj_kernel(x_ref, xp_ref, xn_ref, mod_ref, g_ref, wqkv_ref, wz_ref, wxbc_ref, wdt_ref,
                   cw_ref, cb_ref, dtb_ref,
                   q1_ref, k1_ref, v1_ref, q4_ref, k4_ref, v4_ref, q16_ref, k16_ref, v16_ref,
                   z_ref, xc_ref, dt_ref, buf_ref, slab_ref, *, tm, tiles_per_seq):
    i = pl.program_id(0)
    shift = mod_ref[0, 0:1, :]
    scale = mod_ref[0, 1:2, :]
    g = g_ref[...]

    def norm_mod(x):
        return _rms(x) * g * (1.0 + scale) + shift

    h = norm_mod(x_ref[...])
    hb = h.astype(BF16)
    hh = norm_mod(jnp.concatenate([xp_ref[...], xn_ref[...]], axis=0)).astype(BF16)
    halo = _dot(hh, wxbc_ref[...])
    t_in_seq = i % tiles_per_seq
    prev_ok = (t_in_seq != 0).astype(F32)
    next_ok = (t_in_seq != tiles_per_seq - 1).astype(F32)
    buf_ref[0:SUBLANES, :] = halo[0:SUBLANES] * prev_ok
    buf_ref[SUBLANES + tm:, :] = halo[SUBLANES:] * next_ok
    buf_ref[SUBLANES:SUBLANES + tm, :] = _dot(hb, wxbc_ref[...])

    n_slab = ATTN_WIDTH // LANES
    outs = ((q1_ref, q4_ref, q16_ref), (k1_ref, k4_ref, k16_ref), (v1_ref, v4_ref, v16_ref))
    conv_cols = XBC_WIDTH // len(outs)
    for c, (nat_ref, d4_ref, d16_ref) in enumerate(outs):
        r = _dot(hb, wqkv_ref[:, c * ATTN_WIDTH:(c + 1) * ATTN_WIDTH])
        if c == 0:
            r = r * HEAD_DIM ** -0.5
        nat_ref[...] = r.astype(BF16)
        for s in range(n_slab):
            slab_ref[0, s] = r[:, s * LANES:(s + 1) * LANES]
        step = DILATIONS[1]
        sub = tm // step
        for r4 in range(step):
            for s in range(n_slab):
                rows = slab_ref[0, s, pl.ds(r4, sub, stride=step), :]
                d4_ref[0, r4, :, s * LANES:(s + 1) * LANES] = rows.astype(BF16)
                slab_ref[1, s, r4 * sub:(r4 + 1) * sub, :] = rows
        for r4 in range(step):
            for s4 in range(step):
                for s in range(n_slab):
                    rows = slab_ref[1, s, pl.ds(r4 * sub + s4, sub // step, stride=step), :]
                    d16_ref[0, step * s4 + r4, :, s * LANES:(s + 1) * LANES] = rows.astype(BF16)
        cols = slice(c * conv_cols, (c + 1) * conv_cols)
        acc = cb_ref[:, cols]
        for j in range(CONV_WIDTH):
            off = SUBLANES - CONV_WIDTH // 2 + j
            acc = acc + buf_ref[off:off + tm, cols] * cw_ref[j:j + 1, cols]
        xc_ref[:, cols] = (acc * _sigmoid(acc)).astype(BF16)
    z_ref[...] = _dot(hb, wz_ref[...]).astype(BF16)
    dt_ref[...] = jax.nn.softplus(_dot_f32(h, wdt_ref[...]) + dtb_ref[...])


def _in_proj(x2, mod, g_pre, w_in, conv_w, conv_b, dt_bias, *, seq, tm=512):
    t_total = x2.shape[0]
    nt = t_total // tm
    tiles_per_seq = seq // tm
    hb = tm // SUBLANES
    n_hblk = t_total // SUBLANES
    qkv_w = 3 * ATTN_WIDTH
    w_qkv = w_in[:, :qkv_w].astype(BF16)
    w_z = w_in[:, qkv_w:qkv_w + SSD_WIDTH].astype(BF16)
    w_xbc = w_in[:, qkv_w + SSD_WIDTH:qkv_w + SSD_WIDTH + XBC_WIDTH].astype(BF16)
    w_dt_raw = w_in[:, qkv_w + SSD_WIDTH + XBC_WIDTH:]
    w_dt = jnp.zeros((D_MODEL, 2 * LANES), F32)
    dtb = jnp.zeros((1, 2 * LANES), F32)
    for dr in range(2):
        w_dt = w_dt.at[:, dr * LANES:dr * LANES + SSD_HEADS].set(w_dt_raw[:, dr * SSD_HEADS:(dr + 1) * SSD_HEADS])
        dtb = dtb.at[0, dr * LANES:dr * LANES + SSD_HEADS].set(dt_bias[dr])
    row = lambda i: (i, 0)
    bsz = t_total // seq
    dil_shapes, dil_specs = [], []
    for dil in DILATIONS[1:]:
        dil_shapes += [jax.ShapeDtypeStruct((bsz, dil, seq // dil, ATTN_WIDTH), BF16)] * 3
        dil_specs += [pl.BlockSpec((1, dil, tm // dil, ATTN_WIDTH),
                                   lambda i: (i // tiles_per_seq, 0, i % tiles_per_seq, 0))] * 3
    out_shape = [jax.ShapeDtypeStruct((t_total, ATTN_WIDTH), BF16)] * 3 + dil_shapes + [
        jax.ShapeDtypeStruct((t_total, SSD_WIDTH), BF16),
        jax.ShapeDtypeStruct((t_total, XBC_WIDTH), BF16),
        jax.ShapeDtypeStruct((t_total, 2 * LANES), F32)]
    return pl.pallas_call(
        functools.partial(_inproj_kernel, tm=tm, tiles_per_seq=tiles_per_seq),
        grid=(nt,),
        in_specs=[pl.BlockSpec((tm, D_MODEL), row),
                  pl.BlockSpec((SUBLANES, D_MODEL), lambda i: (jnp.maximum(i * hb - 1, 0), 0)),
                  pl.BlockSpec((SUBLANES, D_MODEL), lambda i: (jnp.minimum((i + 1) * hb, n_hblk - 1), 0)),
                  pl.BlockSpec((1, N_MOD, D_MODEL), lambda i: (i // tiles_per_seq, 0, 0)),
                  _const_spec((1, D_MODEL)),
                  _const_spec((D_MODEL, qkv_w)),
                  _const_spec((D_MODEL, SSD_WIDTH)),
                  _const_spec((D_MODEL, XBC_WIDTH)),
                  _const_spec((D_MODEL, 2 * LANES)),
                  _const_spec((CONV_WIDTH, XBC_WIDTH)),
                  _const_spec((1, XBC_WIDTH)),
                  _const_spec((1, 2 * LANES))],
        out_specs=[pl.BlockSpec((tm, ATTN_WIDTH), row)] * 3 + dil_specs + [
            pl.BlockSpec((tm, SSD_WIDTH), row),
            pl.BlockSpec((tm, XBC_WIDTH), row),
            pl.BlockSpec((tm, 2 * LANES), row)],
        out_shape=out_shape,
        scratch_shapes=[pltpu.VMEM((tm + 2 * SUBLANES, XBC_WIDTH), F32),
                        pltpu.VMEM((2, ATTN_WIDTH // LANES, tm, LANES), F32)],
        compiler_params=_cparams("parallel"),
        name="in_proj",
    )(x2, x2, x2, mod, g_pre.reshape(1, -1), w_qkv, w_z, w_xbc, w_dt, conv_w, conv_b.reshape(1, -1), dtb)


_NEG = -1e30


def _ssd_direction(xc_ref, dt_ref, a_ref, e_ref, h_ref, y_ref, reverse):
    qn = SSD_CHUNK
    gw = SSD_WIDTH // SSD_GROUPS
    li = lax.broadcasted_iota(jnp.int32, (qn, qn), 0)
    si = lax.broadcasted_iota(jnp.int32, (qn, qn), 1)
    mask = (si >= li) if reverse else (si <= li)
    tri = jnp.where(mask, 1.0, 0.0).astype(BF16)
    dt = dt_ref[...]
    adt = dt * a_ref[...]
    p0 = adt.astype(BF16)
    r0 = adt - p0.astype(F32)
    p1 = r0.astype(BF16)
    p2 = (r0 - p1.astype(F32)).astype(BF16)
    acum = _dot(tri, p0) + (_dot(tri, p1) + _dot(tri, p2))
    last = 0 if reverse else qn - 1
    eo = jnp.exp(acum)
    ds = jnp.exp(acum[last:last + 1, :] - acum)
    dt_h, dt_l = _split2(dt)
    eo_h, eo_l = _split2(eo)
    ex = _dot(jnp.concatenate([dt_h, eo_h, ds.astype(BF16)], axis=0), e_ref[...])
    ex_l = _dot(jnp.concatenate([dt_l, eo_l], axis=0), e_ref[...])
    dt_x = ex[0:qn] + ex_l[0:qn]
    eo_x = ex[qn:2 * qn] + ex_l[qn:]
    ds_x = ex[2 * qn:]
    xd = xc_ref[:, 0:SSD_WIDTH].astype(F32) * dt_x
    xdb = xd.astype(BF16)
    xwb = (xd * ds_x).astype(BF16)
    acum_t = acum.T
    lane = lax.broadcasted_iota(jnp.int32, (qn, LANES), 1)
    for g in range(SSD_GROUPS):
        b0 = SSD_WIDTH + g * SSD_STATE
        c0 = SSD_WIDTH + SSD_GROUPS * SSD_STATE + g * SSD_STATE
        bm = xc_ref[:, b0:b0 + SSD_STATE]
        cm = xc_ref[:, c0:c0 + SSD_STATE]
        cb = lax.dot_general(cm, bm, (((1,), (1,)), ((), ())), preferred_element_type=F32)
        hg = h_ref[g]
        yoff = _dot(cm, hg.astype(BF16))
        st = lax.dot_general(bm, xwb[:, g * gw:(g + 1) * gw], (((0,), (0,)), ((), ())),
                             preferred_element_type=F32)
        h_ref[g] = hg * eo_x[last:last + 1, g * gw:(g + 1) * gw] + st
        for pr in range(gw // LANES):
            col = g * gw + pr * LANES
            xp = xdb[:, col:col + LANES]
            ms = []
            for e in (col // SSD_HEAD_DIM, col // SSD_HEAD_DIM + 1):
                seg = acum[:, e:e + 1] - acum_t[e:e + 1, :]
                ms.append((cb * jnp.exp(jnp.where(mask, seg, _NEG))).astype(BF16))
            zero = jnp.zeros_like(xp)
            xp2 = jnp.concatenate([jnp.where(lane < SSD_HEAD_DIM, xp, zero),
                                   jnp.where(lane < SSD_HEAD_DIM, zero, xp)], axis=0)
            yd = _dot(jnp.concatenate(ms, axis=1), xp2)
            y = yd + yoff[:, pr * LANES:(pr + 1) * LANES] * eo_x[:, col:col + LANES]
            y_ref[:, col:col + LANES] = y.astype(BF16)


_ATTN_QB = 2 * ATTN_HALF
_ATTN_WIN = _ATTN_QB + 2 * ATTN_HALF
_ATTN_LQ = 512
_ATTN_UNROLL = 4
_ATTN_SSD_UNROLL = 2


def _attn_kernel(*refs, lq, sub_len, with_ssd):
    if with_ssd:
        (q_ref, k_ref, kp_ref, kn_ref, v_ref, vp_ref, vn_ref, bias_ref, xcf_ref, xcb_ref, dtf_ref, dtb_ref,
         a_ref, e_ref, o_ref, lse_ref, yf_ref, yb_ref, kw_ref, vw_ref, hf_ref, hb_ref) = refs
    else:
        q_ref, k_ref, kp_ref, kn_ref, v_ref, vp_ref, vn_ref, bias_ref, o_ref, lse_ref, kw_ref, vw_ref = refs
    t = pl.program_id(2)
    if with_ssd:
        @pl.when(t == 0)
        def _():
            hf_ref[...] = jnp.zeros_like(hf_ref)
            hb_ref[...] = jnp.zeros_like(hb_ref)
    hb = ATTN_HALF
    for src, halo_p, halo_n, win in ((k_ref, kp_ref, kn_ref, kw_ref), (v_ref, vp_ref, vn_ref, vw_ref)):
        win[0:hb, :] = halo_p[...]
        win[hb:hb + lq, :] = src[...]
        win[hb + lq:, :] = halo_n[...]
    qn = _ATTN_QB
    lane = lax.broadcasted_iota(jnp.int32, (qn, LANES), 1)
    first_head = lane < HEAD_DIM
    kpos = lax.broadcasted_iota(jnp.int32, (1, _ATTN_WIN), 1)

    def body(qb, carry):
        r0 = pl.multiple_of(qb * qn, qn)
        kidx = t * lq + r0 - hb + kpos
        in_seq = (kidx >= 0) & (kidx < sub_len)
        lse_tile = jnp.zeros((qn, LANES), F32)
        for hp in range(ATTN_HEADS // 2):
            cs = slice(hp * LANES, (hp + 1) * LANES)
            q = q_ref[pl.ds(r0, qn), cs]
            zero = jnp.zeros_like(q)
            q2 = jnp.concatenate([jnp.where(first_head, q, zero), jnp.where(first_head, zero, q)], axis=0)
            s = lax.dot_general(q2, kw_ref[pl.ds(r0, _ATTN_WIN), cs], (((1,), (1,)), ((), ())),
                                preferred_element_type=F32) + bias_ref[hp]
            s = jnp.where(in_seq, s, _NEG)
            m = jnp.max(s, axis=-1, keepdims=True)
            p = jnp.exp(s - m)
            den = jnp.sum(p, axis=-1, keepdims=True)
            pv = _dot(p.astype(BF16), vw_ref[pl.ds(r0, _ATTN_WIN), cs]) * (1.0 / den)
            o_ref[pl.ds(r0, qn), cs] = jnp.where(first_head, pv[0:qn], pv[qn:]).astype(BF16)
            lse = m + jnp.log(den)
            lse_tile = jnp.where(lane == 2 * hp, lse[0:qn], jnp.where(lane == 2 * hp + 1, lse[qn:], lse_tile))
        lse_ref[pl.ds(r0, qn), :] = lse_tile
        if with_ssd:
            b0 = pl.multiple_of((lq // qn - 1 - qb) * qn, qn)
            _ssd_direction(xcf_ref.at[pl.ds(r0, qn)], dtf_ref.at[pl.ds(r0, qn)], a_ref.at[0], e_ref, hf_ref,
                           yf_ref.at[pl.ds(r0, qn)], False)
            _ssd_direction(xcb_ref.at[pl.ds(b0, qn)], dtb_ref.at[pl.ds(b0, qn)], a_ref.at[1], e_ref, hb_ref,
                           yb_ref.at[pl.ds(b0, qn)], True)
        return carry

    lax.fori_loop(0, lq // qn, body, 0, unroll=_ATTN_SSD_UNROLL if with_ssd else _ATTN_UNROLL)


def _attn_bias(dilation):
    slopes = jnp.exp2(-8.0 * jnp.arange(1, ATTN_HEADS + 1, dtype=F32) / ATTN_HEADS)
    rel = np.abs(np.arange(_ATTN_WIN)[None, :] - ATTN_HALF - np.arange(_ATTN_QB)[:, None])
    dist = jnp.asarray((rel * dilation).astype(np.float32))
    b = jnp.where(jnp.asarray(rel <= ATTN_HALF), -slopes[:, None, None] * dist, _NEG)
    return b.reshape(ATTN_HEADS // 2, 2 * _ATTN_QB, _ATTN_WIN)


def _dilated_attention(q, k, v, dilation, *, seq, ssd=None):
    t_total = q.shape[0]
    bsz = t_total // seq
    sub_len = seq // dilation
    lq = min(_ATTN_LQ, sub_len)
    nt = sub_len // lq
    hpb = lq // ATTN_HALF
    n_hblk = t_total // ATTN_HALF
    tile = lambda b, r, t: (b * dilation + r) * nt + t
    main = lambda b, r, t: (tile(b, r, t), 0)
    prev = lambda b, r, t: (jnp.maximum(tile(b, r, t) * hpb - 1, 0), 0)
    nxt = lambda b, r, t: (jnp.minimum((tile(b, r, t) + 1) * hpb, n_hblk - 1), 0)
    blk = pl.BlockSpec((lq, ATTN_WIDTH), main)
    hblk_p = pl.BlockSpec((ATTN_HALF, ATTN_WIDTH), prev)
    hblk_n = pl.BlockSpec((ATTN_HALF, ATTN_WIDTH), nxt)
    window = pltpu.VMEM((lq + 2 * ATTN_HALF, ATTN_WIDTH), BF16)
    args = [q, k, k, k, v, v, v, _attn_bias(dilation)]
    in_specs = [blk, blk, hblk_p, hblk_n, blk, hblk_p, hblk_n,
                _const_spec((ATTN_HEADS // 2, 2 * _ATTN_QB, _ATTN_WIN))]
    out_specs = [blk, pl.BlockSpec((lq, LANES), main)]
    out_shape = [jax.ShapeDtypeStruct((t_total, ATTN_WIDTH), BF16), jax.ShapeDtypeStruct((t_total, LANES), F32)]
    scratch = [window, window]
    semantics = ("parallel", "parallel", "parallel")
    if ssd is not None:
        assert dilation == 1 and _ATTN_QB == SSD_CHUNK
        xc, dt, a_log = ssd
        a_rows = jnp.zeros((2, 1, LANES), F32).at[:, 0, :SSD_HEADS].set(-jnp.exp(a_log))
        expand = (jnp.arange(LANES)[:, None] == jnp.arange(SSD_WIDTH)[None, :] // SSD_HEAD_DIM).astype(BF16)
        back = lambda b, r, t: (b * nt + nt - 1 - t, 0)
        args += [xc, xc, dt, dt, a_rows, expand]
        in_specs += [pl.BlockSpec((lq, XBC_WIDTH), main), pl.BlockSpec((lq, XBC_WIDTH), back),
                     pl.BlockSpec((lq, LANES), main),
                     pl.BlockSpec((lq, LANES), lambda b, r, t: (b * nt + nt - 1 - t, 1)),
                     _const_spec((2, 1, LANES)), _const_spec((LANES, SSD_WIDTH))]
        out_specs += [pl.BlockSpec((lq, SSD_WIDTH), main), pl.BlockSpec((lq, SSD_WIDTH), back)]
        out_shape += [jax.ShapeDtypeStruct((t_total, SSD_WIDTH), BF16)] * 2
        state = pltpu.VMEM((SSD_GROUPS, SSD_STATE, SSD_WIDTH // SSD_GROUPS), F32)
        scratch += [state, state]
        semantics = ("parallel", "arbitrary", "arbitrary")
    return pl.pallas_call(
        functools.partial(_attn_kernel, lq=lq, sub_len=sub_len, with_ssd=ssd is not None),
        grid=(bsz, dilation, nt),
        in_specs=in_specs,
        out_specs=out_specs,
        out_shape=out_shape,
        scratch_shapes=scratch,
        compiler_params=_cparams(*semantics),
        name=f"attn_d{dilation}" + ("_ssd" if ssd is not None else ""),
    )(*args)


def _mix_kernel(o0_ref, o1_ref, o2_ref, l0_ref, l1_ref, l2_ref, yf_ref, yb_ref, xs_ref, z_ref, x_ref,
                mod_ref, e_ref, dskip_ref, gssd_ref, wout_ref, gpost_ref, gpre_ref, wr_ref, br_ref,
                res_ref, h2_ref, route_ref, cnt_ref, lnat_ref, onat_ref, *, tm):
    n_slab = ATTN_WIDTH // LANES

    dilated = ((DILATIONS[1], l1_ref, o1_ref), (DILATIONS[2], l2_ref, o2_ref))
    for gi, (dil, l_ref, _) in enumerate(dilated):
        for rr in range(dil):
            lnat_ref[gi, pl.ds(rr, tm // dil, stride=dil), :] = l_ref[0, rr]

    l0, l1, l2 = l0_ref[...], lnat_ref[0], lnat_ref[1]
    m = jnp.maximum(jnp.maximum(l0, l1), l2)
    es = [jnp.exp(l - m) for l in (l0, l1, l2)]
    inv = 1.0 / (es[0] + es[1] + es[2])
    expand = e_ref[...]

    def widen(w):
        wh, wl = _split2(w)
        return _dot(wh, expand) + _dot(wl, expand)

    attn = widen(es[0] * inv) * o0_ref[...].astype(F32)
    for gi, (dil, _, o_ref) in enumerate(dilated):
        for rr in range(dil):
            for s in range(n_slab):
                onat_ref[s, pl.ds(rr, tm // dil, stride=dil), :] = (
                    o_ref[0, rr, :, s * LANES:(s + 1) * LANES].astype(F32))
        o_nat = jnp.concatenate([onat_ref[s] for s in range(n_slab)], axis=1)
        attn = attn + widen(es[gi + 1] * inv) * o_nat

    xs = xs_ref[...].astype(F32)
    z = z_ref[...].astype(F32)
    y = yf_ref[...].astype(F32) + yb_ref[...].astype(F32) + dskip_ref[...] * xs
    y = _rms(y * (z * _sigmoid(z))) * gssd_ref[...]
    mix = _dot(attn.astype(BF16), wout_ref[0:ATTN_WIDTH, :]) + _dot(y.astype(BF16), wout_ref[ATTN_WIDTH:, :])
    gate_m = mod_ref[0, 2:3, :]
    shift_f = mod_ref[0, 3:4, :]
    scale_f = mod_ref[0, 4:5, :]
    res = x_ref[...] + gate_m * (_rms(mix) * gpost_ref[...])
    res_ref[...] = res
    h2 = _rms(res) * gpre_ref[...] * (1.0 + scale_f) + shift_f
    h2_ref[...] = h2.astype(BF16)

    vals = _dot_f32(h2, wr_ref[...]) + br_ref[...]
    lane = lax.broadcasted_iota(jnp.int32, (tm, LANES), 1)
    sels, tops = [], []
    for _ in range(TOP_K):
        mx = jnp.max(vals, axis=-1, keepdims=True)
        idx = jnp.min(jnp.where(vals == mx, lane, LANES), axis=-1, keepdims=True)
        sel = lane == idx
        sels.append(sel)
        tops.append((mx, idx))
        vals = jnp.where(sel, -jnp.inf, vals)
    ex = [jnp.exp(tv - tops[0][0]) for tv, _ in tops]
    hit = jnp.zeros((tm, LANES), F32)
    for sel in sels:
        hit = jnp.where(sel, 1.0, hit)
    ri = lax.broadcasted_iota(jnp.int32, (tm, tm), 0)
    ci = lax.broadcasted_iota(jnp.int32, (tm, tm), 1)
    before = jnp.where(ci < ri, 1.0, 0.0).astype(BF16)
    rank = _dot(before, hit.astype(BF16))
    cnt = jnp.sum(hit, axis=0, keepdims=True)
    ei = lax.broadcasted_iota(jnp.int32, (LANES, LANES), 0)
    ej = lax.broadcasted_iota(jnp.int32, (LANES, LANES), 1)
    lower = jnp.where(ei < ej, 1.0, 0.0).astype(BF16)
    cnt_h, cnt_l = _split2(jnp.broadcast_to(cnt, (SUBLANES, LANES)))
    first_row = (_dot(cnt_h, lower) + _dot(cnt_l, lower))[0:1, :]
    route = jnp.zeros((tm, LANES), F32)
    for kk, sel in enumerate(sels):
        rk = jnp.sum(jnp.where(sel, rank, 0.0), axis=-1, keepdims=True)
        dest = jnp.sum(jnp.where(sel, rank + first_row, 0.0), axis=-1, keepdims=True)
        for base, val in ((0, tops[kk][1].astype(F32)), (TOP_K, dest), (2 * TOP_K, rk), (3 * TOP_K, ex[kk])):
            route = jnp.where(lane == base + kk, val, route)
    route_ref[...] = route
    cnt_ref[0] = jnp.broadcast_to(cnt, (SUBLANES, LANES)).astype(jnp.int32)


def _mix_and_route(outs, lses, yf, yb, xc, z, x2, mod, d_skip, g_ssd, w_out, g_post, g_pre_ffn,
                   w_router, b_router, *, seq, tm=256):
    t_total = x2.shape[0]
    nt = t_total // tm
    tiles_per_seq = seq // tm
    expand = (jnp.arange(LANES)[:, None] == jnp.arange(ATTN_WIDTH)[None, :] // HEAD_DIM).astype(BF16)
    wr = jnp.zeros((D_MODEL, LANES), F32).at[:, :N_EXPERTS].set(w_router)
    br = jnp.full((1, LANES), _NEG, F32).at[0, :N_EXPERTS].set(b_router)
    row = lambda i: (i, 0)
    wide = pl.BlockSpec((tm, D_MODEL), row)
    narrow = pl.BlockSpec((tm, LANES), row)
    vec = _const_spec((1, D_MODEL))
    bsz = t_total // seq

    def dilated(a, dil):
        a = a.reshape(bsz, dil, seq // dil, a.shape[-1])
        spec = pl.BlockSpec((1, dil, tm // dil, a.shape[-1]),
                            lambda i: (i // tiles_per_seq, 0, i % tiles_per_seq, 0))
        return a, spec

    o_args, o_specs = [outs[0]], [wide]
    l_args, l_specs = [lses[0]], [narrow]
    for dil, o, l in zip(DILATIONS[1:], outs[1:], lses[1:]):
        a, spec = dilated(o, dil)
        o_args.append(a)
        o_specs.append(spec)
        a, spec = dilated(l, dil)
        l_args.append(a)
        l_specs.append(spec)
    outs, lses = o_args, l_args
    return pl.pallas_call(
        functools.partial(_mix_kernel, tm=tm),
        grid=(nt,),
        in_specs=o_specs + l_specs + [wide] * 5 + [
            pl.BlockSpec((1, N_MOD, D_MODEL), lambda i: (i // tiles_per_seq, 0, 0)),
            _const_spec((LANES, ATTN_WIDTH)), vec, vec,
            _const_spec((ATTN_WIDTH + SSD_WIDTH, D_MODEL)), vec, vec,
            _const_spec((D_MODEL, LANES)), _const_spec((1, LANES))],
        out_specs=[wide, wide, narrow, pl.BlockSpec((1, SUBLANES, LANES), lambda i: (i, 0, 0))],
        out_shape=[jax.ShapeDtypeStruct((t_total, D_MODEL), F32),
                   jax.ShapeDtypeStruct((t_total, D_MODEL), BF16),
                   jax.ShapeDtypeStruct((t_total, LANES), F32),
                   jax.ShapeDtypeStruct((nt, SUBLANES, LANES), jnp.int32)],
        scratch_shapes=[pltpu.VMEM((len(DILATIONS) - 1, tm, LANES), F32),
                        pltpu.VMEM((ATTN_WIDTH // LANES, tm, LANES), F32)],
        compiler_params=_cparams("parallel"),
        name="mix_route",
    )(*outs, *lses, yf, yb, xc, z, x2, mod, expand, jnp.repeat(d_skip, SSD_HEAD_DIM).reshape(1, -1),
      g_ssd.reshape(1, -1), w_out.astype(BF16), g_post.reshape(1, -1), g_pre_ffn.reshape(1, -1), wr, br)


_MOE_TM = 256
_ROUTE_TM = 256
_MOE_CAP = 32
_BF16_ROWS = 2 * SUBLANES
_MOE_WIN = _MOE_CAP + _BF16_ROWS
_MOE_TAIL = _MOE_TM
_MOE_SLOTS = N_EXPERTS + _ROUTE_TM * TOP_K // _MOE_CAP
_MOE_KBLOCK = 16
_ROW_SPLIT = D_MODEL // LANES


def _store_rows(ref, lead, row0, vals):
    n = vals.shape[0]
    for q in range(_ROW_SPLIT):
        ref[lead, pl.ds(row0 * _ROW_SPLIT + q, n, stride=_ROW_SPLIT), :] = vals[:, q * LANES:(q + 1) * LANES]


def _load_rows(ref, lead, row0, n):
    parts = [ref[lead, pl.ds(row0 * _ROW_SPLIT + q, n, stride=_ROW_SPLIT), :] for q in range(_ROW_SPLIT)]
    return jnp.concatenate(parts, axis=1)


def _sorted_rows(t_total):
    return t_total * TOP_K + N_EXPERTS * _MOE_CAP + _MOE_TAIL


def _dispatch_kernel(src_ref, dst_ref, num_ref, nhalf_ref, pad_ref, route_ref, h_ref, xs_ref, stage_ref, sem,
                     *, nt):
    i = pl.program_id(0)
    slot = i % 2
    cap = _MOE_CAP
    n_pairs = _ROUTE_TM * TOP_K
    chunk_rows = cap * _ROW_SPLIT
    per_tile = _MOE_SLOTS + N_EXPERTS

    def chunk_copy(step, j, s, rows):
        src = pl.multiple_of(src_ref[step * per_tile + j] * _ROW_SPLIT, _ROW_SPLIT)
        dst = pl.multiple_of(dst_ref[step * per_tile + j] * _ROW_SPLIT, _ROW_SPLIT)
        return pltpu.make_async_copy(stage_ref.at[s, pl.ds(src, rows * _ROW_SPLIT)],
                                     xs_ref.at[pl.ds(dst, rows * _ROW_SPLIT)], sem.at[s])

    def for_chunks(step, s, fn):
        def full(j, carry):
            fn(chunk_copy(step, j, s, cap))
            return carry

        def half(j, carry):
            fn(chunk_copy(step, _MOE_SLOTS + j, s, cap // 2))
            return carry

        lax.fori_loop(0, num_ref[step], full, 0)
        lax.fori_loop(0, nhalf_ref[step], half, 0)

    @pl.when(i == 0)
    def _():
        for s in range(2):
            stage_ref[s, n_pairs * _ROW_SPLIT:, :] = jnp.zeros((chunk_rows, LANES), F32)

    @pl.when(i < nt)
    def _():
        pos_t = route_ref[...].T[TOP_K:2 * TOP_K, :]
        hb = h_ref[...]
        blk = _ROUTE_TM
        for b in range(n_pairs // blk):
            jrow = (lax.broadcasted_iota(jnp.int32, (blk, _ROUTE_TM), 0) + b * blk).astype(F32)
            onehot = jnp.zeros((blk, _ROUTE_TM), F32)
            for kk in range(TOP_K):
                onehot = onehot + jnp.where(pos_t[kk:kk + 1, :] == jrow, 1.0, 0.0)
            _store_rows(stage_ref, slot, b * blk, _dot(onehot.astype(BF16), hb))

    @pl.when(i > 0)
    def _():
        for_chunks(i - 1, 1 - slot, lambda cp: cp.wait())

    for_chunks(i, slot, lambda cp: cp.start())

    @pl.when(i == nt)
    def _():
        n_rows = xs_ref.shape[0] // _ROW_SPLIT
        zeros = stage_ref.at[slot, pl.ds(0, chunk_rows)]
        zeros[...] = jnp.zeros((chunk_rows, LANES), F32)
        fills = [pad_ref[e] for e in range(N_EXPERTS)]
        fills += [min(n_rows - _MOE_TAIL + j * cap, n_rows - cap) for j in range(-(-_MOE_TAIL // cap))]
        copies = [pltpu.make_async_copy(zeros, xs_ref.at[pl.ds(pl.multiple_of(r * _ROW_SPLIT, _ROW_SPLIT),
                                                               chunk_rows)], sem.at[slot]) for r in fills]
        for cp in copies:
            cp.start()
        for cp in copies:
            cp.wait()


def _dispatch(h2, route, lists):
    t_total = h2.shape[0]
    nt = t_total // _ROUTE_TM
    n_rows = _sorted_rows(t_total)
    tile = lambda i, *_: (jnp.minimum(i, nt - 1), 0)
    grid_spec = pltpu.PrefetchScalarGridSpec(
        num_scalar_prefetch=5,
        grid=(nt + 1,),
        in_specs=[pl.BlockSpec((_ROUTE_TM, LANES), tile),
                  pl.BlockSpec((_ROUTE_TM, D_MODEL), tile)],
        out_specs=pl.BlockSpec(memory_space=pl.ANY),
        scratch_shapes=[pltpu.VMEM((2, (_ROUTE_TM * TOP_K + _MOE_CAP) * _ROW_SPLIT, LANES), F32),
                        pltpu.SemaphoreType.DMA((2,))])
    return pl.pallas_call(
        functools.partial(_dispatch_kernel, nt=nt),
        grid_spec=grid_spec,
        out_shape=jax.ShapeDtypeStruct((n_rows * _ROW_SPLIT, LANES), F32),
        compiler_params=_cparams("arbitrary"),
        name="moe_dispatch",
    )(lists["stage_row"], lists["sorted_row"], lists["num_full"], lists["num_half"], lists["pad_row"], route, h2)


def _gmm_kernel(tile_ref, exp_ref, valid_ref, offs_ref, x_ref, wgu_hbm, bgu_ref, wdn_hbm, bdn_ref, y_ref,
                wgu_bf, wdn_bf, wgu_f32, wdn_f32, sem):
    j = pl.program_id(0)
    e = exp_ref[j]
    tile = tile_ref[j]
    prev_j = jnp.maximum(j - 1, 0)
    new_expert = (j == 0) | (exp_ref[prev_j] != e)
    new_tile = (j == 0) | (tile_ref[prev_j] != tile)

    def fetch(expert, s):
        return (pltpu.make_async_copy(wgu_hbm.at[expert], wgu_f32.at[s], sem.at[0, s]),
                pltpu.make_async_copy(wdn_hbm.at[expert], wdn_f32.at[s], sem.at[1, s]))

    @pl.when(j == 0)
    def _():
        for cp in fetch(e, e % 2):
            cp.start()

    @pl.when(new_expert)
    def _():
        s = e % 2
        for cp in fetch(e, s):
            cp.wait()
        wgu_bf[...] = wgu_f32[s].astype(BF16)
        wdn_bf[...] = wdn_f32[s].astype(BF16)

        @pl.when(e + 1 < N_EXPERTS)
        def _():
            for cp in fetch(e + 1, 1 - s):
                cp.start()

    @pl.when(valid_ref[j] != 0)
    def _():
        half = D_MODEL // 2
        gu = bgu_ref[0]
        for kc in range(2):
            parts = [x_ref[0, pl.ds(q, _MOE_TM, stride=_ROW_SPLIT), :]
                     for q in range(kc * _ROW_SPLIT // 2, (kc + 1) * _ROW_SPLIT // 2)]
            xk = jnp.concatenate(parts, axis=1).astype(BF16)
            gu = gu + _dot(xk, wgu_bf[kc * half:(kc + 1) * half, :])
        gate = jnp.minimum(gu[:, :EXPERT_FF], SWIGLU_LIMIT)
        up = jnp.clip(gu[:, EXPERT_FF:], -SWIGLU_LIMIT, SWIGLU_LIMIT)
        act = (up + 1.0) * gate * _sigmoid(SWIGLU_ALPHA * gate)
        y = (_dot(act.astype(BF16), wdn_bf[...]) + bdn_ref[0]).astype(BF16)
        rows = tile * _MOE_TM + lax.broadcasted_iota(jnp.int32, (_MOE_TM, 1), 0)
        mine = (rows >= offs_ref[e]) & (rows < offs_ref[e + 1])

        @pl.when(new_tile)
        def _():
            y_ref[...] = jnp.where(mine, y, jnp.zeros_like(y))

        @pl.when(jnp.logical_not(new_tile))
        def _():
            y_ref[...] = jnp.where(mine, y, y_ref[...])


def _grouped_mlp(xs, offs, w_gate_up, b_gate_up, w_down, b_down):
    n_tiles = xs.shape[0] // (_MOE_TM * _ROW_SPLIT)
    n_visits = n_tiles + N_EXPERTS - 1
    first = offs[:-1] // _MOE_TM
    last = (offs[1:] - 1) // _MOE_TM
    per_expert = last - first + 1
    vstart = jnp.concatenate([jnp.zeros((1,), jnp.int32), jnp.cumsum(per_expert)]).astype(jnp.int32)
    total = vstart[-1]
    vis = jnp.minimum(jnp.arange(n_visits, dtype=jnp.int32), total - 1)
    exp_ids = (jnp.sum(vstart[None, :] <= vis[:, None], axis=1) - 1).astype(jnp.int32)
    mine = exp_ids[:, None] == jnp.arange(N_EXPERTS, dtype=jnp.int32)[None, :]
    pick = lambda a: jnp.sum(jnp.where(mine, a[None, :], 0), axis=1)
    tile_ids = (pick(first) + vis - pick(vstart[:-1])).astype(jnp.int32)
    valid = (jnp.arange(n_visits) < total).astype(jnp.int32)
    grid_spec = pltpu.PrefetchScalarGridSpec(
        num_scalar_prefetch=4,
        grid=(n_visits,),
        in_specs=[pl.BlockSpec((1, _MOE_TM * _ROW_SPLIT, LANES), lambda j, t, e, v, o: (t[j], 0, 0)),
                  pl.BlockSpec(memory_space=pl.ANY),
                  pl.BlockSpec((1, 1, 2 * EXPERT_FF), lambda j, t, e, v, o: (e[j], 0, 0)),
                  pl.BlockSpec(memory_space=pl.ANY),
                  pl.BlockSpec((1, 1, D_MODEL), lambda j, t, e, v, o: (e[j], 0, 0))],
        out_specs=pl.BlockSpec((_MOE_TM, D_MODEL), lambda j, t, e, v, o: (t[j], 0)),
        scratch_shapes=[pltpu.VMEM((D_MODEL, 2 * EXPERT_FF), BF16), pltpu.VMEM((EXPERT_FF, D_MODEL), BF16),
                        pltpu.VMEM((2, D_MODEL, 2 * EXPERT_FF), F32), pltpu.VMEM((2, EXPERT_FF, D_MODEL), F32),
                        pltpu.SemaphoreType.DMA((2, 2))])
    tiled = (n_tiles, _MOE_TM * _ROW_SPLIT, LANES)
    return pl.pallas_call(
        _gmm_kernel,
        grid_spec=grid_spec,
        out_shape=jax.ShapeDtypeStruct((n_tiles * _MOE_TM, D_MODEL), BF16),
        compiler_params=_cparams("arbitrary"),
        name="moe_mlp",
    )(tile_ids, exp_ids, valid, offs, xs.reshape(tiled), w_gate_up, b_gate_up.reshape(N_EXPERTS, 1, -1),
      w_down, b_down.reshape(N_EXPERTS, 1, -1))


def _combine_kernel(win_ref, num_ref, ys_ref, route_ref, base_ref, res_ref, mod_ref, g_ref,
                    o_ref, ybuf_ref, acc_ref, sem, *, nt):
    i = pl.program_id(0)
    slot = i % 2
    cap = _MOE_CAP
    win = _MOE_WIN
    tm = _ROUTE_TM
    kcols = _MOE_KBLOCK * win

    def window_copy(step, j, s):
        src = pl.multiple_of(win_ref[step * _MOE_SLOTS + j], _BF16_ROWS)
        return pltpu.make_async_copy(ys_ref.at[pl.ds(src, win)],
                                     ybuf_ref.at[s, pl.ds(pl.multiple_of(j * win, _BF16_ROWS), win)], sem.at[s])

    def start_all(step, s):
        def body(j, carry):
            window_copy(step, j, s).start()
            return carry
        lax.fori_loop(0, num_ref[step], body, 0)

    @pl.when(i == 0)
    def _():
        ybuf_ref[...] = jnp.zeros_like(ybuf_ref)
        start_all(0, 0)

    @pl.when(i + 1 < nt)
    def _():
        start_all(i + 1, 1 - slot)

    def wait_own(j, carry):
        window_copy(i, j, slot).wait()
        return carry

    lax.fori_loop(0, num_ref[i], wait_own, 0)

    route = route_ref[...]
    lane = lax.broadcasted_iota(jnp.int32, (tm, LANES), 1).astype(F32)
    base = base_ref[0, 0:1, :]
    cols, wts = [], []
    for kk in range(TOP_K):
        expert = route[:, kk:kk + 1]
        rank = route[:, 2 * TOP_K + kk:2 * TOP_K + kk + 1]
        chunk = jnp.floor(rank * (1.0 / cap))
        first = jnp.sum(jnp.where(lane == expert, base, 0.0), axis=-1, keepdims=True)
        cols.append(first + chunk * win + (rank - chunk * cap))
        wts.append(route[:, 3 * TOP_K + kk:3 * TOP_K + kk + 1])

    def block(b):
        col = (lax.broadcasted_iota(jnp.int32, (tm, kcols), 1) + b * kcols).astype(F32)
        pw = jnp.zeros((tm, kcols), F32)
        for kk in range(TOP_K):
            pw = pw + jnp.where(col == cols[kk], wts[kk], 0.0)
        return _dot(pw.astype(BF16), ybuf_ref[slot, b * kcols:(b + 1) * kcols, :])

    sure = _ROUTE_TM * TOP_K // cap // _MOE_KBLOCK + 1
    part = block(0)
    for b in range(1, sure):
        part = part + block(b)
    acc_ref[...] = part
    for b in range(sure, _MOE_SLOTS // _MOE_KBLOCK):
        @pl.when(b * _MOE_KBLOCK < num_ref[i])
        def _():
            acc_ref[...] += block(b)

    ffn = acc_ref[...] * (1.0 / (wts[0] + wts[1] + wts[2] + wts[3]))
    gate_f = mod_ref[0, 5:6, :]
    o_ref[...] = res_ref[...] + gate_f * (_rms(ffn) * g_ref[...])


def _combine(ys, route, lists, res, mod, g_post_ffn, *, seq):
    t_total = res.shape[0]
    nt = t_total // _ROUTE_TM
    tiles_per_seq = seq // _ROUTE_TM
    row = lambda i, *_: (i, 0)
    grid_spec = pltpu.PrefetchScalarGridSpec(
        num_scalar_prefetch=2,
        grid=(nt,),
        in_specs=[pl.BlockSpec(memory_space=pl.ANY),
                  pl.BlockSpec((_ROUTE_TM, LANES), row),
                  pl.BlockSpec((1, SUBLANES, LANES), lambda i, *_: (i, 0, 0)),
                  pl.BlockSpec((_ROUTE_TM, D_MODEL), row),
                  pl.BlockSpec((1, N_MOD, D_MODEL), lambda i, *_: (i // tiles_per_seq, 0, 0)),
                  pl.BlockSpec((1, D_MODEL), lambda i, *_: (0, 0))],
        out_specs=pl.BlockSpec((_ROUTE_TM, D_MODEL), row),
        scratch_shapes=[pltpu.VMEM((2, _MOE_SLOTS * _MOE_WIN, D_MODEL), BF16),
                        pltpu.VMEM((_ROUTE_TM, D_MODEL), F32),
                        pltpu.SemaphoreType.DMA((2,))])
    return pl.pallas_call(
        functools.partial(_combine_kernel, nt=nt),
        grid_spec=grid_spec,
        out_shape=jax.ShapeDtypeStruct((t_total, D_MODEL), F32),
        compiler_params=_cparams("arbitrary"),
        name="moe_combine",
    )(lists["window_row"], lists["num"], ys, route, lists["window_base"], res, mod, g_post_ffn.reshape(1, -1))


def _chunk_lists(n_te, offs):
    nt = n_te.shape[0]
    cap = _MOE_CAP
    zero_col = jnp.zeros((nt, 1), jnp.int32)
    zero_row = jnp.zeros((1, N_EXPERTS), jnp.int32)
    before = jnp.concatenate([zero_row, jnp.cumsum(n_te, axis=0)[:-1]], axis=0)
    seg = offs[None, :N_EXPERTS] + before
    first = jnp.concatenate([zero_col, jnp.cumsum(n_te, axis=1)[:, :-1]], axis=1)
    experts = jnp.arange(N_EXPERTS, dtype=jnp.int32)

    def slot_lists(count, n_slots):
        base = jnp.concatenate([zero_col, jnp.cumsum(count, axis=1)], axis=1)
        slots = jnp.arange(n_slots, dtype=jnp.int32)
        owner = jnp.minimum(jnp.sum(base[:, 1:, None] <= slots[None, None, :], axis=1), N_EXPERTS - 1)
        pick = owner[:, None, :] == experts[None, :, None]
        take = lambda a: jnp.sum(jnp.where(pick, a[:, :, None], 0), axis=1)
        return base, take, slots[None, :] - take(base[:, :N_EXPERTS])

    as_i32 = lambda a: a.astype(jnp.int32).reshape(-1)
    with_end = lambda a: jnp.concatenate([a, jnp.zeros((1,), jnp.int32)])
    chunk_base, take, chunk = slot_lists((n_te + cap - 1) // cap, _MOE_SLOTS)
    window_row = (take(seg) + chunk * cap) // _BF16_ROWS * _BF16_ROWS
    window_base = chunk_base[:, :N_EXPERTS] * _MOE_WIN + seg % _BF16_ROWS
    window_base = jnp.zeros((nt, SUBLANES, LANES), F32).at[:, :, :N_EXPERTS].set(
        window_base.astype(F32)[:, None, :])
    rem = n_te % cap
    has_half = (rem > 0) & (rem <= cap // 2)
    n_full = n_te // cap + (rem > cap // 2)
    _, take_f, chunk_f = slot_lists(n_full, _MOE_SLOTS)
    half_base, take_h, _ = slot_lists(has_half.astype(jnp.int32), N_EXPERTS)
    stage_row = jnp.concatenate([take_f(first) + chunk_f * cap, take_h(first + n_full * cap)], axis=1)
    sorted_row = jnp.concatenate([take_f(seg) + chunk_f * cap, take_h(seg + n_full * cap)], axis=1)
    return {"stage_row": as_i32(stage_row),
            "sorted_row": as_i32(sorted_row),
            "num_full": as_i32(with_end(jnp.sum(n_full, axis=1))),
            "num_half": as_i32(with_end(half_base[:, -1])),
            "window_row": as_i32(window_row),
            "num": as_i32(with_end(chunk_base[:, -1])),
            "pad_row": as_i32(offs[1:] - cap).at[N_EXPERTS - 1].add(-_MOE_TAIL),
            "window_base": window_base}


def _forward(x, c, w_ada, b_ada, g_pre_mix, g_post_mix, w_in, conv_w, conv_b, dt_bias, a_log, d_skip,
             g_ssd_norm, w_out, g_pre_ffn, g_post_ffn, w_router, b_router, w_gate_up, b_gate_up, w_down, b_down):
    bsz, seq, _ = x.shape
    x2 = x.astype(F32).reshape(bsz * seq, D_MODEL)
    mod = _ada_mod(c.astype(F32), w_ada, b_ada)
    *qkv, z, xc, dt = _in_proj(x2, mod, g_pre_mix, w_in, conv_w, conv_b, dt_bias, seq=seq)
    attn = []
    for gi, dil in enumerate(DILATIONS):
        q, k, v = (a.reshape(bsz * seq, ATTN_WIDTH) for a in qkv[3 * gi:3 * gi + 3])
        if gi == 0:
            o, lse, yf, yb = _dilated_attention(q, k, v, dil, seq=seq, ssd=(xc, dt, a_log))
            attn.append((o, lse))
        else:
            attn.append(_dilated_attention(q, k, v, dil, seq=seq))
    res, h2, route, cnt = _mix_and_route(
        [o for o, _ in attn], [l for _, l in attn], yf, yb, xc, z, x2, mod, d_skip, g_ssd_norm, w_out,
        g_post_mix, g_pre_ffn, w_router, b_router, seq=seq, tm=_ROUTE_TM)
    n_te = cnt[:, 0, :N_EXPERTS]
    region = jnp.sum(n_te, axis=0) + _MOE_CAP
    region = region.at[N_EXPERTS - 1].add(_MOE_TAIL)
    offs = jnp.concatenate([jnp.zeros((1,), jnp.int32), jnp.cumsum(region)]).astype(jnp.int32)
    lists = _chunk_lists(n_te, offs)
    xs = _dispatch(h2, route, lists)
    ys = _grouped_mlp(xs, offs, w_gate_up, b_gate_up, w_down, b_down)
    out = _combine(ys, route, lists, res, mod, g_post_ffn, seq=seq)
    return out.reshape(bsz, seq, D_MODEL)


def kernel(x, c, w_ada, b_ada, g_pre_mix, g_post_mix, w_in, conv_w, conv_b, dt_bias, a_log, d_skip, g_ssd_norm, w_out, g_pre_ffn, g_post_ffn, w_router, b_router, w_gate_up, b_gate_up, w_down, b_down):
    layer = lambda t: t[0].astype(F32)
    out = _forward(x, c, layer(w_ada), layer(b_ada), layer(g_pre_mix), layer(g_post_mix), layer(w_in),
                   layer(conv_w), layer(conv_b), layer(dt_bias), layer(a_log), layer(d_skip),
                   layer(g_ssd_norm), layer(w_out), layer(g_pre_ffn), layer(g_post_ffn), layer(w_router),
                   layer(b_router), layer(w_gate_up), layer(b_gate_up), layer(w_down), layer(b_down))
    return out.astype(x.dtype)
```
